```python
import jax, jax.numpy as jnp
from jax import lax
import numpy as np

D_MODEL = 1024
BATCH = 16
SEQ = 2048
DEPTH = 2
DEC_BATCH = 32
DEC_SEQ = 8
PAST_LEN = 16384
PAGE_SIZE = 128

GLA_HEADS = 4
GLA_DK = D_MODEL // 2 // GLA_HEADS
GLA_DV = D_MODEL // GLA_HEADS
GLA_QK = GLA_HEADS * GLA_DK
GLA_VW = GLA_HEADS * GLA_DV
GLA_GATE_RANK = 16
GLA_GATE_NORM = 16.0
GLA_CHUNK = 64
POOL_WIDTH = D_MODEL
POOL_WINDOWS = (2, 4, 8, 16)
POOL_GROUP = POOL_WIDTH // 4
POOL_STATE = 15
ATT_GROUPS = ((128, 1), (512, 4), (2048, 16))
ATT_HEADS_PER_GROUP = 4
ATT_HEADS = 12
ATT_HEAD_DIM = 64
ATT_WIDTH = ATT_HEADS * ATT_HEAD_DIM
D_FF = ((8 * D_MODEL // 3 + 127) // 128) * 128
CONV_WIDTH = 3
NORM_EPS = 1e-6
NEG_INF = -1e30
IN_SPLITS = (GLA_QK, GLA_QK, GLA_VW, GLA_VW, GLA_GATE_RANK, POOL_WIDTH, ATT_WIDTH, ATT_WIDTH, ATT_WIDTH, D_MODEL, D_MODEL, D_MODEL)
IN_WIDTH = sum(IN_SPLITS)

kernel_name = 'hybrid_gla_pool_dilated_attn_step'


def _split_points():
    return [int(v) for v in np.cumsum(IN_SPLITS)[:-1]]


def alibi_slopes():
    return (2.0 ** (-8.0 * np.arange(1, ATT_HEADS + 1) / ATT_HEADS)).astype(np.float32)


def rmsnorm(x, g):
    xf = x.astype(jnp.float32)
    y = xf * lax.rsqrt(jnp.mean(xf * xf, axis=-1, keepdims=True) + NORM_EPS)
    return (y * g.astype(jnp.float32)).astype(x.dtype)


def gla_chunked(q, k, v, log_a, s0):
    f32 = jnp.float32
    bsz, t_len = q.shape[:2]
    c = GLA_CHUNK if t_len % GLA_CHUNK == 0 else t_len
    nc = t_len // c
    def chunks(z):
        return z.astype(f32).reshape(bsz, nc, c, *z.shape[2:])
    q, k, v, log_a = chunks(q), chunks(k), chunks(v), chunks(log_a)
    b = jnp.cumsum(log_a, axis=2)
    r = (c - 1) // 2
    b_ref = b[:, :, r:r + 1]
    b_last = b[:, :, -1:]
    causal = jnp.tril(jnp.ones((c, c), dtype=bool))
    att = jnp.einsum('bnthk,bnshk->bnhts', q * jnp.exp(b - b_ref), k * jnp.exp(b_ref - b))
    att = jnp.where(causal, att, 0.0)
    o_intra = jnp.einsum('bnhts,bnshv->bnthv', att, v)
    q_in = q * jnp.exp(b)
    k_st = k * jnp.exp(b_last - b)
    decay = jnp.exp(b_last[:, :, 0])
    def step(s, xs):
        qn, kn, vn, dn = xs
        on = jnp.einsum('bthk,bhkv->bthv', qn, s)
        s = dn[..., None] * s + jnp.einsum('bthk,bthv->bhkv', kn, vn)
        return s, on
    xs = (jnp.moveaxis(q_in, 1, 0), jnp.moveaxis(k_st, 1, 0), jnp.moveaxis(v, 1, 0), jnp.moveaxis(decay, 1, 0))
    s_fin, o_inter = lax.scan(step, s0.astype(f32), xs)
    o = o_intra + jnp.moveaxis(o_inter, 0, 1)
    return o.reshape(bsz, t_len, GLA_HEADS, GLA_DV), s_fin


def pool_mixer(u_ext, n_prev, pool_w, pool_scale):
    f32 = jnp.float32
    bsz, n, _ = u_ext.shape
    uf = u_ext.astype(f32)
    cs = jnp.concatenate([jnp.zeros((bsz, 1, POOL_WIDTH), f32), jnp.cumsum(uf, axis=1)], axis=1)
    rows = np.arange(n_prev, n)
    groups = []
    for gi, w in enumerate(POOL_WINDOWS):
        lo = np.maximum(rows + 1 - w, 0)
        cnt = (rows + 1 - lo).astype(np.float32)[None, :, None]
        sl = slice(gi * POOL_GROUP, (gi + 1) * POOL_GROUP)
        cg = cs[:, :, sl]
        mean = (cg[:, rows + 1] - cg[:, lo]) / cnt
        groups.append(mean - uf[:, n_prev:, sl])
    pooled = jnp.stack(groups, axis=2)
    mixed = jnp.einsum('btgc,gcd->btgd', pooled, pool_w.astype(f32))
    out = mixed.reshape(bsz, n - n_prev, POOL_WIDTH) * pool_scale.astype(f32)
    return out.astype(u_ext.dtype)


def _softmax_parts(s, axis):
    m = jnp.max(s, axis=axis, keepdims=True)
    p = jnp.exp(s - m)
    den = jnp.sum(p, axis=axis, keepdims=True)
    return p, den, m + jnp.log(den)


def dilated_attention_prompt(q, k, v, window, dil, slopes):
    f32 = jnp.float32
    bsz, s_len, hg, e = q.shape
    span = window // dil
    n = s_len // dil
    nb = -(-n // span)
    pad = nb * span - n
    def strided(z):
        z = z.astype(f32).reshape(bsz, n, dil, hg, e).transpose(0, 2, 1, 3, 4)
        return jnp.pad(z, ((0, 0), (0, 0), (0, pad), (0, 0), (0, 0)))
    qs, ks, vs = strided(q), strided(k), strided(v)
    qb = qs.reshape(bsz, dil, nb, span, hg, e)
    def band(z):
        zp = jnp.pad(z, ((0, 0), (0, 0), (span, 0), (0, 0), (0, 0)))
        prev = zp[:, :, :nb * span].reshape(bsz, dil, nb, span, hg, e)
        cur = z.reshape(bsz, dil, nb, span, hg, e)
        return jnp.concatenate([prev, cur], axis=3)
    kb, vb = band(ks), band(vs)
    s = jnp.einsum('bdnqhe,bdnkhe->bdnhqk', qb, kb) * (e ** -0.5)
    a_idx = np.arange(span)[:, None]
    c_idx = np.arange(2 * span)[None, :]
    j = a_idx - c_idx + span
    blk = np.arange(nb)[:, None, None]
    valid = (j >= 0) & (j <= span) & ((blk > 0) | (c_idx >= span))[...]
    bias = (-slopes[:, None, None] * (j * dil).astype(np.float32)[None]).astype(np.float32)
    s = jnp.where(valid[None, None, :, None], s + bias, NEG_INF)
    p, den, lse = _softmax_parts(s, -1)
    o = jnp.einsum('bdnhqk,bdnkhe->bdnqhe', p, vb) / jnp.swapaxes(den[..., 0], -1, -2)[..., None]
    lse = jnp.swapaxes(lse[..., 0], -1, -2)
    o = o.reshape(bsz, dil, nb * span, hg, e)[:, :, :n].transpose(0, 2, 1, 3, 4).reshape(bsz, s_len, hg, e)
    lse = lse.reshape(bsz, dil, nb * span, hg)[:, :, :n].transpose(0, 2, 1, 3).reshape(bsz, s_len, hg)
    return o, lse


def dilated_attention_sample(q, k_ext, v_ext, window, dil, slopes):
    f32 = jnp.float32
    bsz, l_len, hg, e = q.shape
    wb = k_ext.shape[1] - l_len
    span = window // dil
    jj = np.arange(span + 1)
    idx = wb + np.arange(l_len)[:, None] - jj[None, :] * dil
    valid = idx >= 0
    idxc = np.maximum(idx, 0)
    kg = k_ext.astype(f32)[:, idxc]
    vg = v_ext.astype(f32)[:, idxc]
    s = jnp.einsum('blhe,bljhe->blhj', q.astype(f32), kg) * (e ** -0.5)
    bias = (-slopes[:, None] * (jj * dil).astype(np.float32)[None, :]).astype(np.float32)
    s = jnp.where(valid[None, :, None, :], s + bias, NEG_INF)
    p, den, lse = _softmax_parts(s, -1)
    o = jnp.einsum('blhj,bljhe->blhe', p, vg) / den
    return o, lse[..., 0]


def token_mixer(xn, gla_prev, pool_prev, kv_prev, w_in, gla_wa2, gla_ba, gla_norm_g,
                pool_w, pool_scale, w_oa, w_ob, w_oc, w_out):
    f32 = jnp.float32
    dt = xn.dtype
    bsz, t_len, _ = xn.shape
    prompt = gla_prev is None
    (q_g, k_g, v_g, r_g, a_lr, u_pool, q_a, k_a, v_a,
     g_a, g_b, g_c) = jnp.split(xn @ w_in, _split_points(), axis=-1)
    q_g = q_g.reshape(bsz, t_len, GLA_HEADS, GLA_DK) * (GLA_DK ** -0.5)
    k_g = k_g.reshape(bsz, t_len, GLA_HEADS, GLA_DK)
    v_g = v_g.reshape(bsz, t_len, GLA_HEADS, GLA_DV)
    log_a = jax.nn.log_sigmoid((a_lr @ gla_wa2 + gla_ba).astype(f32)) / GLA_GATE_NORM
    log_a = log_a.reshape(bsz, t_len, GLA_HEADS, GLA_DK)
    s0 = jnp.zeros((bsz, GLA_HEADS, GLA_DK, GLA_DV), f32) if prompt else gla_prev
    o_g, gla_new = gla_chunked(q_g, k_g, v_g, log_a, s0)
    o_g = rmsnorm(o_g, gla_norm_g) * jax.nn.silu(r_g.reshape(bsz, t_len, GLA_HEADS, GLA_DV).astype(f32))
    y_a = o_g.reshape(bsz, t_len, GLA_VW).astype(dt) @ w_oa
    u_ext = u_pool if prompt else jnp.concatenate([pool_prev.astype(dt), u_pool], axis=1)
    n_prev = u_ext.shape[1] - t_len
    y_b = pool_mixer(u_ext, n_prev, pool_w, pool_scale) @ w_ob
    pool_new = u_ext[:, -POOL_STATE:]
    q_a = q_a.reshape(bsz, t_len, ATT_HEADS, ATT_HEAD_DIM)
    k_a = k_a.reshape(bsz, t_len, ATT_HEADS, ATT_HEAD_DIM)
    v_a = v_a.reshape(bsz, t_len, ATT_HEADS, ATT_HEAD_DIM)
    slopes = alibi_slopes()
    outs, lses, kv_new = [], [], []
    for gi, (win, dil) in enumerate(ATT_GROUPS):
        hs = slice(gi * ATT_HEADS_PER_GROUP, (gi + 1) * ATT_HEADS_PER_GROUP)
        qh, kh, vh = q_a[:, :, hs], k_a[:, :, hs], v_a[:, :, hs]
        if prompt:
            o_h, l_h = dilated_attention_prompt(qh, kh, vh, win, dil, slopes[hs])
            keep = min(win, t_len)
            kv_new += [kh[:, -keep:], vh[:, -keep:]]
        else:
            k_ext = jnp.concatenate([kv_prev[2 * gi].astype(dt), kh], axis=1)
            v_ext = jnp.concatenate([kv_prev[2 * gi + 1].astype(dt), vh], axis=1)
            o_h, l_h = dilated_attention_sample(qh, k_ext, v_ext, win, dil, slopes[hs])
            keep = kv_prev[2 * gi].shape[1]
            kv_new += [k_ext[:, -keep:], v_ext[:, -keep:]]
        outs.append(o_h)
        lses.append(l_h)
    alpha = jax.nn.softmax(jnp.stack(lses, axis=2), axis=2)
    o_c = jnp.concatenate([outs[gi] * alpha[:, :, gi, :, None] for gi in range(len(ATT_GROUPS))], axis=2)
    y_c = o_c.reshape(bsz, t_len, ATT_WIDTH).astype(dt) @ w_oc
    merged = jax.nn.sigmoid(g_a) * y_a + jax.nn.sigmoid(g_b) * y_b + jax.nn.sigmoid(g_c) * y_c
    return merged @ w_out, gla_new.astype(dt), pool_new, kv_new


def conv_ffn(xn, conv_prev, w_up, conv_w, conv_b, w_down):
    t_len = xn.shape[1]
    a, b = jnp.split(xn @ w_up, 2, axis=-1)
    ext = jnp.concatenate([conv_prev.astype(a.dtype), a], axis=1)
    y = conv_b + conv_w[0] * ext[:, 0:t_len] + conv_w[1] * ext[:, 1:t_len + 1] + conv_w[2] * ext[:, 2:t_len + 2]
    h = jax.nn.silu(y) * b
    return h @ w_down, ext[:, -(CONV_WIDTH - 1):]


def run_group(x, gla_st, pool_st, kv_st, conv_st, norm1_g, norm2_g, w_in, gla_wa2, gla_ba, gla_norm_g,
              pool_w, pool_scale, w_oa, w_ob, w_oc, w_out, ffn_w_up, ffn_conv_w, ffn_conv_b, ffn_w_down,
              final_norm_g):
    prompt = gla_st is None
    bsz = x.shape[0]
    new_gla, new_pool, new_kv, new_conv = [], [], [], []
    for l in range(DEPTH):
        kv_l = None if prompt else [c[l] for c in kv_st]
        mix, g_new, p_new, kv_new = token_mixer(
            rmsnorm(x, norm1_g[l]), None if prompt else gla_st[l], None if prompt else pool_st[l], kv_l,
            w_in[l], gla_wa2[l], gla_ba[l], gla_norm_g[l], pool_w[l], pool_scale[l],
            w_oa[l], w_ob[l], w_oc[l], w_out[l])
        h = x + mix
        conv_prev = jnp.zeros((bsz, CONV_WIDTH - 1, D_FF), x.dtype) if prompt else conv_st[l]
        f, c_new = conv_ffn(rmsnorm(h, norm2_g[l]), conv_prev, ffn_w_up[l], ffn_conv_w[l], ffn_conv_b[l], ffn_w_down[l])
        x = h + f
        new_gla.append(g_new)
        new_pool.append(p_new)
        new_kv.append(kv_new)
        new_conv.append(c_new)
    kv_out = [jnp.stack([kvl[i] for kvl in new_kv], axis=0) for i in range(2 * len(ATT_GROUPS))]
    return (rmsnorm(x, final_norm_g), jnp.stack(new_gla, 0), jnp.stack(new_pool, 0), kv_out,
            jnp.stack(new_conv, 0))


def setup_inputs(seed: int = 0) -> dict:
    key = jax.random.key(seed)
    ks = jax.random.split(key, 40)
    f32 = jnp.float32
    def rnd(i, shape, scale):
        return jax.random.normal(ks[i], shape, f32) * scale
    hg = ATT_HEADS_PER_GROUP
    wb = [min(w, PAST_LEN) for (w, _) in ATT_GROUPS]
    return {
        'x_prompt': rnd(0, (BATCH, SEQ, D_MODEL), 1.0),
        'x_sample': rnd(1, (DEC_BATCH, DEC_SEQ, D_MODEL), 1.0),
        'state_gla': rnd(2, (DEPTH, DEC_BATCH, GLA_HEADS, GLA_DK, GLA_DV), 1.0),
        'state_pool': rnd(3, (DEPTH, DEC_BATCH, POOL_STATE, POOL_WIDTH), 1.0),
        'cache_k_w128': rnd(4, (DEPTH, DEC_BATCH, wb[0], hg, ATT_HEAD_DIM), 1.0),
        'cache_v_w128': rnd(5, (DEPTH, DEC_BATCH, wb[0], hg, ATT_HEAD_DIM), 1.0),
        'cache_k_w512': rnd(6, (DEPTH, DEC_BATCH, wb[1], hg, ATT_HEAD_DIM), 1.0),
        'cache_v_w512': rnd(7, (DEPTH, DEC_BATCH, wb[1], hg, ATT_HEAD_DIM), 1.0),
        'cache_k_w2048': rnd(8, (DEPTH, DEC_BATCH, wb[2], hg, ATT_HEAD_DIM), 1.0),
        'cache_v_w2048': rnd(9, (DEPTH, DEC_BATCH, wb[2], hg, ATT_HEAD_DIM), 1.0),
        'state_ffn_conv': rnd(10, (DEPTH, DEC_BATCH, CONV_WIDTH - 1, D_FF), 1.0),
        'norm1_g': 1.0 + rnd(11, (DEPTH, D_MODEL), 0.05),
        'norm2_g': 1.0 + rnd(12, (DEPTH, D_MODEL), 0.05),
        'w_in': rnd(13, (DEPTH, D_MODEL, IN_WIDTH), D_MODEL ** -0.5),
        'gla_wa2': rnd(14, (DEPTH, GLA_GATE_RANK, GLA_QK), GLA_GATE_RANK ** -0.5),
        'gla_ba': rnd(15, (DEPTH, GLA_QK), 0.01),
        'gla_norm_g': 1.0 + rnd(16, (DEPTH, GLA_DV), 0.05),
        'pool_w': rnd(17, (DEPTH, 4, POOL_GROUP, POOL_GROUP), POOL_GROUP ** -0.5),
        'pool_scale': 1.0 + rnd(18, (DEPTH, POOL_WIDTH), 0.1),
        'w_oa': rnd(19, (DEPTH, GLA_VW, D_MODEL), GLA_VW ** -0.5),
        'w_ob': rnd(20, (DEPTH, POOL_WIDTH, D_MODEL), POOL_WIDTH ** -0.5),
        'w_oc': rnd(21, (DEPTH, ATT_WIDTH, D_MODEL), ATT_WIDTH ** -0.5),
        'w_out': rnd(22, (DEPTH, D_MODEL, D_MODEL), D_MODEL ** -0.5),
        'ffn_w_up': rnd(23, (DEPTH, D_MODEL, 2 * D_FF), D_MODEL ** -0.5),
        'ffn_conv_w': rnd(24, (DEPTH, CONV_WIDTH, D_FF), CONV_WIDTH ** -0.5),
        'ffn_conv_b': rnd(25, (DEPTH, D_FF), 0.01),
        'ffn_w_down': rnd(26, (DEPTH, D_FF, D_MODEL), D_FF ** -0.5),
        'final_norm_g': 1.0 + rnd(27, (D_MODEL,), 0.05),
    }


def reference(x_prompt, x_sample, state_gla, state_pool, cache_k_w128, cache_v_w128, cache_k_w512, cache_v_w512,
              cache_k_w2048, cache_v_w2048, state_ffn_conv, norm1_g, norm2_g, w_in, gla_wa2, gla_ba, gla_norm_g,
              pool_w, pool_scale, w_oa, w_ob, w_oc, w_out, ffn_w_up, ffn_conv_w, ffn_conv_b, ffn_w_down,
              final_norm_g):
    y_prompt, gla_p, pool_p, kv_p, conv_p = run_group(
        x_prompt, None, None, None, None, norm1_g, norm2_g, w_in, gla_wa2, gla_ba, gla_norm_g,
        pool_w, pool_scale, w_oa, w_ob, w_oc, w_out, ffn_w_up, ffn_conv_w, ffn_conv_b, ffn_w_down, final_norm_g)
    kv_in = [cache_k_w128, cache_v_w128, cache_k_w512, cache_v_w512, cache_k_w2048, cache_v_w2048]
    y_sample, gla_s, pool_s, kv_s, conv_s = run_group(
        x_sample, state_gla, state_pool, kv_in, state_ffn_conv, norm1_g, norm2_g, w_in, gla_wa2, gla_ba,
        gla_norm_g, pool_w, pool_scale, w_oa, w_ob, w_oc, w_out, ffn_w_up, ffn_conv_w, ffn_conv_b, ffn_w_down,
        final_norm_g)
    k128_p, v128_p, k512_p, v512_p, k2048_p, v2048_p = kv_p
    k128_s, v128_s, k512_s, v512_s, k2048_s, v2048_s = kv_s
    return (y_prompt, y_sample, gla_p, gla_s, pool_p, pool_s, k128_p, k128_s, v128_p, v128_s,
            k512_p, k512_s, v512_p, v512_s, k2048_p, k2048_s, v2048_p, v2048_s, conv_p, conv_s)
```

```python
import functools

import jax
import jax.numpy as jnp
import numpy as np
from jax import lax
from jax.experimental import pallas as pl
from jax.experimental.pallas import tpu as pltpu

F32 = jnp.float32
BF16 = jnp.bfloat16

D_MODEL = 1024
DEPTH = 2
GLA_HEADS = 4
GLA_DK = 128
GLA_DV = 256
GLA_QK = GLA_HEADS * GLA_DK
GLA_VW = GLA_HEADS * GLA_DV
GLA_GATE_RANK = 16
GLA_GATE_NORM = 16.0
GLA_CHUNK = 64
POOL_WINDOWS = (2, 4, 8, 16)
POOL_GROUP = 256
POOL_STATE = 15
POOL_HALO = 16
ATT_GROUPS = ((128, 1), (512, 4), (2048, 16))
ATT_HG = 4
ATT_E = 64
ATT_GW = ATT_HG * ATT_E
ATT_SPAN = 128
ATT_HEADS = 12
D_FF = 2816
NORM_EPS = 1e-6
NEG_INF = -1e30

LANE = 128
SUBLANE = 8
BF16_SUBLANE = 16
VMEM_LIMIT = 56 * 1024 * 1024

Z_QG, Z_KG, Z_VG, Z_RG, Z_U, Z_QA, Z_KA, Z_VA, Z_AL = 0, 512, 1024, 2048, 3072, 4096, 4864, 5632, 6400
NZ = 6656
_IN_SPLITS = (GLA_QK, GLA_QK, GLA_VW, GLA_VW, GLA_GATE_RANK, 1024, 768, 768, 768, 1024, 1024, 1024)
_IN_OFF = [0] + [int(v) for v in np.cumsum(_IN_SPLITS)]

_SLOPES = (2.0 ** (-8.0 * np.arange(1, ATT_HEADS + 1) / ATT_HEADS)).astype(np.float32)


def _params(sem):
    return pltpu.CompilerParams(dimension_semantics=sem, vmem_limit_bytes=VMEM_LIMIT)


def _dot(a, b):
    return jnp.dot(a, b, preferred_element_type=F32)


def _dot_nt(a, b):
    return lax.dot_general(a, b, (((1,), (1,)), ((), ())), preferred_element_type=F32)


def _dot_tn(a, b):
    return lax.dot_general(a, b, (((0,), (0,)), ((), ())), preferred_element_type=F32)


def _rms(x, g):
    return x * lax.rsqrt(jnp.mean(x * x, axis=-1, keepdims=True) + NORM_EPS) * g


def _sigmoid(x):
    return 1.0 / (1.0 + jnp.exp(-x))


def _in_proj_kernel(x_ref, g_ref, w_ref, z_ref, xn_ref):
    @pl.when(pl.program_id(1) == 0)
    def _():
        xn_ref[...] = _rms(x_ref[...], g_ref[...]).astype(BF16)

    z_ref[...] = _dot(xn_ref[...], w_ref[...])


def _in_proj(x2, g, wz, tm, tn):
    m = x2.shape[0]
    return pl.pallas_call(
        _in_proj_kernel,
        grid=(m // tm, NZ // tn),
        in_specs=[
            pl.BlockSpec((tm, D_MODEL), lambda i, j: (i, 0)),
            pl.BlockSpec((1, D_MODEL), lambda i, j: (0, 0)),
            pl.BlockSpec((D_MODEL, tn), lambda i, j: (0, j)),
        ],
        out_specs=pl.BlockSpec((tm, tn), lambda i, j: (i, j)),
        out_shape=jax.ShapeDtypeStruct((m, NZ), F32),
        scratch_shapes=[pltpu.VMEM((tm, D_MODEL), BF16)],
        compiler_params=_params(("parallel", "arbitrary")),
        name="in_proj",
    )(x2, g, wz)


def _gla_kernel(q_ref, k_ref, v_ref, r_ref, a_ref, wa_ref, ba_ref, gn_ref, s0_ref, o_ref, so_ref, st_ref,
                *, c, nc):
    ce = max(c, BF16_SUBLANE)
    mid = (c - 1) // 2
    st_ref[...] = s0_ref[...].T
    rowi = lax.broadcasted_iota(jnp.int32, (ce, GLA_DK), 0)
    causal = (lax.broadcasted_iota(jnp.int32, (ce, ce), 0) >= lax.broadcasted_iota(jnp.int32, (ce, ce), 1))

    def load(ref, sl):
        x = ref[sl, :]
        if ce != c:
            x = jnp.concatenate([x, jnp.zeros((ce - c, x.shape[1]), x.dtype)], axis=0)
        return x

    def chunk(n, carry):
        sl = pl.ds(pl.multiple_of(n * c, c), c)
        z = _dot(load(a_ref, sl).astype(BF16), wa_ref[...]) + ba_ref[...]
        la = (jnp.minimum(z, 0.0) - jnp.log1p(jnp.exp(-jnp.abs(z)))) * (1.0 / GLA_GATE_NORM)
        if ce != c:
            la = jnp.where(rowi < c, la, 0.0)
        b = la
        s = 1
        while s < c:
            b = b + jnp.where(rowi >= s, pltpu.roll(b, s, axis=0), 0.0)
            s *= 2
        b_mid = b[mid:mid + 1, :]
        b_last = b[c - 1:c, :]
        q = load(q_ref, sl) * (GLA_DK ** -0.5)
        k = load(k_ref, sl)
        v = load(v_ref, sl).astype(BF16)
        qe = (q * jnp.exp(b - b_mid)).astype(BF16)
        ke = (k * jnp.exp(b_mid - b)).astype(BF16)
        att = jnp.where(causal, _dot_nt(qe, ke), 0.0).astype(BF16)
        st = st_ref[...]
        o = _dot(att, v) + _dot_nt((q * jnp.exp(b)).astype(BF16), st.astype(BF16))
        ks = (k * jnp.exp(b_last - b)).astype(BF16)
        st_ref[...] = st * jnp.exp(b_last) + _dot_tn(v, ks)
        on = _rms(o, gn_ref[...])
        r = load(r_ref, sl)
        o_ref[sl, :] = (on * (r * _sigmoid(r)))[:c].astype(o_ref.dtype)
        return carry

    lax.fori_loop(0, nc, chunk, 0)
    so_ref[...] = st_ref[...].T


def _gla(z3, wa2p, ba, gn, s0, t_len):
    bsz = z3.shape[0]
    c = GLA_CHUNK if t_len % GLA_CHUNK == 0 else t_len
    kern = functools.partial(_gla_kernel, c=c, nc=t_len // c)
    zspec = lambda w, off: pl.BlockSpec((None, t_len, w), lambda b, h: (b, 0, off // w + h))
    return pl.pallas_call(
        kern,
        grid=(bsz, GLA_HEADS),
        in_specs=[
            zspec(GLA_DK, Z_QG), zspec(GLA_DK, Z_KG), zspec(GLA_DV, Z_VG), zspec(GLA_DV, Z_RG),
            pl.BlockSpec((None, t_len, LANE), lambda b, h: (b, 0, Z_AL // LANE)),
            pl.BlockSpec((LANE, GLA_DK), lambda b, h: (0, h)),
            pl.BlockSpec((1, GLA_DK), lambda b, h: (0, h)),
            pl.BlockSpec((1, GLA_DV), lambda b, h: (0, 0)),
            pl.BlockSpec((None, None, GLA_DK, GLA_DV), lambda b, h: (b, h, 0, 0)),
        ],
        out_specs=[
            pl.BlockSpec((None, t_len, GLA_DV), lambda b, h: (b, 0, h)),
            pl.BlockSpec((None, None, GLA_DK, GLA_DV), lambda b, h: (b, h, 0, 0)),
        ],
        out_shape=[
            jax.ShapeDtypeStruct((bsz, t_len, GLA_VW), F32),
            jax.ShapeDtypeStruct((bsz, GLA_HEADS, GLA_DK, GLA_DV), F32),
        ],
        scratch_shapes=[pltpu.VMEM((GLA_DV, GLA_DK), F32)],
        compiler_params=_params(("parallel", "arbitrary")),
        name="gla",
    )(z3, z3, z3, z3, z3, wa2p, ba, gn, s0)


def _pool_kernel(u_ref, prev_ref, pw_ref, ps_ref, p_ref, pn_ref, *, t_len, n_prev):
    ext = jnp.concatenate([prev_ref[...], u_ref[...]], axis=0)
    t_abs = lax.broadcasted_iota(jnp.int32, (t_len, 1), 0) + n_prev
    for gi, w in enumerate(POOL_WINDOWS):
        cols = slice(gi * POOL_GROUP, (gi + 1) * POOL_GROUP)
        x = ext[:, cols]
        acc = x
        s = 1
        while s < w:
            acc = acc + pltpu.roll(acc, s, axis=0)
            s *= 2
        cnt = jnp.minimum(t_abs + 1, w).astype(F32)
        pooled = acc[POOL_HALO:] / cnt - x[POOL_HALO:]
        mixed = _dot(pooled.astype(BF16), pw_ref[gi]) * ps_ref[:, cols]
        p_ref[:, cols] = mixed.astype(p_ref.dtype)
    keep = max(0, POOL_STATE - t_len)
    if keep:
        pn_ref[0:keep, :] = prev_ref[POOL_HALO - keep:POOL_HALO, :]
    pn_ref[keep:POOL_STATE, :] = u_ref[t_len - (POOL_STATE - keep):t_len, :]


def _pool(z3, prev, pw, ps, t_len, n_prev):
    bsz = z3.shape[0]
    kern = functools.partial(_pool_kernel, t_len=t_len, n_prev=n_prev)
    return pl.pallas_call(
        kern,
        grid=(bsz,),
        in_specs=[
            pl.BlockSpec((None, t_len, D_MODEL), lambda b: (b, 0, Z_U // D_MODEL)),
            pl.BlockSpec((None, POOL_HALO, D_MODEL), lambda b: (b, 0, 0)),
            pl.BlockSpec((4, POOL_GROUP, POOL_GROUP), lambda b: (0, 0, 0)),
            pl.BlockSpec((1, D_MODEL), lambda b: (0, 0)),
        ],
        out_specs=[
            pl.BlockSpec((None, t_len, D_MODEL), lambda b: (b, 0, 0)),
            pl.BlockSpec((None, POOL_STATE, D_MODEL), lambda b: (b, 0, 0)),
        ],
        out_shape=[
            jax.ShapeDtypeStruct((bsz, t_len, D_MODEL), F32),
            jax.ShapeDtypeStruct((bsz, POOL_STATE, D_MODEL), F32),
        ],
        compiler_params=_params(("parallel",)),
        name="pool",
    )(z3, prev, pw, ps)


def _head_masks(rows):
    lane = lax.broadcasted_iota(jnp.int32, (rows, ATT_GW), 1)
    return [(lane >= h * ATT_E) & (lane < (h + 1) * ATT_E) for h in range(ATT_HG)]


def _attn_prompt_kernel(q_ref, k_ref, v_ref, o_ref, l_ref, kc_ref, vc_ref, *, n, dil, slopes):
    nb = n // ATT_SPAN
    hm = _head_masks(ATT_SPAN)
    kc_ref[...] = k_ref[n - ATT_SPAN:n, :]
    vc_ref[...] = v_ref[n - ATT_SPAN:n, :]

    def block(qsl, ksl, nk):
        q = q_ref[qsl, :] * (ATT_E ** -0.5)
        kk = k_ref[ksl, :].astype(BF16)
        vv = v_ref[ksl, :].astype(BF16)
        a_idx = lax.broadcasted_iota(jnp.int32, (ATT_SPAN, nk), 0)
        c_idx = lax.broadcasted_iota(jnp.int32, (ATT_SPAN, nk), 1)
        j = a_idx - c_idx + (nk - ATT_SPAN)
        valid = (j >= 0) & (j <= ATT_SPAN)
        jf = j.astype(F32)
        o_acc = jnp.zeros((ATT_SPAN, ATT_GW), F32)
        l_acc = jnp.zeros((ATT_SPAN, ATT_GW), F32)
        for h in range(ATT_HG):
            qm = jnp.where(hm[h], q, 0.0).astype(BF16)
            s = _dot_nt(qm, kk) + jf * float(-slopes[h] * dil)
            s = jnp.where(valid, s, NEG_INF)
            m = jnp.max(s, axis=-1, keepdims=True)
            p = jnp.exp(s - m)
            den = jnp.sum(p, axis=-1, keepdims=True)
            oh = _dot(p.astype(BF16), vv) / den
            o_acc = jnp.where(hm[h], oh, o_acc)
            l_acc = jnp.where(hm[h], m + jnp.log(den), l_acc)
        o_ref[qsl, :] = o_acc
        l_ref[qsl, :] = l_acc

    block(pl.ds(0, ATT_SPAN), pl.ds(0, ATT_SPAN), ATT_SPAN)

    if nb > 1:
        def body(qi, carry):
            q0 = pl.multiple_of(qi * ATT_SPAN, ATT_SPAN)
            block(pl.ds(q0, ATT_SPAN), pl.ds(q0 - ATT_SPAN, 2 * ATT_SPAN), 2 * ATT_SPAN)
            return carry
        lax.fori_loop(1, nb, body, 0)


def _attn_prompt(z, bsz, t_len, gi):
    win, dil = ATT_GROUPS[gi]
    n = t_len // dil
    zv = z.reshape(bsz, n, dil * NZ)
    nzb = NZ // ATT_GW
    kern = functools.partial(_attn_prompt_kernel, n=n, dil=dil,
                             slopes=[float(s) for s in _SLOPES[gi * ATT_HG:(gi + 1) * ATT_HG]])
    zspec = lambda off: pl.BlockSpec((None, n, ATT_GW), lambda b, r: (b, 0, r * nzb + off // ATT_GW + gi))
    rspec = lambda rows: pl.BlockSpec((None, rows, ATT_GW), lambda b, r: (b, 0, r))
    keep = min(win, t_len) // dil
    o, l, kc, vc = pl.pallas_call(
        kern,
        grid=(bsz, dil),
        in_specs=[zspec(Z_QA), zspec(Z_KA), zspec(Z_VA)],
        out_specs=[rspec(n), rspec(n), rspec(keep), rspec(keep)],
        out_shape=[
            jax.ShapeDtypeStruct((bsz, n, dil * ATT_GW), F32),
            jax.ShapeDtypeStruct((bsz, n, dil * ATT_GW), F32),
            jax.ShapeDtypeStruct((bsz, keep, dil * ATT_GW), F32),
            jax.ShapeDtypeStruct((bsz, keep, dil * ATT_GW), F32),
        ],
        compiler_params=_params(("parallel", "parallel")),
        name=f"attn_prompt_g{gi}",
    )(zv, zv, zv)
    m = bsz * t_len
    return (o.reshape(m, ATT_GW), l.reshape(m, ATT_GW),
            kc.reshape(bsz, keep * dil, ATT_HG, ATT_E), vc.reshape(bsz, keep * dil, ATT_HG, ATT_E))


_KX_PAD = LANE


def _attn_sample_kernel(*refs, t_len, layer_slopes):
    ng = len(ATT_GROUPS)
    qkv = refs[0:3 * ng]
    caches = refs[3 * ng:5 * ng]
    outs = refs[5 * ng:]
    o_refs, l_refs, cache_out = outs[0:ng], outs[ng:2 * ng], outs[2 * ng:4 * ng]
    kx_ref, vx_ref = outs[4 * ng:]
    hm = _head_masks(t_len)
    rows = ATT_HG * t_len
    for gi, (win, dil) in enumerate(ATT_GROUPS):
        q_ref, kn_ref, vn_ref = qkv[3 * gi:3 * gi + 3]
        ck_ref, cv_ref = caches[2 * gi:2 * gi + 2]
        ko_ref, vo_ref = cache_out[2 * gi:2 * gi + 2]
        wb = ck_ref.shape[0]
        nk = wb + _KX_PAD
        zpad = jnp.zeros((_KX_PAD - t_len, ATT_GW), F32)
        for c_ref, n_ref, x_ref, co_ref in ((ck_ref, kn_ref, kx_ref, ko_ref), (cv_ref, vn_ref, vx_ref, vo_ref)):
            x_ref[0:wb, :] = c_ref[...].astype(BF16)
            x_ref[wb:nk, :] = jnp.concatenate([n_ref[...], zpad], axis=0).astype(BF16)
            co_ref[0:wb - t_len, :] = c_ref[t_len:wb, :]
            co_ref[wb - t_len:wb, :] = n_ref[...]
        q = q_ref[...] * (ATT_E ** -0.5)
        qst = jnp.concatenate([jnp.where(hm[h], q, 0.0) for h in range(ATT_HG)], axis=0).astype(BF16)
        s = _dot_nt(qst, kx_ref[0:nk, :])
        ri = lax.broadcasted_iota(jnp.int32, (rows, nk), 0)
        ci = lax.broadcasted_iota(jnp.int32, (rows, nk), 1)
        dist = wb + (ri % t_len) - ci
        valid = (dist >= 0) & (dist <= win) & ((dist & (dil - 1)) == 0)
        hrow = lax.broadcasted_iota(jnp.int32, (rows, 1), 0) // t_len
        slope = jnp.zeros((rows, 1), F32)
        for h in range(ATT_HG):
            slope = jnp.where(hrow == h, float(layer_slopes[gi * ATT_HG + h]), slope)
        s = jnp.where(valid, s - slope * dist.astype(F32), NEG_INF)
        m = jnp.max(s, axis=-1, keepdims=True)
        p = jnp.exp(s - m)
        den = jnp.sum(p, axis=-1, keepdims=True)
        ost = _dot(p.astype(BF16), vx_ref[0:nk, :]) / den
        lst = m + jnp.log(den)
        o = jnp.zeros((t_len, ATT_GW), F32)
        l = jnp.zeros((t_len, ATT_GW), F32)
        for h in range(ATT_HG):
            o = jnp.where(hm[h], ost[h * t_len:(h + 1) * t_len], o)
            l = jnp.where(hm[h], lst[h * t_len:(h + 1) * t_len], l)
        o_refs[gi][...] = o
        l_refs[gi][...] = l


def _attn_sample(z3, caches, layer):
    bsz, t_len, _ = z3.shape
    ng = len(ATT_GROUPS)
    kern = functools.partial(_attn_sample_kernel, t_len=t_len, layer_slopes=[float(s) for s in _SLOPES])
    in_specs, args = [], []
    for gi in range(ng):
        for off in (Z_QA, Z_KA, Z_VA):
            in_specs.append(pl.BlockSpec((None, t_len, ATT_GW), lambda b, o=off // ATT_GW + gi: (b, 0, o)))
            args.append(z3)
    wbs = []
    for gi in range(ng):
        for c in caches[2 * gi:2 * gi + 2]:
            wb = c.shape[2]
            in_specs.append(pl.BlockSpec((None, None, wb, ATT_GW), lambda b: (layer, b, 0, 0)))
            args.append(c.reshape(DEPTH, bsz, wb, ATT_GW))
        wbs.append(caches[2 * gi].shape[2])
    small = pl.BlockSpec((None, t_len, ATT_GW), lambda b: (b, 0, 0))
    out_specs = [small] * (2 * ng)
    out_shape = [jax.ShapeDtypeStruct((bsz, t_len, ATT_GW), F32)] * (2 * ng)
    for gi in range(ng):
        for _ in range(2):
            out_specs.append(pl.BlockSpec((None, wbs[gi], ATT_GW), lambda b: (b, 0, 0)))
            out_shape.append(jax.ShapeDtypeStruct((bsz, wbs[gi], ATT_GW), F32))
    res = pl.pallas_call(
        kern,
        grid=(bsz,),
        in_specs=in_specs,
        out_specs=out_specs,
        out_shape=out_shape,
        scratch_shapes=[pltpu.VMEM((max(wbs) + _KX_PAD, ATT_GW), BF16)] * 2,
        compiler_params=_params(("parallel",)),
        name="attn_sample",
    )(*args)
    m = bsz * t_len
    o = [r.reshape(m, ATT_GW) for r in res[0:ng]]
    l = [r.reshape(m, ATT_GW) for r in res[ng:2 * ng]]
    kv = [r.reshape(bsz, r.shape[1], ATT_HG, ATT_E) for r in res[2 * ng:]]
    return o, l, kv


def _merge_kernel(x_ref, og_ref, p_ref, o0_ref, o1_ref, o2_ref, l0_ref, l1_ref, l2_ref, g_ref, wg_ref,
                  woa_ref, wob_ref, woc_ref, wout_ref, h_ref):
    x = x_ref[...]
    xn = _rms(x, g_ref[...]).astype(BF16)

    def gate(i):
        return _sigmoid(_dot(xn, wg_ref[:, i * D_MODEL:(i + 1) * D_MODEL]))

    merged = gate(0) * _dot(og_ref[...].astype(BF16), woa_ref[...])
    merged = merged + gate(1) * _dot(p_ref[...].astype(BF16), wob_ref[...])
    ls = [l0_ref[...], l1_ref[...], l2_ref[...]]
    m = jnp.maximum(jnp.maximum(ls[0], ls[1]), ls[2])
    es = [jnp.exp(l - m) for l in ls]
    inv = 1.0 / (es[0] + es[1] + es[2])
    yc = jnp.zeros_like(x)
    for gi, o_ref in enumerate((o0_ref, o1_ref, o2_ref)):
        oc = (o_ref[...] * (es[gi] * inv)).astype(BF16)
        yc = yc + _dot(oc, woc_ref[gi * ATT_GW:(gi + 1) * ATT_GW, :])
    merged = merged + gate(2) * yc
    h_ref[...] = x + _dot(merged.astype(BF16), wout_ref[...])


def _merge(x2, og, p, o, l, g, wg, woa, wob, woc, wout, tm):
    m = x2.shape[0]
    row = lambda w: pl.BlockSpec((tm, w), lambda i: (i, 0))
    full = lambda a: pl.BlockSpec(a.shape, lambda i: (0,) * a.ndim)
    return pl.pallas_call(
        _merge_kernel,
        grid=(m // tm,),
        in_specs=[row(D_MODEL), row(GLA_VW), row(D_MODEL)] + [row(ATT_GW)] * 6
                 + [full(g), full(wg), full(woa), full(wob), full(woc), full(wout)],
        out_specs=row(D_MODEL),
        out_shape=jax.ShapeDtypeStruct((m, D_MODEL), F32),
        compiler_params=_params(("parallel",)),
        name="merge",
    )(x2, og, p, *o, *l, g, wg, woa, wob, woc, wout)


FFN_TF = 256
FFN_HALO = 16


def _ffn_kernel(*refs, t_len, tm, use_halo, has_prev, final):
    it = iter(refs)
    h_ref = next(it)
    halo_ref = next(it) if use_halo else None
    g_ref, wa_ref, wb_ref, cw_ref, cb_ref, wd_ref, fg_ref = (next(it) for _ in range(7))
    c1_ref, c2_ref = (next(it), next(it)) if has_prev else (None, None)
    y_ref, cv_ref = next(it), next(it)
    hn_ref, acc_ref = next(it), next(it)
    a_scr = None if use_halo else next(it)
    i, j = pl.program_id(0), pl.program_id(1)
    ho = FFN_HALO if use_halo else 0

    @pl.when(j == 0)
    def _():
        if use_halo:
            hn_ref[0:ho, :] = _rms(halo_ref[...], g_ref[...]).astype(BF16)
        hn_ref[ho:ho + tm, :] = _rms(h_ref[...], g_ref[...]).astype(BF16)
        acc_ref[...] = jnp.zeros_like(acc_ref)

    a_ext = _dot(hn_ref[...], wa_ref[...])
    a = a_ext[ho:]
    bg = _dot(hn_ref[ho:ho + tm, :], wb_ref[...])
    t_loc = (lax.broadcasted_iota(jnp.int32, (tm, 1), 0) + i * tm) % t_len
    if use_halo:
        a1, a2 = a_ext[ho - 1:ho - 1 + tm], a_ext[ho - 2:ho - 2 + tm]
    else:
        a1, a2 = pltpu.roll(a, 1, axis=0), pltpu.roll(a, 2, axis=0)
    a1 = jnp.where(t_loc >= 1, a1, c1_ref[...] if has_prev else 0.0)
    a2 = jnp.where(t_loc >= 2, a2, c2_ref[...] if has_prev else 0.0)
    y = cb_ref[...] + cw_ref[0:1, :] * a2 + cw_ref[1:2, :] * a1 + cw_ref[2:3, :] * a
    hh = (y * _sigmoid(y)) * bg
    acc_ref[...] += _dot(hh.astype(BF16), wd_ref[...])
    if use_halo:
        cv_ref[...] = a[tm - 2:tm]
    else:
        nseq = tm // t_len
        for kk in range(FFN_TF // LANE):
            cols = slice(kk * LANE, (kk + 1) * LANE)
            a_scr[kk] = a[:, cols]
            cv_ref[0, :, cols] = a_scr[kk, pl.ds(t_len - 2, nseq, stride=t_len), :]
            cv_ref[1, :, cols] = a_scr[kk, pl.ds(t_len - 1, nseq, stride=t_len), :]

    @pl.when(j == pl.num_programs(1) - 1)
    def _():
        out = h_ref[...] + acc_ref[...]
        if final:
            out = _rms(out, fg_ref[...])
        y_ref[...] = out


def _ffn(h2, g, wup, cw, cb, wd, fg, prev, t_len, tm, final):
    m = h2.shape[0]
    bsz = m // t_len
    nj = D_FF // FFN_TF
    use_halo = tm % t_len != 0
    has_prev = prev is not None
    kern = functools.partial(_ffn_kernel, t_len=t_len, tm=tm, use_halo=use_halo, has_prev=has_prev, final=final)
    in_specs = [pl.BlockSpec((tm, D_MODEL), lambda i, j: (i, 0))]
    args = [h2]
    if use_halo:
        hb = tm // FFN_HALO
        in_specs.append(pl.BlockSpec((FFN_HALO, D_MODEL), lambda i, j: (jnp.maximum(i * hb - 1, 0), 0)))
        args.append(h2)
    in_specs += [
        pl.BlockSpec((1, D_MODEL), lambda i, j: (0, 0)),
        pl.BlockSpec((D_MODEL, FFN_TF), lambda i, j: (0, j)),
        pl.BlockSpec((D_MODEL, FFN_TF), lambda i, j: (0, nj + j)),
        pl.BlockSpec((3, FFN_TF), lambda i, j: (0, j)),
        pl.BlockSpec((1, FFN_TF), lambda i, j: (0, j)),
        pl.BlockSpec((FFN_TF, D_MODEL), lambda i, j: (j, 0)),
        pl.BlockSpec((1, D_MODEL), lambda i, j: (0, 0)),
    ]
    args += [g, wup, wup, cw, cb, wd, fg]
    if has_prev:
        z = jnp.zeros((bsz, t_len - 2, D_FF), F32)
        c1 = jnp.concatenate([prev[:, 1:2], jnp.zeros((bsz, 1, D_FF), F32), z], axis=1).reshape(m, D_FF)
        c2 = jnp.concatenate([prev, z], axis=1).reshape(m, D_FF)
        in_specs += [pl.BlockSpec((tm, FFN_TF), lambda i, j: (i, j))] * 2
        args += [c1, c2]
    scratch = [pltpu.VMEM(((FFN_HALO if use_halo else 0) + tm, D_MODEL), BF16), pltpu.VMEM((tm, D_MODEL), F32)]
    if use_halo:
        cv_spec = pl.BlockSpec((None, 2, FFN_TF), lambda i, j: (i, 0, j))
        cv_shape = jax.ShapeDtypeStruct((m // tm, 2, D_FF), F32)
    else:
        nseq = tm // t_len
        cv_spec = pl.BlockSpec((2, nseq, FFN_TF), lambda i, j: (0, i, j))
        cv_shape = jax.ShapeDtypeStruct((2, bsz, D_FF), F32)
        scratch.append(pltpu.VMEM((FFN_TF // LANE, tm, LANE), F32))
    y, cv = pl.pallas_call(
        kern,
        grid=(m // tm, nj),
        in_specs=in_specs,
        out_specs=[pl.BlockSpec((tm, D_MODEL), lambda i, j: (i, 0)), cv_spec],
        out_shape=[jax.ShapeDtypeStruct((m, D_MODEL), F32), cv_shape],
        scratch_shapes=scratch,
        compiler_params=_params(("arbitrary", "arbitrary")),
        name="ffn",
    )(*args)
    if use_halo:
        per = t_len // tm
        cv = cv[per - 1::per]
    else:
        cv = jnp.swapaxes(cv, 0, 1)
    return y, cv


def _prep_weights(w_in, gla_wa2, pool_w, w_oa, w_ob, w_oc, w_out, ffn_w_up, ffn_w_down):
    o = _IN_OFF
    wz = jnp.concatenate([w_in[:, :, o[0]:o[4]], w_in[:, :, o[5]:o[9]], w_in[:, :, o[4]:o[5]],
                          jnp.zeros((DEPTH, D_MODEL, NZ - (o[9] - o[0])), w_in.dtype)], axis=-1).astype(BF16)
    wg = w_in[:, :, o[9]:o[12]].astype(BF16)
    wa2p = jnp.pad(gla_wa2, ((0, 0), (0, LANE - GLA_GATE_RANK), (0, 0))).astype(BF16)
    cast = lambda a: a.astype(BF16)
    return wz, wg, wa2p, cast(pool_w), cast(w_oa), cast(w_ob), cast(w_oc), cast(w_out), cast(ffn_w_up), cast(ffn_w_down)


def _run_group(x, states, weights, small, tiles):
    bsz, t_len, _ = x.shape
    m = bsz * t_len
    prompt = states is None
    wz, wg, wa2p, pw, woa, wob, woc, wout, wup, wdn = weights
    norm1_g, norm2_g, gla_ba, gla_norm_g, pool_scale, conv_w, conv_b, final_g = small
    x2 = x.reshape(m, D_MODEL)
    new_gla, new_pool, new_kv, new_conv = [], [], [], []
    for l in range(DEPTH):
        z = _in_proj(x2, norm1_g[l][None], wz[l], tiles["in_tm"], tiles["in_tn"])
        z3 = z.reshape(bsz, t_len, NZ)
        if prompt:
            s0 = jnp.zeros((bsz, GLA_HEADS, GLA_DK, GLA_DV), F32)
            prev = jnp.zeros((bsz, POOL_HALO, D_MODEL), F32)
            n_prev = 0
        else:
            s0 = states["gla"][l]
            prev = jnp.pad(states["pool"][l], ((0, 0), (POOL_HALO - POOL_STATE, 0), (0, 0)))
            n_prev = POOL_STATE
        og, g_new = _gla(z3, wa2p[l], gla_ba[l][None], gla_norm_g[l][None], s0, t_len)
        pb, p_new = _pool(z3, prev, pw[l], pool_scale[l][None], t_len, n_prev)
        if prompt:
            o, lse, kv = [], [], []
            for gi in range(len(ATT_GROUPS)):
                og_i, l_i, kc, vc = _attn_prompt(z, bsz, t_len, gi)
                o.append(og_i)
                lse.append(l_i)
                kv += [kc, vc]
        else:
            o, lse, kv = _attn_sample(z3, states["kv"], l)
        h2 = _merge(x2, og.reshape(m, GLA_VW), pb.reshape(m, D_MODEL), o, lse, norm1_g[l][None], wg[l],
                    woa[l], wob[l], woc[l], wout[l], tiles["merge_tm"])
        x2, c_new = _ffn(h2, norm2_g[l][None], wup[l], conv_w[l], conv_b[l][None], wdn[l], final_g[None],
                         None if prompt else states["conv"][l], t_len, tiles["ffn_tm"], l == DEPTH - 1)
        new_gla.append(g_new)
        new_pool.append(p_new)
        new_kv.append(kv)
        new_conv.append(c_new)
    kv_out = [jnp.stack([kvl[i] for kvl in new_kv], axis=0) for i in range(2 * len(ATT_GROUPS))]
    return (x2.reshape(bsz, t_len, D_MODEL), jnp.stack(new_gla, 0), jnp.stack(new_pool, 0), kv_out,
            jnp.stack(new_conv, 0))


def kernel(x_prompt, x_sample, state_gla, state_pool, cache_k_w128, cache_v_w128, cache_k_w512, cache_v_w512,
           cache_k_w2048, cache_v_w2048, state_ffn_conv, norm1_g, norm2_g, w_in, gla_wa2, gla_ba, gla_norm_g,
           pool_w, pool_scale, w_oa, w_ob, w_oc, w_out, ffn_w_up, ffn_conv_w, ffn_conv_b, ffn_w_down,
           final_norm_g):
    weights = _prep_weights(w_in, gla_wa2, pool_w, w_oa, w_ob, w_oc, w_out, ffn_w_up, ffn_w_down)
    small = (norm1_g, norm2_g, gla_ba, gla_norm_g, pool_scale, ffn_conv_w, ffn_conv_b, final_norm_g)
    y_p, gla_p, pool_p, kv_p, conv_p = _run_group(
        x_prompt, None, weights, small, dict(in_tm=1024, in_tn=512, merge_tm=256, ffn_tm=1024))
    states = dict(gla=state_gla, pool=state_pool, conv=state_ffn_conv,
                  kv=[cache_k_w128, cache_v_w128, cache_k_w512, cache_v_w512, cache_k_w2048, cache_v_w2048])
    m_s = x_sample.shape[0] * x_sample.shape[1]
    y_s, gla_s, pool_s, kv_s, conv_s = _run_group(
        x_sample, states, weights, small, dict(in_tm=m_s, in_tn=512, merge_tm=m_s, ffn_tm=m_s))
    k128_p, v128_p, k512_p, v512_p, k2048_p, v2048_p = kv_p
    k128_s, v128_s, k512_s, v512_s, k2048_s, v2048_s = kv_s
    return (y_p, y_s, gla_p, gla_s, pool_p, pool_s, k128_p, k128_s, v128_p, v128_s,
            k512_p, k512_s, v512_p, v512_s, k2048_p, k2048_s, v2048_p, v2048_s, conv_p, conv_s)
```

```python
import functools

import jax
import jax.numpy as jnp
import numpy as np
from jax import lax
from jax.experimental import pallas as pl
from jax.experimental.pallas import tpu as pltpu

F32 = jnp.float32
BF16 = jnp.bfloat16

D_MODEL = 1024
DEPTH = 2
GLA_HEADS = 4
GLA_DK = 128
GLA_DV = 256
GLA_QK = GLA_HEADS * GLA_DK
GLA_VW = GLA_HEADS * GLA_DV
GLA_GATE_RANK = 16
GLA_GATE_NORM = 16.0
GLA_CHUNK = 64
POOL_WINDOWS = (2, 4, 8, 16)
POOL_GROUP = 256
POOL_STATE = 15
POOL_HALO = 16
ATT_GROUPS = ((128, 1), (512, 4), (2048, 16))
ATT_HG = 4
ATT_E = 64
ATT_GW = ATT_HG * ATT_E
ATT_SPAN = 128
ATT_HEADS = 12
ATT_WIDTH = ATT_HEADS * ATT_E
D_FF = 2816
NORM_EPS = 1e-6
NEG_INF = -1e30

LANE = 128
SUBLANE = 8
BF16_SUBLANE = 16
VMEM_LIMIT = 56 * 1024 * 1024

_IN_SPLITS = (GLA_QK, GLA_QK, GLA_VW, GLA_VW, GLA_GATE_RANK, D_MODEL, ATT_WIDTH, ATT_WIDTH, ATT_WIDTH,
              D_MODEL, D_MODEL, D_MODEL)
_IN_OFF = [0] + [int(v) for v in np.cumsum(_IN_SPLITS)]
ZG_W = 2 * GLA_QK + 2 * GLA_VW
ZG_K, ZG_V, ZG_R = GLA_QK, 2 * GLA_QK, 2 * GLA_QK + GLA_VW
UA_W = D_MODEL + 3 * ATT_WIDTH
UA_Q, UA_K, UA_V = D_MODEL, D_MODEL + ATT_WIDTH, D_MODEL + 2 * ATT_WIDTH
ZG_TN = 1024
UA_TN = UA_W // 2

_SLOPES = (2.0 ** (-8.0 * np.arange(1, ATT_HEADS + 1) / ATT_HEADS)).astype(np.float32)


def _params(sem):
    return pltpu.CompilerParams(dimension_semantics=sem, vmem_limit_bytes=VMEM_LIMIT)


def _resident(shape):
    return pl.BlockSpec(shape, lambda *_: (0,) * len(shape), pipeline_mode=pl.Buffered(1))


def _act_dtype(t_len):
    return BF16 if t_len % BF16_SUBLANE == 0 else F32


def _dot(a, b):
    return jnp.dot(a, b, preferred_element_type=F32)


def _dot_nt(a, b):
    return lax.dot_general(a, b, (((1,), (1,)), ((), ())), preferred_element_type=F32)


def _dot_tn(a, b):
    return lax.dot_general(a, b, (((0,), (0,)), ((), ())), preferred_element_type=F32)


def _rms(x, g):
    return x * lax.rsqrt(jnp.mean(x * x, axis=-1, keepdims=True) + NORM_EPS) * g


def _sigmoid(x):
    return 1.0 / (1.0 + jnp.exp(-x))


def _in_proj_kernel(x_ref, g_ref, w1_ref, wal_ref, w2_ref, zg_ref, al_ref, ua_ref, xn_ref, *, n1):
    j = pl.program_id(1)

    @pl.when(j == 0)
    def _():
        xn = _rms(x_ref[...], g_ref[...]).astype(BF16)
        xn_ref[...] = xn
        al_ref[...] = _dot(xn, wal_ref[...]).astype(al_ref.dtype)

    @pl.when(j < n1)
    def _():
        zg_ref[...] = _dot(xn_ref[...], w1_ref[...]).astype(zg_ref.dtype)

    @pl.when(j >= n1)
    def _():
        ua_ref[...] = _dot(xn_ref[...], w2_ref[...])


def _in_proj(x2, g, w1, wal, w2, tm, act):
    m = x2.shape[0]
    n1, n2 = ZG_W // ZG_TN, UA_W // UA_TN
    j1 = lambda j: jnp.minimum(j, n1 - 1)
    j2 = lambda j: jnp.maximum(j - n1, 0)
    return pl.pallas_call(
        functools.partial(_in_proj_kernel, n1=n1),
        grid=(m // tm, n1 + n2),
        in_specs=[
            pl.BlockSpec((tm, D_MODEL), lambda i, j: (i, 0)),
            pl.BlockSpec((1, D_MODEL), lambda i, j: (0, 0)),
            pl.BlockSpec((D_MODEL, ZG_TN), lambda i, j: (0, j1(j))),
            pl.BlockSpec((D_MODEL, LANE), lambda i, j: (0, 0)),
            pl.BlockSpec((D_MODEL, UA_TN), lambda i, j: (0, j2(j))),
        ],
        out_specs=[
            pl.BlockSpec((tm, ZG_TN), lambda i, j: (i, j1(j))),
            pl.BlockSpec((tm, LANE), lambda i, j: (i, 0)),
            pl.BlockSpec((tm, UA_TN), lambda i, j: (i, j2(j))),
        ],
        out_shape=[
            jax.ShapeDtypeStruct((m, ZG_W), act),
            jax.ShapeDtypeStruct((m, LANE), act),
            jax.ShapeDtypeStruct((m, UA_W), F32),
        ],
        scratch_shapes=[pltpu.VMEM((tm, D_MODEL), BF16)],
        compiler_params=_params(("parallel", "arbitrary")),
        name="in_proj",
    )(x2, g, w1, wal, w2)


def _gla_kernel(q_ref, k_ref, v_ref, r_ref, a_ref, wa_ref, ba_ref, gn_ref, s0_ref, o_ref, so_ref,
                qe_ref, ke_ref, qi_ref, ks_ref, vb_ref, dec_ref, oacc_ref, st_ref, *, c, nc):
    ce = max(c, BF16_SUBLANE)
    tp = nc * ce
    mid = (c - 1) // 2

    def load(ref):
        x = ref[...].astype(F32)
        if ce != c:
            x = jnp.concatenate([x, jnp.zeros((ce - c, x.shape[1]), F32)], axis=0)
        return x

    z = _dot(load(a_ref).astype(BF16), wa_ref[...]) + ba_ref[...]
    la = (jnp.minimum(z, 0.0) - jnp.log1p(jnp.exp(-jnp.abs(z)))) * (1.0 / GLA_GATE_NORM)
    pos = lax.broadcasted_iota(jnp.int32, (tp, GLA_DK), 0) & (ce - 1)
    if ce != c:
        la = jnp.where(pos < c, la, 0.0)
    b = la
    s = 1
    while s < c:
        b = b + jnp.where(pos >= s, pltpu.roll(b, s, axis=0), 0.0)
        s *= 2
    b3 = b.reshape(nc, ce, GLA_DK)
    b_mid = b3[:, mid:mid + 1, :]
    b_last = b3[:, c - 1:c, :]
    q3 = (load(q_ref) * (GLA_DK ** -0.5)).reshape(nc, ce, GLA_DK)
    k3 = load(k_ref).reshape(nc, ce, GLA_DK)
    flat = lambda x: x.reshape(tp, GLA_DK).astype(BF16)
    qe_ref[...] = flat(q3 * jnp.exp(b3 - b_mid))
    ke_ref[...] = flat(k3 * jnp.exp(b_mid - b3))
    qi_ref[...] = flat(q3 * jnp.exp(b3))
    ks_ref[...] = flat(k3 * jnp.exp(b_last - b3))
    dec_ref[...] = jnp.exp(b_last)
    vb_ref[...] = load(v_ref).astype(BF16)
    st_ref[...] = s0_ref[...].T
    causal = (lax.broadcasted_iota(jnp.int32, (ce, ce), 0) >= lax.broadcasted_iota(jnp.int32, (ce, ce), 1))

    def chunk(n, carry):
        sl = pl.ds(pl.multiple_of(n * ce, ce), ce)
        v = vb_ref[sl, :]
        att = jnp.where(causal, _dot_nt(qe_ref[sl, :], ke_ref[sl, :]), 0.0).astype(BF16)
        st = st_ref[...]
        oacc_ref[sl, :] = _dot(att, v) + _dot_nt(qi_ref[sl, :], st.astype(BF16))
        st_ref[...] = st * dec_ref[n] + _dot_tn(v, ks_ref[sl, :])
        return carry

    lax.fori_loop(0, nc, chunk, 0, unroll=min(nc, 4))
    r = load(r_ref)
    res = _rms(oacc_ref[...], gn_ref[...]) * (r * _sigmoid(r))
    o_ref[...] = res[:nc * c if ce == c else c].astype(o_ref.dtype)
    so_ref[...] = st_ref[...].T


def _gla(zg3, al3, wa2p, ba, gn, s0, t_len):
    bsz = zg3.shape[0]
    c = GLA_CHUNK if t_len % GLA_CHUNK == 0 else t_len
    nc = t_len // c
    ce = max(c, BF16_SUBLANE)
    assert ce == c or nc == 1
    tp = nc * ce
    kern = functools.partial(_gla_kernel, c=c, nc=nc)
    zspec = lambda w, off: pl.BlockSpec((None, t_len, w), lambda b, h: (b, 0, off // w + h))
    return pl.pallas_call(
        kern,
        grid=(bsz, GLA_HEADS),
        in_specs=[
            zspec(GLA_DK, 0), zspec(GLA_DK, ZG_K), zspec(GLA_DV, ZG_V), zspec(GLA_DV, ZG_R),
            pl.BlockSpec((None, t_len, LANE), lambda b, h: (b, 0, 0)),
            pl.BlockSpec((LANE, GLA_DK), lambda b, h: (0, h)),
            pl.BlockSpec((1, GLA_DK), lambda b, h: (0, h)),
            pl.BlockSpec((1, GLA_DV), lambda b, h: (0, 0)),
            pl.BlockSpec((None, None, GLA_DK, GLA_DV), lambda b, h: (b, h, 0, 0)),
        ],
        out_specs=[
            pl.BlockSpec((None, t_len, GLA_DV), lambda b, h: (b, 0, h)),
            pl.BlockSpec((None, None, GLA_DK, GLA_DV), lambda b, h: (b, h, 0, 0)),
        ],
        out_shape=[
            jax.ShapeDtypeStruct((bsz, t_len, GLA_VW), zg3.dtype),
            jax.ShapeDtypeStruct((bsz, GLA_HEADS, GLA_DK, GLA_DV), F32),
        ],
        scratch_shapes=[pltpu.VMEM((tp, GLA_DK), BF16)] * 4 + [
            pltpu.VMEM((tp, GLA_DV), BF16),
            pltpu.VMEM((nc, 1, GLA_DK), F32),
            pltpu.VMEM((tp, GLA_DV), F32),
            pltpu.VMEM((GLA_DV, GLA_DK), F32),
        ],
        compiler_params=_params(("parallel", "arbitrary")),
        name="gla",
    )(zg3, zg3, zg3, zg3, al3, wa2p, ba, gn, s0)


def _pool_kernel(u_ref, prev_ref, pw_ref, ps_ref, p_ref, pn_ref, *, t_len, n_prev):
    ext = jnp.concatenate([prev_ref[...], u_ref[...]], axis=0)
    t_abs = lax.broadcasted_iota(jnp.int32, (t_len, 1), 0) + n_prev
    for gi, w in enumerate(POOL_WINDOWS):
        cols = slice(gi * POOL_GROUP, (gi + 1) * POOL_GROUP)
        x = ext[:, cols]
        acc = x
        s = 1
        while s < w:
            acc = acc + pltpu.roll(acc, s, axis=0)
            s *= 2
        cnt = jnp.minimum(t_abs + 1, w).astype(F32)
        pooled = acc[POOL_HALO:] / cnt - x[POOL_HALO:]
        mixed = _dot(pooled.astype(BF16), pw_ref[gi]) * ps_ref[:, cols]
        p_ref[:, cols] = mixed.astype(p_ref.dtype)
    keep = max(0, POOL_STATE - t_len)
    if keep:
        pn_ref[0:keep, :] = prev_ref[POOL_HALO - keep:POOL_HALO, :]
    pn_ref[keep:POOL_STATE, :] = u_ref[t_len - (POOL_STATE - keep):t_len, :]


def _pool(ua3, prev, pw, ps, t_len, n_prev):
    bsz = ua3.shape[0]
    kern = functools.partial(_pool_kernel, t_len=t_len, n_prev=n_prev)
    return pl.pallas_call(
        kern,
        grid=(bsz,),
        in_specs=[
            pl.BlockSpec((None, t_len, D_MODEL), lambda b: (b, 0, 0)),
            pl.BlockSpec((None, POOL_HALO, D_MODEL), lambda b: (b, 0, 0)),
            pl.BlockSpec((4, POOL_GROUP, POOL_GROUP), lambda b: (0, 0, 0)),
            pl.BlockSpec((1, D_MODEL), lambda b: (0, 0)),
        ],
        out_specs=[
            pl.BlockSpec((None, t_len, D_MODEL), lambda b: (b, 0, 0)),
            pl.BlockSpec((None, POOL_STATE, D_MODEL), lambda b: (b, 0, 0)),
        ],
        out_shape=[
            jax.ShapeDtypeStruct((bsz, t_len, D_MODEL), _act_dtype(t_len)),
            jax.ShapeDtypeStruct((bsz, POOL_STATE, D_MODEL), F32),
        ],
        compiler_params=_params(("parallel",)),
        name="pool",
    )(ua3, prev, pw, ps)


def _attn_prompt_kernel(q_ref, k_ref, v_ref, o_ref, l_ref, kc_ref, vc_ref, *, t_len, keep, dil, slopes):
    n = t_len // dil
    nb = n // ATT_SPAN
    hp = pl.program_id(1)
    kc_ref[...] = k_ref[t_len - keep:t_len, :]
    vc_ref[...] = v_ref[t_len - keep:t_len, :]
    lane = lax.broadcasted_iota(jnp.int32, (ATT_SPAN, LANE), 1)
    hm = [lane < ATT_E, lane >= ATT_E]
    bias_scale = [jnp.where(hp == 0, -slopes[h] * dil, -slopes[2 + h] * dil) for h in range(2)]
    nk = ATT_SPAN if nb == 1 else 2 * ATT_SPAN
    a_idx = lax.broadcasted_iota(jnp.int32, (ATT_SPAN, nk), 0)
    c_idx = lax.broadcasted_iota(jnp.int32, (ATT_SPAN, nk), 1)
    j = a_idx - c_idx + (nk - ATT_SPAN)
    base_valid = (j >= 0) & (j <= ATT_SPAN)
    jf = j.astype(F32)

    def rows(ref, start):
        if dil == 1:
            return ref[pl.ds(pl.multiple_of(start, ATT_SPAN), ATT_SPAN), :]
        return ref[pl.ds(start, ATT_SPAN, stride=dil), :]

    def block(t, carry):
        r, qi = t // nb, t % nb
        q0 = r + qi * (ATT_SPAN * dil)
        q = rows(q_ref, q0) * (ATT_E ** -0.5)
        if nb > 1:
            p0 = r + jnp.maximum(qi - 1, 0) * (ATT_SPAN * dil)
            kk = jnp.concatenate([rows(k_ref, p0), rows(k_ref, q0)], axis=0).astype(BF16)
            vv = jnp.concatenate([rows(v_ref, p0), rows(v_ref, q0)], axis=0).astype(BF16)
            valid = base_valid & (c_idx >= jnp.where(qi == 0, ATT_SPAN, 0))
        else:
            kk = rows(k_ref, q0).astype(BF16)
            vv = rows(v_ref, q0).astype(BF16)
            valid = base_valid
        o_acc = jnp.zeros((ATT_SPAN, LANE), F32)
        l_acc = jnp.zeros((ATT_SPAN, LANE), F32)
        for h in range(2):
            qm = jnp.where(hm[h], q, 0.0).astype(BF16)
            s = jnp.where(valid, _dot_nt(qm, kk) + jf * bias_scale[h], NEG_INF)
            m = jnp.max(s, axis=-1, keepdims=True)
            p = jnp.exp(s - m)
            den = jnp.sum(p, axis=-1, keepdims=True)
            oh = _dot(p.astype(BF16), vv) / den
            o_acc = jnp.where(hm[h], oh, o_acc)
            l_acc = jnp.where(hm[h], m + jnp.log(den), l_acc)
        if dil == 1:
            osl = pl.ds(pl.multiple_of(q0, ATT_SPAN), ATT_SPAN)
        else:
            osl = pl.ds(q0, ATT_SPAN, stride=dil)
        o_ref[osl, :] = o_acc
        l_ref[osl, :] = l_acc
        return carry

    lax.fori_loop(0, dil * nb, block, 0, unroll=2)


def _attn_prompt(ua3, gi):
    bsz, t_len, _ = ua3.shape
    win, dil = ATT_GROUPS[gi]
    keep = min(win, t_len)
    kern = functools.partial(_attn_prompt_kernel, t_len=t_len, keep=keep, dil=dil,
                             slopes=[float(s) for s in _SLOPES[gi * ATT_HG:(gi + 1) * ATT_HG]])
    uspec = lambda off: pl.BlockSpec((None, t_len, LANE), lambda b, hp: (b, 0, (off + gi * ATT_GW) // LANE + hp))
    ospec = lambda rows: pl.BlockSpec((None, rows, LANE), lambda b, hp: (b, 0, hp))
    o, l, kc, vc = pl.pallas_call(
        kern,
        grid=(bsz, ATT_GW // LANE),
        in_specs=[uspec(UA_Q), uspec(UA_K), uspec(UA_V)],
        out_specs=[ospec(t_len), ospec(t_len), ospec(keep), ospec(keep)],
        out_shape=[
            jax.ShapeDtypeStruct((bsz, t_len, ATT_GW), F32),
            jax.ShapeDtypeStruct((bsz, t_len, ATT_GW), F32),
            jax.ShapeDtypeStruct((bsz, keep, ATT_GW), F32),
            jax.ShapeDtypeStruct((bsz, keep, ATT_GW), F32),
        ],
        compiler_params=_params(("parallel", "parallel")),
        name=f"attn_prompt_g{gi}",
    )(ua3, ua3, ua3)
    m = bsz * t_len
    return (o.reshape(m, ATT_GW), l.reshape(m, ATT_GW),
            kc.reshape(bsz, keep, ATT_HG, ATT_E), vc.reshape(bsz, keep, ATT_HG, ATT_E))


_KX_PAD = LANE


def _head_masks(rows):
    lane = lax.broadcasted_iota(jnp.int32, (rows, ATT_GW), 1)
    return [(lane >= h * ATT_E) & (lane < (h + 1) * ATT_E) for h in range(ATT_HG)]


def _attn_sample_kernel(*refs, t_len, layer_slopes):
    ng = len(ATT_GROUPS)
    qkv = refs[0:3 * ng]
    caches = refs[3 * ng:5 * ng]
    outs = refs[5 * ng:]
    o_refs, l_refs, cache_out = outs[0:ng], outs[ng:2 * ng], outs[2 * ng:4 * ng]
    kx_ref, vx_ref = outs[4 * ng:]
    hm = _head_masks(t_len)
    rows = ATT_HG * t_len
    for gi, (win, dil) in enumerate(ATT_GROUPS):
        q_ref, kn_ref, vn_ref = qkv[3 * gi:3 * gi + 3]
        ck_ref, cv_ref = caches[2 * gi:2 * gi + 2]
        ko_ref, vo_ref = cache_out[2 * gi:2 * gi + 2]
        wb = ck_ref.shape[0]
        nk = wb + _KX_PAD
        zpad = jnp.zeros((_KX_PAD - t_len, ATT_GW), F32)
        for c_ref, n_ref, x_ref, co_ref in ((ck_ref, kn_ref, kx_ref, ko_ref), (cv_ref, vn_ref, vx_ref, vo_ref)):
            x_ref[0:wb, :] = c_ref[...].astype(BF16)
            x_ref[wb:nk, :] = jnp.concatenate([n_ref[...], zpad], axis=0).astype(BF16)
            co_ref[0:wb - t_len, :] = c_ref[t_len:wb, :]
            co_ref[wb - t_len:wb, :] = n_ref[...]
        q = q_ref[...] * (ATT_E ** -0.5)
        qst = jnp.concatenate([jnp.where(hm[h], q, 0.0) for h in range(ATT_HG)], axis=0).astype(BF16)
        s = _dot_nt(qst, kx_ref[0:nk, :])
        ri = lax.broadcasted_iota(jnp.int32, (rows, nk), 0)
        ci = lax.broadcasted_iota(jnp.int32, (rows, nk), 1)
        dist = wb + (ri % t_len) - ci
        valid = (dist >= 0) & (dist <= win) & ((dist & (dil - 1)) == 0)
        hrow = lax.broadcasted_iota(jnp.int32, (rows, 1), 0) // t_len
        slope = jnp.zeros((rows, 1), F32)
        for h in range(ATT_HG):
            slope = jnp.where(hrow == h, float(layer_slopes[gi * ATT_HG + h]), slope)
        s = jnp.where(valid, s - slope * dist.astype(F32), NEG_INF)
        m = jnp.max(s, axis=-1, keepdims=True)
        p = jnp.exp(s - m)
        den = jnp.sum(p, axis=-1, keepdims=True)
        ost = _dot(p.astype(BF16), vx_ref[0:nk, :]) / den
        lst = m + jnp.log(den)
        o = jnp.zeros((t_len, ATT_GW), F32)
        l = jnp.zeros((t_len, ATT_GW), F32)
        for h in range(ATT_HG):
            o = jnp.where(hm[h], ost[h * t_len:(h + 1) * t_len], o)
            l = jnp.where(hm[h], lst[h * t_len:(h + 1) * t_len], l)
        o_refs[gi][...] = o
        l_refs[gi][...] = l


def _attn_sample(ua3, caches, layer):
    bsz, t_len, _ = ua3.shape
    ng = len(ATT_GROUPS)
    kern = functools.partial(_attn_sample_kernel, t_len=t_len, layer_slopes=[float(s) for s in _SLOPES])
    in_specs, args = [], []
    for gi in range(ng):
        for off in (UA_Q, UA_K, UA_V):
            in_specs.append(pl.BlockSpec((None, t_len, ATT_GW), lambda b, o=off // ATT_GW + gi: (b, 0, o)))
            args.append(ua3)
    wbs = []
    for gi in range(ng):
        for c in caches[2 * gi:2 * gi + 2]:
            wb = c.shape[2]
            in_specs.append(pl.BlockSpec((None, None, wb, ATT_GW), lambda b: (layer, b, 0, 0)))
            args.append(c.reshape(DEPTH, bsz, wb, ATT_GW))
        wbs.append(caches[2 * gi].shape[2])
    small = pl.BlockSpec((None, t_len, ATT_GW), lambda b: (b, 0, 0))
    out_specs = [small] * (2 * ng)
    out_shape = [jax.ShapeDtypeStruct((bsz, t_len, ATT_GW), F32)] * (2 * ng)
    for gi in range(ng):
        for _ in range(2):
            out_specs.append(pl.BlockSpec((None, wbs[gi], ATT_GW), lambda b: (b, 0, 0)))
            out_shape.append(jax.ShapeDtypeStruct((bsz, wbs[gi], ATT_GW), F32))
    res = pl.pallas_call(
        kern,
        grid=(bsz,),
        in_specs=in_specs,
        out_specs=out_specs,
        out_shape=out_shape,
        scratch_shapes=[pltpu.VMEM((max(wbs) + _KX_PAD, ATT_GW), BF16)] * 2,
        compiler_params=_params(("parallel",)),
        name="attn_sample",
    )(*args)
    m = bsz * t_len
    o = [r.reshape(m, ATT_GW) for r in res[0:ng]]
    l = [r.reshape(m, ATT_GW) for r in res[ng:2 * ng]]
    kv = [r.reshape(bsz, r.shape[1], ATT_HG, ATT_E) for r in res[2 * ng:]]
    return o, l, kv


def _merge_kernel(x_ref, og_ref, p_ref, o0_ref, o1_ref, o2_ref, l0_ref, l1_ref, l2_ref, g_ref, wg_ref,
                  woa_ref, wob_ref, woc_ref, wout_ref, h_ref):
    x = x_ref[...]
    xn = _rms(x, g_ref[...]).astype(BF16)

    def gate(i):
        return _sigmoid(_dot(xn, wg_ref[:, i * D_MODEL:(i + 1) * D_MODEL]))

    merged = gate(0) * _dot(og_ref[...].astype(BF16), woa_ref[...])
    merged = merged + gate(1) * _dot(p_ref[...].astype(BF16), wob_ref[...])
    ls = [l0_ref[...], l1_ref[...], l2_ref[...]]
    m = jnp.maximum(jnp.maximum(ls[0], ls[1]), ls[2])
    es = [jnp.exp(l - m) for l in ls]
    inv = 1.0 / (es[0] + es[1] + es[2])
    yc = jnp.zeros_like(x)
    for gi, o_ref in enumerate((o0_ref, o1_ref, o2_ref)):
        oc = (o_ref[...] * (es[gi] * inv)).astype(BF16)
        yc = yc + _dot(oc, woc_ref[gi * ATT_GW:(gi + 1) * ATT_GW, :])
    merged = merged + gate(2) * yc
    h_ref[...] = x + _dot(merged.astype(BF16), wout_ref[...])


def _merge(x2, og, p, o, l, g, wg, woa, wob, woc, wout, tm):
    m = x2.shape[0]
    row = lambda w: pl.BlockSpec((tm, w), lambda i: (i, 0))
    return pl.pallas_call(
        _merge_kernel,
        grid=(m // tm,),
        in_specs=[row(D_MODEL), row(GLA_VW), row(D_MODEL)] + [row(ATT_GW)] * 6
                 + [_resident(a.shape) for a in (g, wg, woa, wob, woc, wout)],
        out_specs=row(D_MODEL),
        out_shape=jax.ShapeDtypeStruct((m, D_MODEL), F32),
        compiler_params=_params(("parallel",)),
        name="merge",
    )(x2, og, p, *o, *l, g, wg, woa, wob, woc, wout)


FFN_TF = 256
FFN_HALO = 16


def _ffn_kernel(*refs, t_len, tm, use_halo, has_prev, final):
    it = iter(refs)
    h_ref = next(it)
    halo_ref = next(it) if use_halo else None
    g_ref, wup_ref, cw_ref, cb_ref, wd_ref, fg_ref = (next(it) for _ in range(6))
    c1_ref, c2_ref = (next(it), next(it)) if has_prev else (None, None)
    y_ref, cv_ref = next(it), next(it)
    hn_ref, hh_ref = next(it), next(it)
    a_scr = None if use_halo else next(it)
    ho = FFN_HALO if use_halo else 0
    if use_halo:
        hn_ref[0:ho, :] = _rms(halo_ref[...], g_ref[...]).astype(BF16)
    hn_ref[ho:ho + tm, :] = _rms(h_ref[...], g_ref[...]).astype(BF16)
    t_loc = (lax.broadcasted_iota(jnp.int32, (tm, 1), 0) + pl.program_id(0) * tm) % t_len
    for jt in range(D_FF // FFN_TF):
        cols = slice(jt * FFN_TF, (jt + 1) * FFN_TF)
        a_ext = _dot(hn_ref[...], wup_ref[:, cols])
        a = a_ext[ho:]
        bg = _dot(hn_ref[ho:ho + tm, :], wup_ref[:, D_FF + jt * FFN_TF:D_FF + (jt + 1) * FFN_TF])
        if use_halo:
            a1, a2 = a_ext[ho - 1:ho - 1 + tm], a_ext[ho - 2:ho - 2 + tm]
        else:
            a1, a2 = pltpu.roll(a, 1, axis=0), pltpu.roll(a, 2, axis=0)
        a1 = jnp.where(t_loc >= 1, a1, c1_ref[:, cols] if has_prev else 0.0)
        a2 = jnp.where(t_loc >= 2, a2, c2_ref[:, cols] if has_prev else 0.0)
        y = cb_ref[:, cols] + cw_ref[0:1, cols] * a2 + cw_ref[1:2, cols] * a1 + cw_ref[2:3, cols] * a
        hh_ref[:, cols] = ((y * _sigmoid(y)) * bg).astype(BF16)
        if use_halo:
            cv_ref[:, cols] = a[tm - 2:tm]
        else:
            nseq = tm // t_len
            for kk in range(FFN_TF // LANE):
                lc = slice(kk * LANE, (kk + 1) * LANE)
                oc = slice(jt * FFN_TF + kk * LANE, jt * FFN_TF + (kk + 1) * LANE)
                a_scr[kk] = a[:, lc]
                cv_ref[0, :, oc] = a_scr[kk, pl.ds(t_len - 2, nseq, stride=t_len), :]
                cv_ref[1, :, oc] = a_scr[kk, pl.ds(t_len - 1, nseq, stride=t_len), :]
    out = h_ref[...] + _dot(hh_ref[...], wd_ref[...])
    if final:
        out = _rms(out, fg_ref[...])
    y_ref[...] = out


def _ffn(h2, g, wup, cw, cb, wd, fg, prev, t_len, tm, final):
    m = h2.shape[0]
    bsz = m // t_len
    use_halo = tm % t_len != 0
    has_prev = prev is not None
    kern = functools.partial(_ffn_kernel, t_len=t_len, tm=tm, use_halo=use_halo, has_prev=has_prev, final=final)
    in_specs = [pl.BlockSpec((tm, D_MODEL), lambda i: (i, 0))]
    args = [h2]
    if use_halo:
        hb = tm // FFN_HALO
        in_specs.append(pl.BlockSpec((FFN_HALO, D_MODEL), lambda i: (jnp.maximum(i * hb - 1, 0), 0)))
        args.append(h2)
    in_specs += [_resident(a.shape) for a in (g, wup, cw, cb, wd, fg)]
    args += [g, wup, cw, cb, wd, fg]
    if has_prev:
        z = jnp.zeros((bsz, t_len - 2, D_FF), F32)
        c1 = jnp.concatenate([prev[:, 1:2], jnp.zeros((bsz, 1, D_FF), F32), z], axis=1).reshape(m, D_FF)
        c2 = jnp.concatenate([prev, z], axis=1).reshape(m, D_FF)
        in_specs += [pl.BlockSpec((tm, D_FF), lambda i: (i, 0))] * 2
        args += [c1, c2]
    scratch = [pltpu.VMEM(((FFN_HALO if use_halo else 0) + tm, D_MODEL), BF16), pltpu.VMEM((tm, D_FF), BF16)]
    if use_halo:
        cv_spec = pl.BlockSpec((None, 2, D_FF), lambda i: (i, 0, 0))
        cv_shape = jax.ShapeDtypeStruct((m // tm, 2, D_FF), F32)
    else:
        nseq = tm // t_len
        cv_spec = pl.BlockSpec((2, nseq, D_FF), lambda i: (0, i, 0))
        cv_shape = jax.ShapeDtypeStruct((2, bsz, D_FF), F32)
        scratch.append(pltpu.VMEM((FFN_TF // LANE, tm, LANE), F32))
    y, cv = pl.pallas_call(
        kern,
        grid=(m // tm,),
        in_specs=in_specs,
        out_specs=[pl.BlockSpec((tm, D_MODEL), lambda i: (i, 0)), cv_spec],
        out_shape=[jax.ShapeDtypeStruct((m, D_MODEL), F32), cv_shape],
        scratch_shapes=scratch,
        compiler_params=_params(("parallel",)),
        name="ffn",
    )(*args)
    if use_halo:
        per = t_len // tm
        cv = cv[per - 1::per]
    else:
        cv = jnp.swapaxes(cv, 0, 1)
    return y, cv


def _prep_weights(w_in, gla_wa2, pool_w, w_oa, w_ob, w_oc, w_out, ffn_w_up, ffn_w_down):
    o = _IN_OFF
    cast = lambda a: a.astype(BF16)
    w1 = cast(w_in[:, :, o[0]:o[4]])
    wal = cast(jnp.pad(w_in[:, :, o[4]:o[5]], ((0, 0), (0, 0), (0, LANE - GLA_GATE_RANK))))
    w2 = cast(w_in[:, :, o[5]:o[9]])
    wg = cast(w_in[:, :, o[9]:o[12]])
    wa2p = cast(jnp.pad(gla_wa2, ((0, 0), (0, LANE - GLA_GATE_RANK), (0, 0))))
    return (w1, wal, w2, wg, wa2p, cast(pool_w), cast(w_oa), cast(w_ob), cast(w_oc), cast(w_out),
            cast(ffn_w_up), cast(ffn_w_down))


def _run_group(x, states, weights, small, tiles):
    bsz, t_len, _ = x.shape
    m = bsz * t_len
    prompt = states is None
    act = _act_dtype(t_len)
    w1, wal, w2, wg, wa2p, pw, woa, wob, woc, wout, wup, wdn = weights
    norm1_g, norm2_g, gla_ba, gla_norm_g, pool_scale, conv_w, conv_b, final_g = small
    x2 = x.reshape(m, D_MODEL)
    new_gla, new_pool, new_kv, new_conv = [], [], [], []
    for l in range(DEPTH):
        zg, al, ua = _in_proj(x2, norm1_g[l][None], w1[l], wal[l], w2[l], tiles["in_tm"], act)
        zg3 = zg.reshape(bsz, t_len, ZG_W)
        al3 = al.reshape(bsz, t_len, LANE)
        ua3 = ua.reshape(bsz, t_len, UA_W)
        if prompt:
            s0 = jnp.zeros((bsz, GLA_HEADS, GLA_DK, GLA_DV), F32)
            prev = jnp.zeros((bsz, POOL_HALO, D_MODEL), F32)
            n_prev = 0
        else:
            s0 = states["gla"][l]
            prev = jnp.pad(states["pool"][l], ((0, 0), (POOL_HALO - POOL_STATE, 0), (0, 0)))
            n_prev = POOL_STATE
        og, g_new = _gla(zg3, al3, wa2p[l], gla_ba[l][None], gla_norm_g[l][None], s0, t_len)
        pb, p_new = _pool(ua3, prev, pw[l], pool_scale[l][None], t_len, n_prev)
        if prompt:
            o, lse, kv = [], [], []
            for gi in range(len(ATT_GROUPS)):
                og_i, l_i, kc, vc = _attn_prompt(ua3, gi)
                o.append(og_i)
                lse.append(l_i)
                kv += [kc, vc]
        else:
            o, lse, kv = _attn_sample(ua3, states["kv"], l)
        h2 = _merge(x2, og.reshape(m, GLA_VW), pb.reshape(m, D_MODEL), o, lse, norm1_g[l][None], wg[l],
                    woa[l], wob[l], woc[l], wout[l], tiles["merge_tm"])
        x2, c_new = _ffn(h2, norm2_g[l][None], wup[l], conv_w[l], conv_b[l][None], wdn[l], final_g[None],
                         None if prompt else states["conv"][l], t_len, tiles["ffn_tm"], l == DEPTH - 1)
        new_gla.append(g_new)
        new_pool.append(p_new)
        new_kv.append(kv)
        new_conv.append(c_new)
    kv_out = [jnp.stack([kvl[i] for kvl in new_kv], axis=0) for i in range(2 * len(ATT_GROUPS))]
    return (x2.reshape(bsz, t_len, D_MODEL), jnp.stack(new_gla, 0), jnp.stack(new_pool, 0), kv_out,
            jnp.stack(new_conv, 0))


def kernel(x_prompt, x_sample, state_gla, state_pool, cache_k_w128, cache_v_w128, cache_k_w512, cache_v_w512,
           cache_k_w2048, cache_v_w2048, state_ffn_conv, norm1_g, norm2_g, w_in, gla_wa2, gla_ba, gla_norm_g,
           pool_w, pool_scale, w_oa, w_ob, w_oc, w_out, ffn_w_up, ffn_conv_w, ffn_conv_b, ffn_w_down,
           final_norm_g):
    weights = _prep_weights(w_in, gla_wa2, pool_w, w_oa, w_ob, w_oc, w_out, ffn_w_up, ffn_w_down)
    small = (norm1_g, norm2_g, gla_ba, gla_norm_g, pool_scale, ffn_conv_w, ffn_conv_b, final_norm_g)
    y_p, gla_p, pool_p, kv_p, conv_p = _run_group(
        x_prompt, None, weights, small, dict(in_tm=1024, merge_tm=512, ffn_tm=1024))
    states = dict(gla=state_gla, pool=state_pool, conv=state_ffn_conv,
                  kv=[cache_k_w128, cache_v_w128, cache_k_w512, cache_v_w512, cache_k_w2048, cache_v_w2048])
    m_s = x_sample.shape[0] * x_sample.shape[1]
    y_s, gla_s, pool_s, kv_s, conv_s = _run_group(
        x_sample, states, weights, small, dict(in_tm=m_s, merge_tm=m_s, ffn_tm=m_s))
    k128_p, v128_p, k512_p, v512_p, k2048_p, v2048_p = kv_p
    k128_s, v128_s, k512_s, v512_s, k2048_s, v2048_s = kv_s
    return (y_p, y_s, gla_p, gla_s, pool_p, pool_s, k128_p, k128_s, v128_p, v128_s,
            k512_p, k512_s, v512_p, v512_s, k2048_p, k2048_s, v2048_p, v2048_s, conv_p, conv_s)
```

```python
import functools

import jax
import jax.numpy as jnp
import numpy as np
from jax import lax
from jax.experimental import pallas as pl
from jax.experimental.pallas import tpu as pltpu

F32 = jnp.float32
BF16 = jnp.bfloat16

D_MODEL = 1024
DEPTH = 2
GLA_HEADS = 4
GLA_DK = 128
GLA_DV = 256
GLA_QK = GLA_HEADS * GLA_DK
GLA_VW = GLA_HEADS * GLA_DV
GLA_GATE_RANK = 16
GLA_GATE_NORM = 16.0
GLA_CHUNK = 64
GLA_SCAN_ROWS = 256
POOL_WINDOWS = (2, 4, 8, 16)
POOL_GROUP = 256
POOL_STATE = 15
POOL_HALO = 16
ATT_GROUPS = ((128, 1), (512, 4), (2048, 16))
ATT_HG = 4
ATT_E = 64
ATT_GW = ATT_HG * ATT_E
ATT_SPAN = 128
ATT_HEADS = 12
ATT_WIDTH = ATT_HEADS * ATT_E
D_FF = 2816
NORM_EPS = 1e-6
NEG_INF = -1e30

LANE = 128
SUBLANE = 8
BF16_SUBLANE = 16
VMEM_LIMIT = 56 * 1024 * 1024

_IN_SPLITS = (GLA_QK, GLA_QK, GLA_VW, GLA_VW, GLA_GATE_RANK, D_MODEL, ATT_WIDTH, ATT_WIDTH, ATT_WIDTH,
              D_MODEL, D_MODEL, D_MODEL)
_IN_OFF = [0] + [int(v) for v in np.cumsum(_IN_SPLITS)]
ZG_W = 2 * GLA_QK + 2 * GLA_VW
ZG_K, ZG_V, ZG_R = GLA_QK, 2 * GLA_QK, 2 * GLA_QK + GLA_VW
UA_W = D_MODEL + 3 * ATT_WIDTH
UA_Q, UA_K, UA_V = D_MODEL, D_MODEL + ATT_WIDTH, D_MODEL + 2 * ATT_WIDTH
ZG_TN = 1024
UA_TN = UA_W // 2

_SLOPES = (2.0 ** (-8.0 * np.arange(1, ATT_HEADS + 1) / ATT_HEADS)).astype(np.float32)


def _params(sem):
    return pltpu.CompilerParams(dimension_semantics=sem, vmem_limit_bytes=VMEM_LIMIT)


def _resident(shape):
    return pl.BlockSpec(shape, lambda *_: (0,) * len(shape), pipeline_mode=pl.Buffered(1))


def _act_dtype(t_len):
    return BF16 if t_len % BF16_SUBLANE == 0 else F32


def _dot(a, b):
    return jnp.dot(a, b, preferred_element_type=F32)


def _dot_nt(a, b):
    return lax.dot_general(a, b, (((1,), (1,)), ((), ())), preferred_element_type=F32)


def _dot_tn(a, b):
    return lax.dot_general(a, b, (((0,), (0,)), ((), ())), preferred_element_type=F32)


def _rms(x, g):
    return x * lax.rsqrt(jnp.mean(x * x, axis=-1, keepdims=True) + NORM_EPS) * g


def _sigmoid(x):
    return 1.0 / (1.0 + jnp.exp(-x))


def _in_proj_kernel(x_ref, g_ref, w1_ref, wal_ref, w2_ref, zg_ref, al_ref, ua_ref, xn_ref, *, n1):
    j = pl.program_id(1)

    @pl.when(j == 0)
    def _():
        xn = _rms(x_ref[...], g_ref[...]).astype(BF16)
        xn_ref[...] = xn
        al_ref[...] = _dot(xn, wal_ref[...]).astype(al_ref.dtype)

    @pl.when(j < n1)
    def _():
        zg_ref[...] = _dot(xn_ref[...], w1_ref[...]).astype(zg_ref.dtype)

    @pl.when(j >= n1)
    def _():
        ua_ref[...] = _dot(xn_ref[...], w2_ref[...])


def _in_proj(x2, g, w1, wal, w2, tm, act):
    m = x2.shape[0]
    n1, n2 = ZG_W // ZG_TN, UA_W // UA_TN
    j1 = lambda j: jnp.minimum(j, n1 - 1)
    j2 = lambda j: jnp.maximum(j - n1, 0)
    return pl.pallas_call(
        functools.partial(_in_proj_kernel, n1=n1),
        grid=(m // tm, n1 + n2),
        in_specs=[
            pl.BlockSpec((tm, D_MODEL), lambda i, j: (i, 0)),
            pl.BlockSpec((1, D_MODEL), lambda i, j: (0, 0)),
            pl.BlockSpec((D_MODEL, ZG_TN), lambda i, j: (0, j1(j))),
            pl.BlockSpec((D_MODEL, LANE), lambda i, j: (0, 0)),
            pl.BlockSpec((D_MODEL, UA_TN), lambda i, j: (0, j2(j))),
        ],
        out_specs=[
            pl.BlockSpec((tm, ZG_TN), lambda i, j: (i, j1(j))),
            pl.BlockSpec((tm, LANE), lambda i, j: (i, 0)),
            pl.BlockSpec((tm, UA_TN), lambda i, j: (i, j2(j))),
        ],
        out_shape=[
            jax.ShapeDtypeStruct((m, ZG_W), act),
            jax.ShapeDtypeStruct((m, LANE), act),
            jax.ShapeDtypeStruct((m, UA_W), F32),
        ],
        scratch_shapes=[pltpu.VMEM((tm, D_MODEL), BF16)],
        compiler_params=_params(("parallel", "arbitrary")),
        name="in_proj",
    )(x2, g, w1, wal, w2)


def _gla_kernel(q_ref, k_ref, v_ref, r_ref, a_ref, wa_ref, ba_ref, gn_ref, s0_ref, o_ref, so_ref,
                qe_ref, ke_ref, qi_ref, ks_ref, vb_ref, dec_ref, oacc_ref, kv_ref, st_ref, *, c, nc):
    ce = max(c, BF16_SUBLANE)
    tp = nc * ce
    mid = (c - 1) // 2

    def load(ref):
        x = ref[...].astype(F32)
        if ce != c:
            x = jnp.concatenate([x, jnp.zeros((ce - c, x.shape[1]), F32)], axis=0)
        return x

    z = _dot(load(a_ref).astype(BF16), wa_ref[...]) + ba_ref[...]
    la = (jnp.minimum(z, 0.0) - jnp.log(1.0 + jnp.exp(-jnp.abs(z)))) * (1.0 / GLA_GATE_NORM)
    if ce != c:
        la = jnp.where(lax.broadcasted_iota(jnp.int32, (tp, GLA_DK), 0) < c, la, 0.0)
    slab = min(tp, GLA_SCAN_ROWS)
    ri = lax.broadcasted_iota(jnp.int32, (slab, slab), 0)
    ci = lax.broadcasted_iota(jnp.int32, (slab, slab), 1)
    tri = jnp.where((ri >= ci) & ((ri & -ce) == (ci & -ce)), 1.0, 0.0).astype(BF16)
    la_hi = la.astype(BF16)
    la_lo = (la - la_hi.astype(F32)).astype(BF16)
    b = jnp.concatenate(
        [_dot(tri, la_hi[i * slab:(i + 1) * slab]) + _dot(tri, la_lo[i * slab:(i + 1) * slab])
         for i in range(tp // slab)], axis=0)
    b3 = b.reshape(nc, ce, GLA_DK)
    b_mid = b3[:, mid:mid + 1, :]
    b_last = b3[:, c - 1:c, :]
    q3 = (load(q_ref) * (GLA_DK ** -0.5)).reshape(nc, ce, GLA_DK)
    k3 = load(k_ref).reshape(nc, ce, GLA_DK)
    flat = lambda x: x.reshape(tp, GLA_DK).astype(BF16)
    qe_ref[...] = flat(q3 * jnp.exp(b3 - b_mid))
    ke_ref[...] = flat(k3 * jnp.exp(b_mid - b3))
    qi_ref[...] = flat(q3 * jnp.exp(b3))
    ks_ref[...] = flat(k3 * jnp.exp(b_last - b3))
    dec_ref[...] = jnp.exp(b_last)
    vb_ref[...] = load(v_ref).astype(BF16)
    causal = (lax.broadcasted_iota(jnp.int32, (ce, ce), 0) >= lax.broadcasted_iota(jnp.int32, (ce, ce), 1))
    rows = lambda n: pl.ds(pl.multiple_of(n * ce, ce), ce)

    def intra(n, carry):
        sl = rows(n)
        v = vb_ref[sl, :]
        att = jnp.where(causal, _dot_nt(qe_ref[sl, :], ke_ref[sl, :]), 0.0).astype(BF16)
        oacc_ref[sl, :] = _dot(att, v)
        kv_ref[n] = _dot_tn(v, ks_ref[sl, :])
        return carry

    lax.fori_loop(0, nc, intra, 0, unroll=min(nc, 4))
    st_ref[...] = s0_ref[...].T

    def inter(n, carry):
        sl = rows(n)
        st = st_ref[...]
        oacc_ref[sl, :] += _dot_nt(qi_ref[sl, :], st.astype(BF16))
        st_ref[...] = st * dec_ref[n] + kv_ref[n]
        return carry

    lax.fori_loop(0, nc, inter, 0, unroll=min(nc, 4))
    r = load(r_ref)
    res = _rms(oacc_ref[...], gn_ref[...]) * (r * _sigmoid(r))
    o_ref[...] = res[:nc * c if ce == c else c].astype(o_ref.dtype)
    so_ref[...] = st_ref[...].T


def _gla(zg3, al3, wa2p, ba, gn, s0, t_len):
    bsz = zg3.shape[0]
    c = GLA_CHUNK if t_len % GLA_CHUNK == 0 else t_len
    nc = t_len // c
    ce = max(c, BF16_SUBLANE)
    assert ce == c or nc == 1
    tp = nc * ce
    kern = functools.partial(_gla_kernel, c=c, nc=nc)
    zspec = lambda w, off: pl.BlockSpec((None, t_len, w), lambda b, h: (b, 0, off // w + h))
    return pl.pallas_call(
        kern,
        grid=(bsz, GLA_HEADS),
        in_specs=[
            zspec(GLA_DK, 0), zspec(GLA_DK, ZG_K), zspec(GLA_DV, ZG_V), zspec(GLA_DV, ZG_R),
            pl.BlockSpec((None, t_len, LANE), lambda b, h: (b, 0, 0)),
            pl.BlockSpec((LANE, GLA_DK), lambda b, h: (0, h)),
            pl.BlockSpec((1, GLA_DK), lambda b, h: (0, h)),
            pl.BlockSpec((1, GLA_DV), lambda b, h: (0, 0)),
            pl.BlockSpec((None, None, GLA_DK, GLA_DV), lambda b, h: (b, h, 0, 0)),
        ],
        out_specs=[
            pl.BlockSpec((None, t_len, GLA_DV), lambda b, h: (b, 0, h)),
            pl.BlockSpec((None, None, GLA_DK, GLA_DV), lambda b, h: (b, h, 0, 0)),
        ],
        out_shape=[
            jax.ShapeDtypeStruct((bsz, t_len, GLA_VW), zg3.dtype),
            jax.ShapeDtypeStruct((bsz, GLA_HEADS, GLA_DK, GLA_DV), F32),
        ],
        scratch_shapes=[pltpu.VMEM((tp, GLA_DK), BF16)] * 4 + [
            pltpu.VMEM((tp, GLA_DV), BF16),
            pltpu.VMEM((nc, 1, GLA_DK), F32),
            pltpu.VMEM((tp, GLA_DV), F32),
            pltpu.VMEM((nc, GLA_DV, GLA_DK), F32),
            pltpu.VMEM((GLA_DV, GLA_DK), F32),
        ],
        compiler_params=_params(("parallel", "arbitrary")),
        name="gla",
    )(zg3, zg3, zg3, zg3, al3, wa2p, ba, gn, s0)


def _pool_kernel(u_ref, prev_ref, pw_ref, ps_ref, p_ref, pn_ref, *, t_len, n_prev):
    ext = jnp.concatenate([prev_ref[...], u_ref[...]], axis=0)
    t_abs = lax.broadcasted_iota(jnp.int32, (t_len, 1), 0) + n_prev
    for gi, w in enumerate(POOL_WINDOWS):
        cols = slice(gi * POOL_GROUP, (gi + 1) * POOL_GROUP)
        x = ext[:, cols]
        acc = x
        s = 1
        while s < w:
            acc = acc + pltpu.roll(acc, s, axis=0)
            s *= 2
        cnt = jnp.minimum(t_abs + 1, w).astype(F32)
        pooled = acc[POOL_HALO:] / cnt - x[POOL_HALO:]
        mixed = _dot(pooled.astype(BF16), pw_ref[gi]) * ps_ref[:, cols]
        p_ref[:, cols] = mixed.astype(p_ref.dtype)
    keep = max(0, POOL_STATE - t_len)
    if keep:
        pn_ref[0:keep, :] = prev_ref[POOL_HALO - keep:POOL_HALO, :]
    pn_ref[keep:POOL_STATE, :] = u_ref[t_len - (POOL_STATE - keep):t_len, :]


def _pool(ua3, prev, pw, ps, t_len, n_prev):
    bsz = ua3.shape[0]
    kern = functools.partial(_pool_kernel, t_len=t_len, n_prev=n_prev)
    return pl.pallas_call(
        kern,
        grid=(bsz,),
        in_specs=[
            pl.BlockSpec((None, t_len, D_MODEL), lambda b: (b, 0, 0)),
            pl.BlockSpec((None, POOL_HALO, D_MODEL), lambda b: (b, 0, 0)),
            pl.BlockSpec((4, POOL_GROUP, POOL_GROUP), lambda b: (0, 0, 0)),
            pl.BlockSpec((1, D_MODEL), lambda b: (0, 0)),
        ],
        out_specs=[
            pl.BlockSpec((None, t_len, D_MODEL), lambda b: (b, 0, 0)),
            pl.BlockSpec((None, POOL_STATE, D_MODEL), lambda b: (b, 0, 0)),
        ],
        out_shape=[
            jax.ShapeDtypeStruct((bsz, t_len, D_MODEL), _act_dtype(t_len)),
            jax.ShapeDtypeStruct((bsz, POOL_STATE, D_MODEL), F32),
        ],
        compiler_params=_params(("parallel",)),
        name="pool",
    )(ua3, prev, pw, ps)


def _attn_prompt_kernel(q_ref, k_ref, v_ref, o_ref, l_ref, kc_ref, vc_ref,
                        qm_ref, ks_ref, vs_ref, os_ref, ls_ref, bias_ref, *, t_len, keep, dil, slopes):
    n = t_len // dil
    nb = n // ATT_SPAN
    hp = pl.program_id(1)
    kc_ref[...] = k_ref[t_len - keep:t_len, :]
    vc_ref[...] = v_ref[t_len - keep:t_len, :]
    via_swap = dil % SUBLANE == 0

    def to_res(ref):
        if dil == 1:
            return ref[...][None]
        if via_swap:
            return jnp.swapaxes(ref[...].reshape(n, dil, LANE), 0, 1)
        return jnp.stack([ref[pl.ds(r, n, stride=dil), :] for r in range(dil)], axis=0)

    q3 = to_res(q_ref) * (ATT_E ** -0.5)
    lane3 = lax.broadcasted_iota(jnp.int32, q3.shape, 2)
    qm_ref[0] = jnp.where(lane3 < ATT_E, q3, 0.0).astype(BF16)
    qm_ref[1] = jnp.where(lane3 >= ATT_E, q3, 0.0).astype(BF16)
    ks_ref[...] = to_res(k_ref).astype(BF16)
    vs_ref[...] = to_res(v_ref).astype(BF16)

    nk = ATT_SPAN if nb == 1 else 2 * ATT_SPAN
    a_idx = lax.broadcasted_iota(jnp.int32, (ATT_SPAN, nk), 0)
    c_idx = lax.broadcasted_iota(jnp.int32, (ATT_SPAN, nk), 1)
    for var in range(1 if nb == 1 else 2):
        j = a_idx - c_idx + var * ATT_SPAN
        valid = (j >= 0) & (j <= ATT_SPAN)
        for h in range(2):
            scale = jnp.where(hp == 0, -slopes[h] * dil, -slopes[2 + h] * dil)
            bias_ref[h, var] = jnp.where(valid, j.astype(F32) * scale, NEG_INF)

    lane = lax.broadcasted_iota(jnp.int32, (ATT_SPAN, LANE), 1)
    hm = [lane < ATT_E, lane >= ATT_E]

    def block(t, carry):
        r, qi = t // nb, t % nb
        qsl = pl.ds(pl.multiple_of(qi * ATT_SPAN, ATT_SPAN), ATT_SPAN)
        if nb > 1:
            ksl = pl.ds(pl.multiple_of(jnp.maximum(qi - 1, 0) * ATT_SPAN, ATT_SPAN), nk)
            var = jnp.minimum(qi, 1)
        else:
            ksl, var = qsl, 0
        kk = ks_ref[r, ksl, :]
        vv = vs_ref[r, ksl, :]
        o_acc = jnp.zeros((ATT_SPAN, LANE), F32)
        l_acc = jnp.zeros((ATT_SPAN, LANE), F32)
        for h in range(2):
            s = _dot_nt(qm_ref[h, r, qsl, :], kk) + bias_ref[h, var]
            m = jnp.max(s, axis=-1, keepdims=True)
            p = jnp.exp(s - m)
            den = jnp.sum(p, axis=-1, keepdims=True)
            oh = _dot(p.astype(BF16), vv) / den
            o_acc = jnp.where(hm[h], oh, o_acc)
            l_acc = jnp.where(hm[h], m + jnp.log(den), l_acc)
        os_ref[r, qsl, :] = o_acc
        ls_ref[r, qsl, :] = l_acc
        return carry

    lax.fori_loop(0, dil * nb, block, 0, unroll=2)
    for res_ref, out_ref in ((os_ref, o_ref), (ls_ref, l_ref)):
        if dil == 1:
            out_ref[...] = res_ref[0]
        elif via_swap:
            out_ref[...] = jnp.swapaxes(res_ref[...], 0, 1).reshape(t_len, LANE)
        else:
            for r in range(dil):
                out_ref[pl.ds(r, n, stride=dil), :] = res_ref[r]


def _attn_prompt(ua3, gi):
    bsz, t_len, _ = ua3.shape
    win, dil = ATT_GROUPS[gi]
    keep = min(win, t_len)
    kern = functools.partial(_attn_prompt_kernel, t_len=t_len, keep=keep, dil=dil,
                             slopes=[float(s) for s in _SLOPES[gi * ATT_HG:(gi + 1) * ATT_HG]])
    uspec = lambda off: pl.BlockSpec((None, t_len, LANE), lambda b, hp: (b, 0, (off + gi * ATT_GW) // LANE + hp))
    ospec = lambda rows: pl.BlockSpec((None, rows, LANE), lambda b, hp: (b, 0, hp))
    n = t_len // dil
    nk = ATT_SPAN if n == ATT_SPAN else 2 * ATT_SPAN
    scratch = [pltpu.VMEM((2, dil, n, LANE), BF16), pltpu.VMEM((dil, n, LANE), BF16),
               pltpu.VMEM((dil, n, LANE), BF16), pltpu.VMEM((dil, n, LANE), F32),
               pltpu.VMEM((dil, n, LANE), F32), pltpu.VMEM((2, 2, ATT_SPAN, nk), F32)]
    o, l, kc, vc = pl.pallas_call(
        kern,
        grid=(bsz, ATT_GW // LANE),
        in_specs=[uspec(UA_Q), uspec(UA_K), uspec(UA_V)],
        out_specs=[ospec(t_len), ospec(t_len), ospec(keep), ospec(keep)],
        out_shape=[
            jax.ShapeDtypeStruct((bsz, t_len, ATT_GW), F32),
            jax.ShapeDtypeStruct((bsz, t_len, ATT_GW), F32),
            jax.ShapeDtypeStruct((bsz, keep, ATT_GW), F32),
            jax.ShapeDtypeStruct((bsz, keep, ATT_GW), F32),
        ],
        scratch_shapes=scratch,
        compiler_params=_params(("parallel", "parallel")),
        name=f"attn_prompt_g{gi}",
    )(ua3, ua3, ua3)
    m = bsz * t_len
    return (o.reshape(m, ATT_GW), l.reshape(m, ATT_GW),
            kc.reshape(bsz, keep, ATT_HG, ATT_E), vc.reshape(bsz, keep, ATT_HG, ATT_E))


_KX_PAD = LANE


def _head_masks(rows):
    lane = lax.broadcasted_iota(jnp.int32, (rows, ATT_GW), 1)
    return [(lane >= h * ATT_E) & (lane < (h + 1) * ATT_E) for h in range(ATT_HG)]


def _attn_sample_kernel(*refs, t_len, first, layer_slopes):
    ng = len(ATT_GROUPS)
    qkv = refs[0:3 * ng]
    caches = refs[3 * ng:5 * ng]
    outs = refs[5 * ng:] if first else refs[7 * ng:]
    o_refs, l_refs, cache_out = outs[0:ng], outs[ng:2 * ng], outs[2 * ng:4 * ng]
    kx_ref, vx_ref = outs[4 * ng:]

    def shift_in(c_ref, new_t):
        ext = jnp.concatenate([c_ref[...], new_t], axis=1)
        return ext, pltpu.roll(ext, ext.shape[1] - t_len, axis=1)

    def attend():
        hm = _head_masks(t_len)
        rows = ATT_HG * t_len
        zrows = jnp.zeros((_KX_PAD - t_len, ATT_GW), F32)
        for gi, (win, dil) in enumerate(ATT_GROUPS):
            q_ref, kn_ref, vn_ref = qkv[3 * gi:3 * gi + 3]
            wb = caches[2 * gi].shape[1]
            nk = wb + _KX_PAD
            for c_ref, n_ref, x_ref, co_ref in ((caches[2 * gi], kn_ref, kx_ref, cache_out[2 * gi]),
                                                (caches[2 * gi + 1], vn_ref, vx_ref, cache_out[2 * gi + 1])):
                new_t = jnp.concatenate([n_ref[...], zrows], axis=0).T
                ext, moved = shift_in(c_ref, new_t)
                x_ref[:, 0:nk] = ext.astype(BF16)
                co_ref[...] = moved[:, 0:wb] if first else moved[:, wb - LANE:wb]
            q = q_ref[...] * (ATT_E ** -0.5)
            qst = jnp.concatenate([jnp.where(hm[h], q, 0.0) for h in range(ATT_HG)], axis=0).astype(BF16)
            s = _dot(qst, kx_ref[:, 0:nk])
            ri = lax.broadcasted_iota(jnp.int32, (rows, nk), 0)
            ci = lax.broadcasted_iota(jnp.int32, (rows, nk), 1)
            dist = wb + (ri % t_len) - ci
            valid = (dist >= 0) & (dist <= win) & ((dist & (dil - 1)) == 0)
            hrow = lax.broadcasted_iota(jnp.int32, (rows, 1), 0) // t_len
            slope = jnp.zeros((rows, 1), F32)
            for h in range(ATT_HG):
                slope = jnp.where(hrow == h, float(layer_slopes[gi * ATT_HG + h]), slope)
            s = jnp.where(valid, s - slope * dist.astype(F32), NEG_INF)
            m = jnp.max(s, axis=-1, keepdims=True)
            p = jnp.exp(s - m)
            den = jnp.sum(p, axis=-1, keepdims=True)
            ost = _dot_nt(p.astype(BF16), vx_ref[:, 0:nk]) / den
            lst = m + jnp.log(den)
            o = jnp.zeros((t_len, ATT_GW), F32)
            l = jnp.zeros((t_len, ATT_GW), F32)
            for h in range(ATT_HG):
                o = jnp.where(hm[h], ost[h * t_len:(h + 1) * t_len], o)
                l = jnp.where(hm[h], lst[h * t_len:(h + 1) * t_len], l)
            o_refs[gi][...] = o
            l_refs[gi][...] = l

    if not first:
        attend()
        return
    pl.when(pl.program_id(1) == 0)(attend)

    @pl.when(pl.program_id(1) > 0)
    def _():
        for c_ref, co_ref in zip(caches, cache_out):
            _, moved = shift_in(c_ref, jnp.zeros((ATT_GW, _KX_PAD), F32))
            co_ref[...] = moved[:, 0:c_ref.shape[1]]


def _attn_sample(ua3, caches_t, layer, cache_prev):
    bsz, t_len, _ = ua3.shape
    ng = len(ATT_GROUPS)
    first = cache_prev is None
    kern = functools.partial(_attn_sample_kernel, t_len=t_len, first=first,
                             layer_slopes=[float(s) for s in _SLOPES])
    in_specs, args = [], []
    for gi in range(ng):
        for off in (UA_Q, UA_K, UA_V):
            in_specs.append(pl.BlockSpec((None, t_len, ATT_GW), lambda b, *_, o=off // ATT_GW + gi: (b, 0, o)))
            args.append(ua3)
    small = pl.BlockSpec((None, t_len, ATT_GW), lambda b, *_: (b, 0, 0))
    out_specs = [small] * (2 * ng)
    out_shape = [jax.ShapeDtypeStruct((bsz, t_len, ATT_GW), F32)] * (2 * ng)
    wbs = [c.shape[3] for c in caches_t]
    for c, wb in zip(caches_t, wbs):
        whole = (None, None, ATT_GW, wb)
        if first:
            in_specs.append(pl.BlockSpec(whole, lambda b, l: (l, b, 0, 0)))
            out_specs.append(pl.BlockSpec(whole, lambda b, l: (l, b, 0, 0)))
        else:
            in_specs.append(pl.BlockSpec(whole, lambda b: (layer, b, 0, 0)))
            out_specs.append(pl.BlockSpec((None, None, ATT_GW, LANE), lambda b, t=wb // LANE - 1: (layer, b, 0, t)))
        args.append(c)
        out_shape.append(jax.ShapeDtypeStruct(c.shape, F32))
    aliases = {}
    if not first:
        for i, c in enumerate(cache_prev):
            in_specs.append(pl.BlockSpec(memory_space=pl.ANY))
            args.append(c)
            aliases[5 * ng + i] = 2 * ng + i
    res = pl.pallas_call(
        kern,
        grid=(bsz, DEPTH) if first else (bsz,),
        in_specs=in_specs,
        out_specs=out_specs,
        out_shape=out_shape,
        scratch_shapes=[pltpu.VMEM((ATT_GW, max(wbs) + _KX_PAD), BF16)] * 2,
        input_output_aliases=aliases,
        compiler_params=_params(("parallel", "arbitrary") if first else ("parallel",)),
        name="attn_sample",
    )(*args)
    m = bsz * t_len
    o = [r.reshape(m, ATT_GW) for r in res[0:ng]]
    l = [r.reshape(m, ATT_GW) for r in res[ng:2 * ng]]
    return o, l, list(res[2 * ng:])


def _merge_kernel(x_ref, og_ref, p_ref, o0_ref, o1_ref, o2_ref, l0_ref, l1_ref, l2_ref, g_ref, wg_ref,
                  woa_ref, wob_ref, woc_ref, wout_ref, h_ref):
    x = x_ref[...]
    xn = _rms(x, g_ref[...]).astype(BF16)

    def gate(i):
        return _sigmoid(_dot(xn, wg_ref[:, i * D_MODEL:(i + 1) * D_MODEL]))

    merged = gate(0) * _dot(og_ref[...].astype(BF16), woa_ref[...])
    merged = merged + gate(1) * _dot(p_ref[...].astype(BF16), wob_ref[...])
    ls = [l0_ref[...], l1_ref[...], l2_ref[...]]
    m = jnp.maximum(jnp.maximum(ls[0], ls[1]), ls[2])
    es = [jnp.exp(l - m) for l in ls]
    inv = 1.0 / (es[0] + es[1] + es[2])
    yc = jnp.zeros_like(x)
    for gi, o_ref in enumerate((o0_ref, o1_ref, o2_ref)):
        oc = (o_ref[...] * (es[gi] * inv)).astype(BF16)
        yc = yc + _dot(oc, woc_ref[gi * ATT_GW:(gi + 1) * ATT_GW, :])
    merged = merged + gate(2) * yc
    h_ref[...] = x + _dot(merged.astype(BF16), wout_ref[...])


def _merge(x2, og, p, o, l, g, wg, woa, wob, woc, wout, tm):
    m = x2.shape[0]
    row = lambda w: pl.BlockSpec((tm, w), lambda i: (i, 0))
    return pl.pallas_call(
        _merge_kernel,
        grid=(m // tm,),
        in_specs=[row(D_MODEL), row(GLA_VW), row(D_MODEL)] + [row(ATT_GW)] * 6
                 + [_resident(a.shape) for a in (g, wg, woa, wob, woc, wout)],
        out_specs=row(D_MODEL),
        out_shape=jax.ShapeDtypeStruct((m, D_MODEL), F32),
        compiler_params=_params(("parallel",)),
        name="merge",
    )(x2, og, p, *o, *l, g, wg, woa, wob, woc, wout)


FFN_TF = 256
FFN_HALO = 16


def _ffn_kernel(*refs, t_len, tm, use_halo, has_prev, final):
    it = iter(refs)
    h_ref = next(it)
    halo_ref = next(it) if use_halo else None
    g_ref, wup_ref, cw_ref, cb_ref, wd_ref, fg_ref = (next(it) for _ in range(6))
    c1_ref, c2_ref = (next(it), next(it)) if has_prev else (None, None)
    y_ref, cv_ref = next(it), next(it)
    hn_ref, hh_ref = next(it), next(it)
    a_scr = None if use_halo else next(it)
    ho = FFN_HALO if use_halo else 0
    if use_halo:
        hn_ref[0:ho, :] = _rms(halo_ref[...], g_ref[...]).astype(BF16)
    hn_ref[ho:ho + tm, :] = _rms(h_ref[...], g_ref[...]).astype(BF16)
    t_loc = (lax.broadcasted_iota(jnp.int32, (tm, 1), 0) + pl.program_id(0) * tm) % t_len
    for jt in range(D_FF // FFN_TF):
        cols = slice(jt * FFN_TF, (jt + 1) * FFN_TF)
        a_ext = _dot(hn_ref[...], wup_ref[:, cols])
        a = a_ext[ho:]
        bg = _dot(hn_ref[ho:ho + tm, :], wup_ref[:, D_FF + jt * FFN_TF:D_FF + (jt + 1) * FFN_TF])
        if use_halo:
            a1, a2 = a_ext[ho - 1:ho - 1 + tm], a_ext[ho - 2:ho - 2 + tm]
        else:
            a1, a2 = pltpu.roll(a, 1, axis=0), pltpu.roll(a, 2, axis=0)
        a1 = jnp.where(t_loc >= 1, a1, c1_ref[:, cols] if has_prev else 0.0)
        a2 = jnp.where(t_loc >= 2, a2, c2_ref[:, cols] if has_prev else 0.0)
        y = cb_ref[:, cols] + cw_ref[0:1, cols] * a2 + cw_ref[1:2, cols] * a1 + cw_ref[2:3, cols] * a
        hh_ref[:, cols] = ((y * _sigmoid(y)) * bg).astype(BF16)
        if use_halo:
            cv_ref[:, cols] = a[tm - 2:tm]
        else:
            nseq = tm // t_len
            for kk in range(FFN_TF // LANE):
                lc = slice(kk * LANE, (kk + 1) * LANE)
                oc = slice(jt * FFN_TF + kk * LANE, jt * FFN_TF + (kk + 1) * LANE)
                a_scr[kk] = a[:, lc]
                cv_ref[0, :, oc] = a_scr[kk, pl.ds(t_len - 2, nseq, stride=t_len), :]
                cv_ref[1, :, oc] = a_scr[kk, pl.ds(t_len - 1, nseq, stride=t_len), :]
    out = h_ref[...] + _dot(hh_ref[...], wd_ref[...])
    if final:
        out = _rms(out, fg_ref[...])
    y_ref[...] = out


def _ffn(h2, g, wup, cw, cb, wd, fg, prev, t_len, tm, final):
    m = h2.shape[0]
    bsz = m // t_len
    use_halo = tm % t_len != 0
    has_prev = prev is not None
    kern = functools.partial(_ffn_kernel, t_len=t_len, tm=tm, use_halo=use_halo, has_prev=has_prev, final=final)
    in_specs = [pl.BlockSpec((tm, D_MODEL), lambda i: (i, 0))]
    args = [h2]
    if use_halo:
        hb = tm // FFN_HALO
        in_specs.append(pl.BlockSpec((FFN_HALO, D_MODEL), lambda i: (jnp.maximum(i * hb - 1, 0), 0)))
        args.append(h2)
    in_specs += [_resident(a.shape) for a in (g, wup, cw, cb, wd, fg)]
    args += [g, wup, cw, cb, wd, fg]
    if has_prev:
        z = jnp.zeros((bsz, t_len - 2, D_FF), F32)
        c1 = jnp.concatenate([prev[:, 1:2], jnp.zeros((bsz, 1, D_FF), F32), z], axis=1).reshape(m, D_FF)
        c2 = jnp.concatenate([prev, z], axis=1).reshape(m, D_FF)
        in_specs += [pl.BlockSpec((tm, D_FF), lambda i: (i, 0))] * 2
        args += [c1, c2]
    scratch = [pltpu.VMEM(((FFN_HALO if use_halo else 0) + tm, D_MODEL), BF16), pltpu.VMEM((tm, D_FF), BF16)]
    if use_halo:
        cv_spec = pl.BlockSpec((None, 2, D_FF), lambda i: (i, 0, 0))
        cv_shape = jax.ShapeDtypeStruct((m // tm, 2, D_FF), F32)
    else:
        nseq = tm // t_len
        cv_spec = pl.BlockSpec((2, nseq, D_FF), lambda i: (0, i, 0))
        cv_shape = jax.ShapeDtypeStruct((2, bsz, D_FF), F32)
        scratch.append(pltpu.VMEM((FFN_TF // LANE, tm, LANE), F32))
    y, cv = pl.pallas_call(
        kern,
        grid=(m // tm,),
        in_specs=in_specs,
        out_specs=[pl.BlockSpec((tm, D_MODEL), lambda i: (i, 0)), cv_spec],
        out_shape=[jax.ShapeDtypeStruct((m, D_MODEL), F32), cv_shape],
        scratch_shapes=scratch,
        compiler_params=_params(("parallel",)),
        name="ffn",
    )(*args)
    if use_halo:
        per = t_len // tm
        cv = cv[per - 1::per]
    else:
        cv = jnp.swapaxes(cv, 0, 1)
    return y, cv


def _prep_weights(w_in, gla_wa2, pool_w, w_oa, w_ob, w_oc, w_out, ffn_w_up, ffn_w_down):
    o = _IN_OFF
    cast = lambda a: a.astype(BF16)
    w1 = cast(w_in[:, :, o[0]:o[4]])
    wal = cast(jnp.pad(w_in[:, :, o[4]:o[5]], ((0, 0), (0, 0), (0, LANE - GLA_GATE_RANK))))
    w2 = cast(w_in[:, :, o[5]:o[9]])
    wg = cast(w_in[:, :, o[9]:o[12]])
    wa2p = cast(jnp.pad(gla_wa2, ((0, 0), (0, LANE - GLA_GATE_RANK), (0, 0))))
    return (w1, wal, w2, wg, wa2p, cast(pool_w), cast(w_oa), cast(w_ob), cast(w_oc), cast(w_out),
            cast(ffn_w_up), cast(ffn_w_down))


def _cache_channel_major(c):
    d, b, w = c.shape[:3]
    return jnp.transpose(c, (0, 1, 3, 4, 2)).reshape(d, b, ATT_GW, w)


def _cache_token_major(ct):
    d, b, _, w = ct.shape
    return jnp.transpose(ct.reshape(d, b, ATT_HG, ATT_E, w), (0, 1, 4, 2, 3))


def _run_group(x, states, weights, small, tiles):
    bsz, t_len, _ = x.shape
    m = bsz * t_len
    prompt = states is None
    act = _act_dtype(t_len)
    w1, wal, w2, wg, wa2p, pw, woa, wob, woc, wout, wup, wdn = weights
    norm1_g, norm2_g, gla_ba, gla_norm_g, pool_scale, conv_w, conv_b, final_g = small
    x2 = x.reshape(m, D_MODEL)
    new_gla, new_pool, new_kv, new_conv = [], [], [], []
    kv_t = None if prompt else [_cache_channel_major(c) for c in states["kv"]]
    kv_run = None
    for l in range(DEPTH):
        zg, al, ua = _in_proj(x2, norm1_g[l][None], w1[l], wal[l], w2[l], tiles["in_tm"], act)
        zg3 = zg.reshape(bsz, t_len, ZG_W)
        al3 = al.reshape(bsz, t_len, LANE)
        ua3 = ua.reshape(bsz, t_len, UA_W)
        if prompt:
            s0 = jnp.zeros((bsz, GLA_HEADS, GLA_DK, GLA_DV), F32)
            prev = jnp.zeros((bsz, POOL_HALO, D_MODEL), F32)
            n_prev = 0
        else:
            s0 = states["gla"][l]
            prev = jnp.pad(states["pool"][l], ((0, 0), (POOL_HALO - POOL_STATE, 0), (0, 0)))
            n_prev = POOL_STATE
        og, g_new = _gla(zg3, al3, wa2p[l], gla_ba[l][None], gla_norm_g[l][None], s0, t_len)
        pb, p_new = _pool(ua3, prev, pw[l], pool_scale[l][None], t_len, n_prev)
        if prompt:
            o, lse, kv = [], [], []
            for gi in range(len(ATT_GROUPS)):
                og_i, l_i, kc, vc = _attn_prompt(ua3, gi)
                o.append(og_i)
                lse.append(l_i)
                kv += [kc, vc]
        else:
            o, lse, kv_run = _attn_sample(ua3, kv_t, l, kv_run)
            kv = None
        h2 = _merge(x2, og.reshape(m, GLA_VW), pb.reshape(m, D_MODEL), o, lse, norm1_g[l][None], wg[l],
                    woa[l], wob[l], woc[l], wout[l], tiles["merge_tm"])
        x2, c_new = _ffn(h2, norm2_g[l][None], wup[l], conv_w[l], conv_b[l][None], wdn[l], final_g[None],
                         None if prompt else states["conv"][l], t_len, tiles["ffn_tm"], l == DEPTH - 1)
        new_gla.append(g_new)
        new_pool.append(p_new)
        new_kv.append(kv)
        new_conv.append(c_new)
    if prompt:
        kv_out = [jnp.stack([kvl[i] for kvl in new_kv], axis=0) for i in range(2 * len(ATT_GROUPS))]
    else:
        kv_out = [_cache_token_major(c) for c in kv_run]
    return (x2.reshape(bsz, t_len, D_MODEL), jnp.stack(new_gla, 0), jnp.stack(new_pool, 0), kv_out,
            jnp.stack(new_conv, 0))


def kernel(x_prompt, x_sample, state_gla, state_pool, cache_k_w128, cache_v_w128, cache_k_w512, cache_v_w512,
           cache_k_w2048, cache_v_w2048, state_ffn_conv, norm1_g, norm2_g, w_in, gla_wa2, gla_ba, gla_norm_g,
           pool_w, pool_scale, w_oa, w_ob, w_oc, w_out, ffn_w_up, ffn_conv_w, ffn_conv_b, ffn_w_down,
           final_norm_g):
    weights = _prep_weights(w_in, gla_wa2, pool_w, w_oa, w_ob, w_oc, w_out, ffn_w_up, ffn_w_down)
    small = (norm1_g, norm2_g, gla_ba, gla_norm_g, pool_scale, ffn_conv_w, ffn_conv_b, final_norm_g)
    y_p, gla_p, pool_p, kv_p, conv_p = _run_group(
        x_prompt, None, weights, small, dict(in_tm=1024, merge_tm=512, ffn_tm=1024))
    states = dict(gla=state_gla, pool=state_pool, conv=state_ffn_conv,
                  kv=[cache_k_w128, cache_v_w128, cache_k_w512, cache_v_w512, cache_k_w2048, cache_v_w2048])
    m_s = x_sample.shape[0] * x_sample.shape[1]
    y_s, gla_s, pool_s, kv_s, conv_s = _run_group(
        x_sample, states, weights, small, dict(in_tm=m_s, merge_tm=m_s, ffn_tm=m_s))
    k128_p, v128_p, k512_p, v512_p, k2048_p, v2048_p = kv_p
    k128_s, v128_s, k512_s, v512_s, k2048_s, v2048_s = kv_s
    return (y_p, y_s, gla_p, gla_s, pool_p, pool_s, k128_p, k128_s, v128_p, v128_s,
            k512_p, k512_s, v512_p, v512_s, k2048_p, k2048_s, v2048_p, v2048_s, conv_p, conv_s)
```

```python
import functools

import jax
import jax.numpy as jnp
import numpy as np
from jax import lax
from jax.experimental import pallas as pl
from jax.experimental.pallas import tpu as pltpu

F32 = jnp.float32
BF16 = jnp.bfloat16

D_MODEL = 1024
DEPTH = 2
GLA_HEADS = 4
GLA_DK = 128
GLA_DV = 256
GLA_QK = GLA_HEADS * GLA_DK
GLA_VW = GLA_HEADS * GLA_DV
GLA_GATE_RANK = 16
GLA_GATE_NORM = 16.0
GLA_CHUNK = 64
GLA_SCAN_ROWS = 256
GLA_UNROLL = 8
POOL_WINDOWS = (2, 4, 8, 16)
POOL_GROUP = 256
POOL_STATE = 15
POOL_HALO = 16
ATT_GROUPS = ((128, 1), (512, 4), (2048, 16))
ATT_HG = 4
ATT_E = 64
ATT_GW = ATT_HG * ATT_E
ATT_SPAN = 128
ATT_UNROLL = 4
ATT_HEADS = 12
ATT_WIDTH = ATT_HEADS * ATT_E
D_FF = 2816
NORM_EPS = 1e-6
NEG_INF = -1e30

LANE = 128
SUBLANE = 8
BF16_SUBLANE = 16
VMEM_LIMIT = 56 * 1024 * 1024

_IN_SPLITS = (GLA_QK, GLA_QK, GLA_VW, GLA_VW, GLA_GATE_RANK, D_MODEL, ATT_WIDTH, ATT_WIDTH, ATT_WIDTH,
              D_MODEL, D_MODEL, D_MODEL)
_IN_OFF = [0] + [int(v) for v in np.cumsum(_IN_SPLITS)]
ZG_W = 2 * GLA_QK + 2 * GLA_VW
ZG_K, ZG_V, ZG_R = GLA_QK, 2 * GLA_QK, 2 * GLA_QK + GLA_VW
UA_W = D_MODEL + 3 * ATT_WIDTH
UA_Q, UA_K, UA_V = D_MODEL, D_MODEL + ATT_WIDTH, D_MODEL + 2 * ATT_WIDTH
ZG_TN = 1024
UA_TN = UA_W // 2

_SLOPES = (2.0 ** (-8.0 * np.arange(1, ATT_HEADS + 1) / ATT_HEADS)).astype(np.float32)


def _params(sem):
    return pltpu.CompilerParams(dimension_semantics=sem, vmem_limit_bytes=VMEM_LIMIT)


def _resident(shape):
    return pl.BlockSpec(shape, lambda *_: (0,) * len(shape), pipeline_mode=pl.Buffered(1))


def _act_dtype(t_len):
    return BF16 if t_len % BF16_SUBLANE == 0 else F32


def _dot(a, b):
    return jnp.dot(a, b, preferred_element_type=F32)


def _dot_nt(a, b):
    return lax.dot_general(a, b, (((1,), (1,)), ((), ())), preferred_element_type=F32)


def _dot_tn(a, b):
    return lax.dot_general(a, b, (((0,), (0,)), ((), ())), preferred_element_type=F32)


def _rms(x, g):
    return x * lax.rsqrt(jnp.mean(x * x, axis=-1, keepdims=True) + NORM_EPS) * g


def _sigmoid(x):
    return 1.0 / (1.0 + jnp.exp(-x))


def _in_proj_kernel(x_ref, g_ref, w1_ref, wal_ref, w2_ref, zg_ref, al_ref, ua_ref, xn_ref, *, n1):
    j = pl.program_id(1)

    @pl.when(j == 0)
    def _():
        xn = _rms(x_ref[...], g_ref[...]).astype(BF16)
        xn_ref[...] = xn
        al_ref[...] = _dot(xn, wal_ref[...]).astype(al_ref.dtype)

    @pl.when(j < n1)
    def _():
        zg_ref[...] = _dot(xn_ref[...], w1_ref[...]).astype(zg_ref.dtype)

    @pl.when(j >= n1)
    def _():
        ua_ref[...] = _dot(xn_ref[...], w2_ref[...])


def _in_proj(x2, g, w1, wal, w2, tm, act):
    m = x2.shape[0]
    n1, n2 = ZG_W // ZG_TN, UA_W // UA_TN
    j1 = lambda j: jnp.minimum(j, n1 - 1)
    j2 = lambda j: jnp.maximum(j - n1, 0)
    return pl.pallas_call(
        functools.partial(_in_proj_kernel, n1=n1),
        grid=(m // tm, n1 + n2),
        in_specs=[
            pl.BlockSpec((tm, D_MODEL), lambda i, j: (i, 0)),
            pl.BlockSpec((1, D_MODEL), lambda i, j: (0, 0)),
            pl.BlockSpec((D_MODEL, ZG_TN), lambda i, j: (0, j1(j))),
            pl.BlockSpec((D_MODEL, LANE), lambda i, j: (0, 0)),
            pl.BlockSpec((D_MODEL, UA_TN), lambda i, j: (0, j2(j))),
        ],
        out_specs=[
            pl.BlockSpec((tm, ZG_TN), lambda i, j: (i, j1(j))),
            pl.BlockSpec((tm, LANE), lambda i, j: (i, 0)),
            pl.BlockSpec((tm, UA_TN), lambda i, j: (i, j2(j))),
        ],
        out_shape=[
            jax.ShapeDtypeStruct((m, ZG_W), act),
            jax.ShapeDtypeStruct((m, LANE), act),
            jax.ShapeDtypeStruct((m, UA_W), F32),
        ],
        scratch_shapes=[pltpu.VMEM((tm, D_MODEL), BF16)],
        compiler_params=_params(("parallel", "arbitrary")),
        name="in_proj",
    )(x2, g, w1, wal, w2)


def _gla_kernel(q_ref, k_ref, v_ref, r_ref, a_ref, wa_ref, ba_ref, gn_ref, s0_ref, o_ref, so_ref,
                qe_ref, ke_ref, qi_ref, ks_ref, vb_ref, att_ref, dec_ref, oacc_ref, kv_ref, st_ref, *, c, nc):
    ce = max(c, BF16_SUBLANE)
    tp = nc * ce
    mid = (c - 1) // 2

    def load(ref):
        x = ref[...].astype(F32)
        if ce != c:
            x = jnp.concatenate([x, jnp.zeros((ce - c, x.shape[1]), F32)], axis=0)
        return x

    z = _dot(load(a_ref).astype(BF16), wa_ref[...]) + ba_ref[...]
    la = (jnp.minimum(z, 0.0) - jnp.log(1.0 + jnp.exp(-jnp.abs(z)))) * (1.0 / GLA_GATE_NORM)
    if ce != c:
        la = jnp.where(lax.broadcasted_iota(jnp.int32, (tp, GLA_DK), 0) < c, la, 0.0)
    slab = min(tp, GLA_SCAN_ROWS)
    ri = lax.broadcasted_iota(jnp.int32, (slab, slab), 0)
    ci = lax.broadcasted_iota(jnp.int32, (slab, slab), 1)
    tri = jnp.where((ri >= ci) & ((ri & -ce) == (ci & -ce)), 1.0, 0.0).astype(BF16)
    la_hi = la.astype(BF16)
    la_lo = (la - la_hi.astype(F32)).astype(BF16)
    b = jnp.concatenate(
        [_dot(tri, la_hi[i * slab:(i + 1) * slab]) + _dot(tri, la_lo[i * slab:(i + 1) * slab])
         for i in range(tp // slab)], axis=0)
    b3 = b.reshape(nc, ce, GLA_DK)
    b_mid = b3[:, mid:mid + 1, :]
    b_last = b3[:, c - 1:c, :]
    q3 = (load(q_ref) * (GLA_DK ** -0.5)).reshape(nc, ce, GLA_DK)
    k3 = load(k_ref).reshape(nc, ce, GLA_DK)
    flat = lambda x: x.reshape(tp, GLA_DK).astype(BF16)
    qe_ref[...] = flat(q3 * jnp.exp(b3 - b_mid))
    ke_ref[...] = flat(k3 * jnp.exp(b_mid - b3))
    qi_ref[...] = flat(q3 * jnp.exp(b3))
    ks_ref[...] = flat(k3 * jnp.exp(b_last - b3))
    dec_ref[...] = jnp.exp(b_last)
    vb_ref[...] = load(v_ref).astype(BF16)
    causal = (lax.broadcasted_iota(jnp.int32, (ce, ce), 0) >= lax.broadcasted_iota(jnp.int32, (ce, ce), 1))
    rows = lambda n: pl.ds(pl.multiple_of(n * ce, ce), ce)

    def scores(n, carry):
        sl = rows(n)
        att_ref[sl, :] = jnp.where(causal, _dot_nt(qe_ref[sl, :], ke_ref[sl, :]), 0.0).astype(BF16)
        return carry

    def within(n, carry):
        sl = rows(n)
        oacc_ref[sl, :] = _dot(att_ref[sl, :], vb_ref[sl, :])
        return carry

    def increments(n, carry):
        sl = rows(n)
        kv_ref[n] = _dot_tn(vb_ref[sl, :], ks_ref[sl, :])
        return carry

    def across(n, carry):
        sl = rows(n)
        st = st_ref[...]
        oacc_ref[sl, :] += _dot_nt(qi_ref[sl, :], st.astype(BF16))
        st_ref[...] = st * dec_ref[n] + kv_ref[n]
        return carry

    unroll = min(nc, GLA_UNROLL)
    lax.fori_loop(0, nc, scores, 0, unroll=unroll)
    lax.fori_loop(0, nc, within, 0, unroll=unroll)
    lax.fori_loop(0, nc, increments, 0, unroll=unroll)
    st_ref[...] = s0_ref[...].T
    lax.fori_loop(0, nc, across, 0, unroll=unroll)
    r = load(r_ref)
    res = _rms(oacc_ref[...], gn_ref[...]) * (r * _sigmoid(r))
    o_ref[...] = res[:nc * c if ce == c else c].astype(o_ref.dtype)
    so_ref[...] = st_ref[...].T


def _gla(zg3, al3, wa2p, ba, gn, s0, t_len):
    bsz = zg3.shape[0]
    c = GLA_CHUNK if t_len % GLA_CHUNK == 0 else t_len
    nc = t_len // c
    ce = max(c, BF16_SUBLANE)
    assert ce == c or nc == 1
    tp = nc * ce
    kern = functools.partial(_gla_kernel, c=c, nc=nc)
    zspec = lambda w, off: pl.BlockSpec((None, t_len, w), lambda b, h: (b, 0, off // w + h))
    return pl.pallas_call(
        kern,
        grid=(bsz, GLA_HEADS),
        in_specs=[
            zspec(GLA_DK, 0), zspec(GLA_DK, ZG_K), zspec(GLA_DV, ZG_V), zspec(GLA_DV, ZG_R),
            pl.BlockSpec((None, t_len, LANE), lambda b, h: (b, 0, 0)),
            pl.BlockSpec((LANE, GLA_DK), lambda b, h: (0, h)),
            pl.BlockSpec((1, GLA_DK), lambda b, h: (0, h)),
            pl.BlockSpec((1, GLA_DV), lambda b, h: (0, 0)),
            pl.BlockSpec((None, None, GLA_DK, GLA_DV), lambda b, h: (b, h, 0, 0)),
        ],
        out_specs=[
            pl.BlockSpec((None, t_len, GLA_DV), lambda b, h: (b, 0, h)),
            pl.BlockSpec((None, None, GLA_DK, GLA_DV), lambda b, h: (b, h, 0, 0)),
        ],
        out_shape=[
            jax.ShapeDtypeStruct((bsz, t_len, GLA_VW), zg3.dtype),
            jax.ShapeDtypeStruct((bsz, GLA_HEADS, GLA_DK, GLA_DV), F32),
        ],
        scratch_shapes=[pltpu.VMEM((tp, GLA_DK), BF16)] * 4 + [
            pltpu.VMEM((tp, GLA_DV), BF16),
            pltpu.VMEM((tp, ce), BF16),
            pltpu.VMEM((nc, 1, GLA_DK), F32),
            pltpu.VMEM((tp, GLA_DV), F32),
            pltpu.VMEM((nc, GLA_DV, GLA_DK), F32),
            pltpu.VMEM((GLA_DV, GLA_DK), F32),
        ],
        compiler_params=_params(("parallel", "arbitrary")),
        name="gla",
    )(zg3, zg3, zg3, zg3, al3, wa2p, ba, gn, s0)


def _pool_kernel(u_ref, prev_ref, pw_ref, ps_ref, p_ref, pn_ref, *, t_len, n_prev):
    ext = jnp.concatenate([prev_ref[...], u_ref[...]], axis=0)
    t_abs = lax.broadcasted_iota(jnp.int32, (t_len, 1), 0) + n_prev
    for gi, w in enumerate(POOL_WINDOWS):
        cols = slice(gi * POOL_GROUP, (gi + 1) * POOL_GROUP)
        x = ext[:, cols]
        acc = x
        s = 1
        while s < w:
            acc = acc + pltpu.roll(acc, s, axis=0)
            s *= 2
        cnt = jnp.minimum(t_abs + 1, w).astype(F32)
        pooled = acc[POOL_HALO:] / cnt - x[POOL_HALO:]
        mixed = _dot(pooled.astype(BF16), pw_ref[gi]) * ps_ref[:, cols]
        p_ref[:, cols] = mixed.astype(p_ref.dtype)
    keep = max(0, POOL_STATE - t_len)
    if keep:
        pn_ref[0:keep, :] = prev_ref[POOL_HALO - keep:POOL_HALO, :]
    pn_ref[keep:POOL_STATE, :] = u_ref[t_len - (POOL_STATE - keep):t_len, :]


def _pool(ua3, prev, pw, ps, t_len, n_prev):
    bsz = ua3.shape[0]
    kern = functools.partial(_pool_kernel, t_len=t_len, n_prev=n_prev)
    return pl.pallas_call(
        kern,
        grid=(bsz,),
        in_specs=[
            pl.BlockSpec((None, t_len, D_MODEL), lambda b: (b, 0, 0)),
            pl.BlockSpec((None, POOL_HALO, D_MODEL), lambda b: (b, 0, 0)),
            pl.BlockSpec((4, POOL_GROUP, POOL_GROUP), lambda b: (0, 0, 0)),
            pl.BlockSpec((1, D_MODEL), lambda b: (0, 0)),
        ],
        out_specs=[
            pl.BlockSpec((None, t_len, D_MODEL), lambda b: (b, 0, 0)),
            pl.BlockSpec((None, POOL_STATE, D_MODEL), lambda b: (b, 0, 0)),
        ],
        out_shape=[
            jax.ShapeDtypeStruct((bsz, t_len, D_MODEL), _act_dtype(t_len)),
            jax.ShapeDtypeStruct((bsz, POOL_STATE, D_MODEL), F32),
        ],
        compiler_params=_params(("parallel",)),
        name="pool",
    )(ua3, prev, pw, ps)


def _attn_prompt_kernel(q_ref, k_ref, v_ref, o_ref, l_ref, kc_ref, vc_ref,
                        qm_ref, ks_ref, vs_ref, os_ref, ls_ref, bias_ref, s_ref, m_ref,
                        *, t_len, keep, dil, slopes):
    n = t_len // dil
    nb = n // ATT_SPAN
    hp = pl.program_id(1)
    kc_ref[...] = k_ref[t_len - keep:t_len, :]
    vc_ref[...] = v_ref[t_len - keep:t_len, :]
    via_swap = dil % SUBLANE == 0

    def to_res(ref):
        if dil == 1:
            return ref[...][None]
        if via_swap:
            return jnp.swapaxes(ref[...].reshape(n, dil, LANE), 0, 1)
        return jnp.stack([ref[pl.ds(r, n, stride=dil), :] for r in range(dil)], axis=0)

    q3 = to_res(q_ref) * (ATT_E ** -0.5)
    lane3 = lax.broadcasted_iota(jnp.int32, q3.shape, 2)
    qm_ref[0] = jnp.where(lane3 < ATT_E, q3, 0.0).astype(BF16)
    qm_ref[1] = jnp.where(lane3 >= ATT_E, q3, 0.0).astype(BF16)
    ks_ref[...] = to_res(k_ref).astype(BF16)
    vs_ref[:, :, 0:LANE] = to_res(v_ref).astype(BF16)
    vs_ref[:, :, LANE:2 * LANE] = jnp.ones((dil, n, LANE), BF16)

    nk = ATT_SPAN if nb == 1 else 2 * ATT_SPAN
    a_idx = lax.broadcasted_iota(jnp.int32, (ATT_SPAN, nk), 0)
    c_idx = lax.broadcasted_iota(jnp.int32, (ATT_SPAN, nk), 1)
    for var in range(1 if nb == 1 else 2):
        j = a_idx - c_idx + var * ATT_SPAN
        valid = (j >= 0) & (j <= ATT_SPAN)
        for h in range(2):
            scale = jnp.where(hp == 0, -slopes[h] * dil, -slopes[2 + h] * dil)
            bias_ref[h, var] = jnp.where(valid, j.astype(F32) * scale, NEG_INF)

    lane = lax.broadcasted_iota(jnp.int32, (ATT_SPAN, LANE), 1)
    hm = [lane < ATT_E, lane >= ATT_E]

    def slices(t):
        r, qi = t // nb, t % nb
        qsl = pl.ds(pl.multiple_of(qi * ATT_SPAN, ATT_SPAN), ATT_SPAN)
        if nb > 1:
            return r, qsl, pl.ds(pl.multiple_of(jnp.maximum(qi - 1, 0) * ATT_SPAN, ATT_SPAN), nk), jnp.minimum(qi, 1)
        return r, qsl, qsl, 0

    def scores(t, carry):
        r, qsl, ksl, var = slices(t)
        kk = ks_ref[r, ksl, :]
        for h in range(2):
            s = _dot_nt(qm_ref[h, r, qsl, :], kk) + bias_ref[h, var]
            s_ref[t, h] = s
            m_ref[t, h] = jnp.max(s, axis=-1, keepdims=True)
        return carry

    def outputs(t, carry):
        r, qsl, ksl, _ = slices(t)
        vv = vs_ref[r, ksl, :]
        o_acc = jnp.zeros((ATT_SPAN, LANE), F32)
        l_acc = jnp.zeros((ATT_SPAN, LANE), F32)
        for h in range(2):
            m = m_ref[t, h]
            p = jnp.exp(s_ref[t, h] - m).astype(BF16)
            od = _dot(p, vv)
            den = od[:, LANE:]
            o_acc = jnp.where(hm[h], od[:, :LANE] / den, o_acc)
            l_acc = jnp.where(hm[h], m + jnp.log(den), l_acc)
        os_ref[r, qsl, :] = o_acc
        ls_ref[r, qsl, :] = l_acc
        return carry

    lax.fori_loop(0, dil * nb, scores, 0, unroll=ATT_UNROLL)
    lax.fori_loop(0, dil * nb, outputs, 0, unroll=ATT_UNROLL)
    for res_ref, out_ref in ((os_ref, o_ref), (ls_ref, l_ref)):
        if dil == 1:
            out_ref[...] = res_ref[0]
        elif via_swap:
            out_ref[...] = jnp.swapaxes(res_ref[...], 0, 1).reshape(t_len, LANE)
        else:
            for r in range(dil):
                out_ref[pl.ds(r, n, stride=dil), :] = res_ref[r]


def _attn_prompt(ua3, gi):
    bsz, t_len, _ = ua3.shape
    win, dil = ATT_GROUPS[gi]
    keep = min(win, t_len)
    kern = functools.partial(_attn_prompt_kernel, t_len=t_len, keep=keep, dil=dil,
                             slopes=[float(s) for s in _SLOPES[gi * ATT_HG:(gi + 1) * ATT_HG]])
    uspec = lambda off: pl.BlockSpec((None, t_len, LANE), lambda b, hp: (b, 0, (off + gi * ATT_GW) // LANE + hp))
    ospec = lambda rows: pl.BlockSpec((None, rows, LANE), lambda b, hp: (b, 0, hp))
    n = t_len // dil
    nk = ATT_SPAN if n == ATT_SPAN else 2 * ATT_SPAN
    nblk = t_len // ATT_SPAN
    scratch = [pltpu.VMEM((2, dil, n, LANE), BF16), pltpu.VMEM((dil, n, LANE), BF16),
               pltpu.VMEM((dil, n, 2 * LANE), BF16), pltpu.VMEM((dil, n, LANE), F32),
               pltpu.VMEM((dil, n, LANE), F32), pltpu.VMEM((2, 2, ATT_SPAN, nk), F32),
               pltpu.VMEM((nblk, 2, ATT_SPAN, nk), F32), pltpu.VMEM((nblk, 2, ATT_SPAN, 1), F32)]
    o, l, kc, vc = pl.pallas_call(
        kern,
        grid=(bsz, ATT_GW // LANE),
        in_specs=[uspec(UA_Q), uspec(UA_K), uspec(UA_V)],
        out_specs=[ospec(t_len), ospec(t_len), ospec(keep), ospec(keep)],
        out_shape=[
            jax.ShapeDtypeStruct((bsz, t_len, ATT_GW), F32),
            jax.ShapeDtypeStruct((bsz, t_len, ATT_GW), F32),
            jax.ShapeDtypeStruct((bsz, keep, ATT_GW), F32),
            jax.ShapeDtypeStruct((bsz, keep, ATT_GW), F32),
        ],
        scratch_shapes=scratch,
        compiler_params=_params(("parallel", "parallel")),
        name=f"attn_prompt_g{gi}",
    )(ua3, ua3, ua3)
    m = bsz * t_len
    return (o.reshape(m, ATT_GW), l.reshape(m, ATT_GW),
            kc.reshape(bsz, keep, ATT_HG, ATT_E), vc.reshape(bsz, keep, ATT_HG, ATT_E))


_KX_PAD = LANE


def _head_masks(rows):
    lane = lax.broadcasted_iota(jnp.int32, (rows, ATT_GW), 1)
    return [(lane >= h * ATT_E) & (lane < (h + 1) * ATT_E) for h in range(ATT_HG)]


def _attn_sample_kernel(*refs, t_len, first, layer_slopes):
    ng = len(ATT_GROUPS)
    qkv = refs[0:3 * ng]
    caches = refs[3 * ng:5 * ng]
    outs = refs[5 * ng:] if first else refs[7 * ng:]
    o_refs, l_refs, cache_out = outs[0:ng], outs[ng:2 * ng], outs[2 * ng:4 * ng]
    kx_ref, vx_ref = outs[4 * ng:]

    def shift_in(c_ref, new_t):
        ext = jnp.concatenate([c_ref[...], new_t], axis=1)
        return ext, pltpu.roll(ext, ext.shape[1] - t_len, axis=1)

    def attend():
        hm = _head_masks(t_len)
        rows = ATT_HG * t_len
        zrows = jnp.zeros((_KX_PAD - t_len, ATT_GW), F32)
        for gi, (win, dil) in enumerate(ATT_GROUPS):
            q_ref, kn_ref, vn_ref = qkv[3 * gi:3 * gi + 3]
            wb = caches[2 * gi].shape[1]
            nk = wb + _KX_PAD
            for c_ref, n_ref, x_ref, co_ref in ((caches[2 * gi], kn_ref, kx_ref, cache_out[2 * gi]),
                                                (caches[2 * gi + 1], vn_ref, vx_ref, cache_out[2 * gi + 1])):
                new_t = jnp.concatenate([n_ref[...], zrows], axis=0).T
                ext, moved = shift_in(c_ref, new_t)
                x_ref[:, 0:nk] = ext.astype(BF16)
                co_ref[...] = moved[:, 0:wb] if first else moved[:, wb - LANE:wb]
            q = q_ref[...] * (ATT_E ** -0.5)
            qst = jnp.concatenate([jnp.where(hm[h], q, 0.0) for h in range(ATT_HG)], axis=0).astype(BF16)
            s = _dot(qst, kx_ref[:, 0:nk])
            ri = lax.broadcasted_iota(jnp.int32, (rows, nk), 0)
            ci = lax.broadcasted_iota(jnp.int32, (rows, nk), 1)
            dist = wb + (ri % t_len) - ci
            valid = (dist >= 0) & (dist <= win) & ((dist & (dil - 1)) == 0)
            hrow = lax.broadcasted_iota(jnp.int32, (rows, 1), 0) // t_len
            slope = jnp.zeros((rows, 1), F32)
            for h in range(ATT_HG):
                slope = jnp.where(hrow == h, float(layer_slopes[gi * ATT_HG + h]), slope)
            s = jnp.where(valid, s - slope * dist.astype(F32), NEG_INF)
            m = jnp.max(s, axis=-1, keepdims=True)
            p = jnp.exp(s - m)
            den = jnp.sum(p, axis=-1, keepdims=True)
            ost = _dot_nt(p.astype(BF16), vx_ref[:, 0:nk]) / den
            lst = m + jnp.log(den)
            o = jnp.zeros((t_len, ATT_GW), F32)
            l = jnp.zeros((t_len, ATT_GW), F32)
            for h in range(ATT_HG):
                o = jnp.where(hm[h], ost[h * t_len:(h + 1) * t_len], o)
                l = jnp.where(hm[h], lst[h * t_len:(h + 1) * t_len], l)
            o_refs[gi][...] = o
            l_refs[gi][...] = l

    if not first:
        attend()
        return
    pl.when(pl.program_id(1) == 0)(attend)

    @pl.when(pl.program_id(1) > 0)
    def _():
        for c_ref, co_ref in zip(caches, cache_out):
            _, moved = shift_in(c_ref, jnp.zeros((ATT_GW, _KX_PAD), F32))
            co_ref[...] = moved[:, 0:c_ref.shape[1]]


def _attn_sample(ua3, caches_t, layer, cache_prev):
    bsz, t_len, _ = ua3.shape
    ng = len(ATT_GROUPS)
    first = cache_prev is None
    kern = functools.partial(_attn_sample_kernel, t_len=t_len, first=first,
                             layer_slopes=[float(s) for s in _SLOPES])
    in_specs, args = [], []
    for gi in range(ng):
        for off in (UA_Q, UA_K, UA_V):
            in_specs.append(pl.BlockSpec((None, t_len, ATT_GW), lambda b, *_, o=off // ATT_GW + gi: (b, 0, o)))
            args.append(ua3)
    small = pl.BlockSpec((None, t_len, ATT_GW), lambda b, *_: (b, 0, 0))
    out_specs = [small] * (2 * ng)
    out_shape = [jax.ShapeDtypeStruct((bsz, t_len, ATT_GW), F32)] * (2 * ng)
    wbs = [c.shape[3] for c in caches_t]
    for c, wb in zip(caches_t, wbs):
        whole = (None, None, ATT_GW, wb)
        if first:
            in_specs.append(pl.BlockSpec(whole, lambda b, l: (l, b, 0, 0)))
            out_specs.append(pl.BlockSpec(whole, lambda b, l: (l, b, 0, 0)))
        else:
            in_specs.append(pl.BlockSpec(whole, lambda b: (layer, b, 0, 0)))
            out_specs.append(pl.BlockSpec((None, None, ATT_GW, LANE), lambda b, t=wb // LANE - 1: (layer, b, 0, t)))
        args.append(c)
        out_shape.append(jax.ShapeDtypeStruct(c.shape, F32))
    aliases = {}
    if not first:
        for i, c in enumerate(cache_prev):
            in_specs.append(pl.BlockSpec(memory_space=pl.ANY))
            args.append(c)
            aliases[5 * ng + i] = 2 * ng + i
    res = pl.pallas_call(
        kern,
        grid=(bsz, DEPTH) if first else (bsz,),
        in_specs=in_specs,
        out_specs=out_specs,
        out_shape=out_shape,
        scratch_shapes=[pltpu.VMEM((ATT_GW, max(wbs) + _KX_PAD), BF16)] * 2,
        input_output_aliases=aliases,
        compiler_params=_params(("parallel", "arbitrary") if first else ("parallel",)),
        name="attn_sample",
    )(*args)
    m = bsz * t_len
    o = [r.reshape(m, ATT_GW) for r in res[0:ng]]
    l = [r.reshape(m, ATT_GW) for r in res[ng:2 * ng]]
    return o, l, list(res[2 * ng:])


def _merge_kernel(x_ref, og_ref, p_ref, o0_ref, o1_ref, o2_ref, l0_ref, l1_ref, l2_ref, g_ref, wg_ref,
                  woa_ref, wob_ref, woc_ref, wout_ref, h_ref):
    x = x_ref[...]
    xn = _rms(x, g_ref[...]).astype(BF16)

    def gate(i):
        return _sigmoid(_dot(xn, wg_ref[:, i * D_MODEL:(i + 1) * D_MODEL]))

    merged = gate(0) * _dot(og_ref[...].astype(BF16), woa_ref[...])
    merged = merged + gate(1) * _dot(p_ref[...].astype(BF16), wob_ref[...])
    ls = [l0_ref[...], l1_ref[...], l2_ref[...]]
    m = jnp.maximum(jnp.maximum(ls[0], ls[1]), ls[2])
    es = [jnp.exp(l - m) for l in ls]
    inv = 1.0 / (es[0] + es[1] + es[2])
    yc = jnp.zeros_like(x)
    for gi, o_ref in enumerate((o0_ref, o1_ref, o2_ref)):
        oc = (o_ref[...] * (es[gi] * inv)).astype(BF16)
        yc = yc + _dot(oc, woc_ref[gi * ATT_GW:(gi + 1) * ATT_GW, :])
    merged = merged + gate(2) * yc
    h_ref[...] = x + _dot(merged.astype(BF16), wout_ref[...])


def _merge(x2, og, p, o, l, g, wg, woa, wob, woc, wout, tm):
    m = x2.shape[0]
    row = lambda w: pl.BlockSpec((tm, w), lambda i: (i, 0))
    return pl.pallas_call(
        _merge_kernel,
        grid=(m // tm,),
        in_specs=[row(D_MODEL), row(GLA_VW), row(D_MODEL)] + [row(ATT_GW)] * 6
                 + [_resident(a.shape) for a in (g, wg, woa, wob, woc, wout)],
        out_specs=row(D_MODEL),
        out_shape=jax.ShapeDtypeStruct((m, D_MODEL), F32),
        compiler_params=_params(("parallel",)),
        name="merge",
    )(x2, og, p, *o, *l, g, wg, woa, wob, woc, wout)


FFN_TF = 256
FFN_HALO = 16


def _ffn_kernel(*refs, t_len, tm, use_halo, has_prev, final):
    it = iter(refs)
    h_ref = next(it)
    halo_ref = next(it) if use_halo else None
    g_ref, wup_ref, cw_ref, cb_ref, wd_ref, fg_ref = (next(it) for _ in range(6))
    c1_ref, c2_ref = (next(it), next(it)) if has_prev else (None, None)
    y_ref, cv_ref = next(it), next(it)
    hn_ref, hh_ref = next(it), next(it)
    a_scr = None if use_halo else next(it)
    ho = FFN_HALO if use_halo else 0
    if use_halo:
        hn_ref[0:ho, :] = _rms(halo_ref[...], g_ref[...]).astype(BF16)
    hn_ref[ho:ho + tm, :] = _rms(h_ref[...], g_ref[...]).astype(BF16)
    t_loc = (lax.broadcasted_iota(jnp.int32, (tm, 1), 0) + pl.program_id(0) * tm) % t_len
    for jt in range(D_FF // FFN_TF):
        cols = slice(jt * FFN_TF, (jt + 1) * FFN_TF)
        a_ext = _dot(hn_ref[...], wup_ref[:, cols])
        a = a_ext[ho:]
        bg = _dot(hn_ref[ho:ho + tm, :], wup_ref[:, D_FF + jt * FFN_TF:D_FF + (jt + 1) * FFN_TF])
        if use_halo:
            a1, a2 = a_ext[ho - 1:ho - 1 + tm], a_ext[ho - 2:ho - 2 + tm]
        else:
            a1, a2 = pltpu.roll(a, 1, axis=0), pltpu.roll(a, 2, axis=0)
        a1 = jnp.where(t_loc >= 1, a1, c1_ref[:, cols] if has_prev else 0.0)
        a2 = jnp.where(t_loc >= 2, a2, c2_ref[:, cols] if has_prev else 0.0)
        y = cb_ref[:, cols] + cw_ref[0:1, cols] * a2 + cw_ref[1:2, cols] * a1 + cw_ref[2:3, cols] * a
        hh_ref[:, cols] = ((y * _sigmoid(y)) * bg).astype(BF16)
        if use_halo:
            cv_ref[:, cols] = a[tm - 2:tm]
        else:
            nseq = tm // t_len
            for kk in range(FFN_TF // LANE):
                lc = slice(kk * LANE, (kk + 1) * LANE)
                oc = slice(jt * FFN_TF + kk * LANE, jt * FFN_TF + (kk + 1) * LANE)
                a_scr[kk] = a[:, lc]
                cv_ref[0, :, oc] = a_scr[kk, pl.ds(t_len - 2, nseq, stride=t_len), :]
                cv_ref[1, :, oc] = a_scr[kk, pl.ds(t_len - 1, nseq, stride=t_len), :]
    out = h_ref[...] + _dot(hh_ref[...], wd_ref[...])
    if final:
        out = _rms(out, fg_ref[...])
    y_ref[...] = out


def _ffn(h2, g, wup, cw, cb, wd, fg, prev, t_len, tm, final):
    m = h2.shape[0]
    bsz = m // t_len
    use_halo = tm % t_len != 0
    has_prev = prev is not None
    kern = functools.partial(_ffn_kernel, t_len=t_len, tm=tm, use_halo=use_halo, has_prev=has_prev, final=final)
    in_specs = [pl.BlockSpec((tm, D_MODEL), lambda i: (i, 0))]
    args = [h2]
    if use_halo:
        hb = tm // FFN_HALO
        in_specs.append(pl.BlockSpec((FFN_HALO, D_MODEL), lambda i: (jnp.maximum(i * hb - 1, 0), 0)))
        args.append(h2)
    in_specs += [_resident(a.shape) for a in (g, wup, cw, cb, wd, fg)]
    args += [g, wup, cw, cb, wd, fg]
    if has_prev:
        z = jnp.zeros((bsz, t_len - 2, D_FF), F32)
        c1 = jnp.concatenate([prev[:, 1:2], jnp.zeros((bsz, 1, D_FF), F32), z], axis=1).reshape(m, D_FF)
        c2 = jnp.concatenate([prev, z], axis=1).reshape(m, D_FF)
        in_specs += [pl.BlockSpec((tm, D_FF), lambda i: (i, 0))] * 2
        args += [c1, c2]
    scratch = [pltpu.VMEM(((FFN_HALO if use_halo else 0) + tm, D_MODEL), BF16), pltpu.VMEM((tm, D_FF), BF16)]
    if use_halo:
        cv_spec = pl.BlockSpec((None, 2, D_FF), lambda i: (i, 0, 0))
        cv_shape = jax.ShapeDtypeStruct((m // tm, 2, D_FF), F32)
    else:
        nseq = tm // t_len
        cv_spec = pl.BlockSpec((2, nseq, D_FF), lambda i: (0, i, 0))
        cv_shape = jax.ShapeDtypeStruct((2, bsz, D_FF), F32)
        scratch.append(pltpu.VMEM((FFN_TF // LANE, tm, LANE), F32))
    y, cv = pl.pallas_call(
        kern,
        grid=(m // tm,),
        in_specs=in_specs,
        out_specs=[pl.BlockSpec((tm, D_MODEL), lambda i: (i, 0)), cv_spec],
        out_shape=[jax.ShapeDtypeStruct((m, D_MODEL), F32), cv_shape],
        scratch_shapes=scratch,
        compiler_params=_params(("parallel",)),
        name="ffn",
    )(*args)
    if use_halo:
        per = t_len // tm
        cv = cv[per - 1::per]
    else:
        cv = jnp.swapaxes(cv, 0, 1)
    return y, cv


def _prep_weights(w_in, gla_wa2, pool_w, w_oa, w_ob, w_oc, w_out, ffn_w_up, ffn_w_down):
    o = _IN_OFF
    cast = lambda a: a.astype(BF16)
    w1 = cast(w_in[:, :, o[0]:o[4]])
    wal = cast(jnp.pad(w_in[:, :, o[4]:o[5]], ((0, 0), (0, 0), (0, LANE - GLA_GATE_RANK))))
    w2 = cast(w_in[:, :, o[5]:o[9]])
    wg = cast(w_in[:, :, o[9]:o[12]])
    wa2p = cast(jnp.pad(gla_wa2, ((0, 0), (0, LANE - GLA_GATE_RANK), (0, 0))))
    return (w1, wal, w2, wg, wa2p, cast(pool_w), cast(w_oa), cast(w_ob), cast(w_oc), cast(w_out),
            cast(ffn_w_up), cast(ffn_w_down))


def _cache_channel_major(c):
    d, b, w = c.shape[:3]
    return jnp.transpose(c, (0, 1, 3, 4, 2)).reshape(d, b, ATT_GW, w)


def _cache_token_major(ct):
    d, b, _, w = ct.shape
    return jnp.transpose(ct.reshape(d, b, ATT_HG, ATT_E, w), (0, 1, 4, 2, 3))


def _run_group(x, states, weights, small, tiles):
    bsz, t_len, _ = x.shape
    m = bsz * t_len
    prompt = states is None
    act = _act_dtype(t_len)
    w1, wal, w2, wg, wa2p, pw, woa, wob, woc, wout, wup, wdn = weights
    norm1_g, norm2_g, gla_ba, gla_norm_g, pool_scale, conv_w, conv_b, final_g = small
    x2 = x.reshape(m, D_MODEL)
    new_gla, new_pool, new_kv, new_conv = [], [], [], []
    kv_t = None if prompt else [_cache_channel_major(c) for c in states["kv"]]
    kv_run = None
    for l in range(DEPTH):
        zg, al, ua = _in_proj(x2, norm1_g[l][None], w1[l], wal[l], w2[l], tiles["in_tm"], act)
        zg3 = zg.reshape(bsz, t_len, ZG_W)
        al3 = al.reshape(bsz, t_len, LANE)
        ua3 = ua.reshape(bsz, t_len, UA_W)
        if prompt:
            s0 = jnp.zeros((bsz, GLA_HEADS, GLA_DK, GLA_DV), F32)
            prev = jnp.zeros((bsz, POOL_HALO, D_MODEL), F32)
            n_prev = 0
        else:
            s0 = states["gla"][l]
            prev = jnp.pad(states["pool"][l], ((0, 0), (POOL_HALO - POOL_STATE, 0), (0, 0)))
            n_prev = POOL_STATE
        og, g_new = _gla(zg3, al3, wa2p[l], gla_ba[l][None], gla_norm_g[l][None], s0, t_len)
        pb, p_new = _pool(ua3, prev, pw[l], pool_scale[l][None], t_len, n_prev)
        if prompt:
            o, lse, kv = [], [], []
            for gi in range(len(ATT_GROUPS)):
                og_i, l_i, kc, vc = _attn_prompt(ua3, gi)
                o.append(og_i)
                lse.append(l_i)
                kv += [kc, vc]
        else:
            o, lse, kv_run = _attn_sample(ua3, kv_t, l, kv_run)
            kv = None
        h2 = _merge(x2, og.reshape(m, GLA_VW), pb.reshape(m, D_MODEL), o, lse, norm1_g[l][None], wg[l],
                    woa[l], wob[l], woc[l], wout[l], tiles["merge_tm"])
        x2, c_new = _ffn(h2, norm2_g[l][None], wup[l], conv_w[l], conv_b[l][None], wdn[l], final_g[None],
                         None if prompt else states["conv"][l], t_len, tiles["ffn_tm"], l == DEPTH - 1)
        new_gla.append(g_new)
        new_pool.append(p_new)
        new_kv.append(kv)
        new_conv.append(c_new)
    if prompt:
        kv_out = [jnp.stack([kvl[i] for kvl in new_kv], axis=0) for i in range(2 * len(ATT_GROUPS))]
    else:
        kv_out = [_cache_token_major(c) for c in kv_run]
    return (x2.reshape(bsz, t_len, D_MODEL), jnp.stack(new_gla, 0), jnp.stack(new_pool, 0), kv_out,
            jnp.stack(new_conv, 0))


def kernel(x_prompt, x_sample, state_gla, state_pool, cache_k_w128, cache_v_w128, cache_k_w512, cache_v_w512,
           cache_k_w2048, cache_v_w2048, state_ffn_conv, norm1_g, norm2_g, w_in, gla_wa2, gla_ba, gla_norm_g,
           pool_w, pool_scale, w_oa, w_ob, w_oc, w_out, ffn_w_up, ffn_conv_w, ffn_conv_b, ffn_w_down,
           final_norm_g):
    weights = _prep_weights(w_in, gla_wa2, pool_w, w_oa, w_ob, w_oc, w_out, ffn_w_up, ffn_w_down)
    small = (norm1_g, norm2_g, gla_ba, gla_norm_g, pool_scale, ffn_conv_w, ffn_conv_b, final_norm_g)
    y_p, gla_p, pool_p, kv_p, conv_p = _run_group(
        x_prompt, None, weights, small, dict(in_tm=1024, merge_tm=512, ffn_tm=1024))
    states = dict(gla=state_gla, pool=state_pool, conv=state_ffn_conv,
                  kv=[cache_k_w128, cache_v_w128, cache_k_w512, cache_v_w512, cache_k_w2048, cache_v_w2048])
    m_s = x_sample.shape[0] * x_sample.shape[1]
    y_s, gla_s, pool_s, kv_s, conv_s = _run_group(
        x_sample, states, weights, small, dict(in_tm=m_s, merge_tm=m_s, ffn_tm=m_s))
    k128_p, v128_p, k512_p, v512_p, k2048_p, v2048_p = kv_p
    k128_s, v128_s, k512_s, v512_s, k2048_s, v2048_s = kv_s
    return (y_p, y_s, gla_p, gla_s, pool_p, pool_s, k128_p, k128_s, v128_p, v128_s,
            k512_p, k512_s, v512_p, v512_s, k2048_p, k2048_s, v2048_p, v2048_s, conv_p, conv_s)
```

```python
import functools

import jax
import jax.numpy as jnp
import numpy as np
from jax import lax
from jax.experimental import pallas as pl
from jax.experimental.pallas import tpu as pltpu

F32 = jnp.float32
BF16 = jnp.bfloat16

D_MODEL = 1024
DEPTH = 2
GLA_HEADS = 4
GLA_DK = 128
GLA_DV = 256
GLA_QK = GLA_HEADS * GLA_DK
GLA_VW = GLA_HEADS * GLA_DV
GLA_GATE_RANK = 16
GLA_GATE_NORM = 16.0
GLA_CHUNK = 64
GLA_SCAN_ROWS = 256
GLA_UNROLL = 8
POOL_WINDOWS = (2, 4, 8, 16)
POOL_GROUP = 256
POOL_STATE = 15
POOL_HALO = 16
ATT_GROUPS = ((128, 1), (512, 4), (2048, 16))
ATT_HG = 4
ATT_E = 64
ATT_GW = ATT_HG * ATT_E
ATT_SPAN = 128
ATT_UNROLL = 4
ATT_HEADS = 12
ATT_WIDTH = ATT_HEADS * ATT_E
D_FF = 2816
NORM_EPS = 1e-6
NEG_INF = -1e30

LANE = 128
SUBLANE = 8
BF16_SUBLANE = 16
VMEM_LIMIT = 56 * 1024 * 1024

_IN_SPLITS = (GLA_QK, GLA_QK, GLA_VW, GLA_VW, GLA_GATE_RANK, D_MODEL, ATT_WIDTH, ATT_WIDTH, ATT_WIDTH,
              D_MODEL, D_MODEL, D_MODEL)
_IN_OFF = [0] + [int(v) for v in np.cumsum(_IN_SPLITS)]
ZG_W = 2 * GLA_QK + 2 * GLA_VW
ZG_K, ZG_V, ZG_R = GLA_QK, 2 * GLA_QK, 2 * GLA_QK + GLA_VW
A_W = 3 * ATT_WIDTH
A_Q, A_K, A_V = 0, ATT_WIDTH, 2 * ATT_WIDTH
ZG_TN, U_TN, A_TN = 1024, 512, 768

_SLOPES = (2.0 ** (-8.0 * np.arange(1, ATT_HEADS + 1) / ATT_HEADS)).astype(np.float32)


def _params(sem):
    return pltpu.CompilerParams(dimension_semantics=sem, vmem_limit_bytes=VMEM_LIMIT)


def _resident(shape):
    return pl.BlockSpec(shape, lambda *_: (0,) * len(shape), pipeline_mode=pl.Buffered(1))


def _act_dtype(t_len):
    return BF16 if t_len % BF16_SUBLANE == 0 else F32


def _dot(a, b):
    return jnp.dot(a, b, preferred_element_type=F32)


def _dot_nt(a, b):
    return lax.dot_general(a, b, (((1,), (1,)), ((), ())), preferred_element_type=F32)


def _dot_tn(a, b):
    return lax.dot_general(a, b, (((0,), (0,)), ((), ())), preferred_element_type=F32)


def _rms(x, g):
    return x * lax.rsqrt(jnp.mean(x * x, axis=-1, keepdims=True) + NORM_EPS) * g


def _sigmoid(x):
    return 1.0 / (1.0 + jnp.exp(-x))


def _in_proj_kernel(x_ref, g_ref, wal_ref, *refs, bounds):
    nseg = len(bounds) - 1
    w_refs, al_ref, out_refs, xn_ref = refs[:nseg], refs[nseg], refs[nseg + 1:2 * nseg + 1], refs[-1]
    j = pl.program_id(1)

    @pl.when(j == 0)
    def _():
        xn = _rms(x_ref[...], g_ref[...]).astype(BF16)
        xn_ref[...] = xn
        al_ref[...] = _dot(xn, wal_ref[...]).astype(al_ref.dtype)

    for s in range(nseg):
        @pl.when((j >= bounds[s]) & (j < bounds[s + 1]))
        def _(s=s):
            out_refs[s][...] = _dot(xn_ref[...], w_refs[s][...]).astype(out_refs[s].dtype)


def _in_proj(x2, g, wal, segs, tm, act):
    m = x2.shape[0]
    counts = [w.shape[1] // tn for w, tn, _ in segs]
    bounds = [0] + [int(v) for v in np.cumsum(counts)]
    col = lambda s: (lambda j: jnp.clip(j - bounds[s], 0, counts[s] - 1))
    in_specs = [
        pl.BlockSpec((tm, D_MODEL), lambda i, j: (i, 0)),
        pl.BlockSpec((1, D_MODEL), lambda i, j: (0, 0)),
        pl.BlockSpec((D_MODEL, LANE), lambda i, j: (0, 0)),
    ]
    out_specs = [pl.BlockSpec((tm, LANE), lambda i, j: (i, 0))]
    out_shape = [jax.ShapeDtypeStruct((m, LANE), act)]
    for s, (w, tn, dt) in enumerate(segs):
        in_specs.append(pl.BlockSpec((D_MODEL, tn), lambda i, j, c=col(s): (0, c(j))))
        out_specs.append(pl.BlockSpec((tm, tn), lambda i, j, c=col(s): (i, c(j))))
        out_shape.append(jax.ShapeDtypeStruct((m, w.shape[1]), dt))
    return pl.pallas_call(
        functools.partial(_in_proj_kernel, bounds=tuple(bounds)),
        grid=(m // tm, bounds[-1]),
        in_specs=in_specs,
        out_specs=out_specs,
        out_shape=out_shape,
        scratch_shapes=[pltpu.VMEM((tm, D_MODEL), BF16)],
        compiler_params=_params(("parallel", "arbitrary")),
        name="in_proj",
    )(x2, g, wal, *[w for w, _, _ in segs])


def _gla_kernel(q_ref, k_ref, v_ref, r_ref, a_ref, wa_ref, ba_ref, gn_ref, s0_ref, o_ref, so_ref, *scratch,
                c, nc, hps):
    for hh in range(hps):
        dk = slice(hh * GLA_DK, (hh + 1) * GLA_DK)
        dv = slice(hh * GLA_DV, (hh + 1) * GLA_DV)
        _gla_head(q_ref.at[:, dk], k_ref.at[:, dk], v_ref.at[:, dv], r_ref.at[:, dv], a_ref, wa_ref.at[:, dk],
                  ba_ref.at[:, dk], gn_ref, s0_ref.at[hh], o_ref.at[:, dv], so_ref.at[hh], *scratch, c=c, nc=nc)


def _gla_head(q_ref, k_ref, v_ref, r_ref, a_ref, wa_ref, ba_ref, gn_ref, s0_ref, o_ref, so_ref,
              qe_ref, ke_ref, qi_ref, ks_ref, vb_ref, att_ref, dec_ref, oacc_ref, kv_ref, st_ref, *, c, nc):
    ce = max(c, BF16_SUBLANE)
    tp = nc * ce
    mid = (c - 1) // 2

    def load(ref):
        x = ref[...].astype(F32)
        if ce != c:
            x = jnp.concatenate([x, jnp.zeros((ce - c, x.shape[1]), F32)], axis=0)
        return x

    z = _dot(load(a_ref).astype(BF16), wa_ref[...]) + ba_ref[...]
    la = (jnp.minimum(z, 0.0) - jnp.log(1.0 + jnp.exp(-jnp.abs(z)))) * (1.0 / GLA_GATE_NORM)
    if ce != c:
        la = jnp.where(lax.broadcasted_iota(jnp.int32, (tp, GLA_DK), 0) < c, la, 0.0)
    slab = min(tp, GLA_SCAN_ROWS)
    ri = lax.broadcasted_iota(jnp.int32, (slab, slab), 0)
    ci = lax.broadcasted_iota(jnp.int32, (slab, slab), 1)
    tri = jnp.where((ri >= ci) & ((ri & -ce) == (ci & -ce)), 1.0, 0.0).astype(BF16)
    la_hi = la.astype(BF16)
    la_lo = (la - la_hi.astype(F32)).astype(BF16)
    b = jnp.concatenate(
        [_dot(tri, la_hi[i * slab:(i + 1) * slab]) + _dot(tri, la_lo[i * slab:(i + 1) * slab])
         for i in range(tp // slab)], axis=0)
    b3 = b.reshape(nc, ce, GLA_DK)
    b_mid = b3[:, mid:mid + 1, :]
    b_last = b3[:, c - 1:c, :]
    q3 = (load(q_ref) * (GLA_DK ** -0.5)).reshape(nc, ce, GLA_DK)
    k3 = load(k_ref).reshape(nc, ce, GLA_DK)
    flat = lambda x: x.reshape(tp, GLA_DK).astype(BF16)
    qe_ref[...] = flat(q3 * jnp.exp(b3 - b_mid))
    ke_ref[...] = flat(k3 * jnp.exp(b_mid - b3))
    qi_ref[...] = flat(q3 * jnp.exp(b3))
    ks_ref[...] = flat(k3 * jnp.exp(b_last - b3))
    dec_ref[...] = jnp.exp(b_last)
    vb_ref[...] = load(v_ref).astype(BF16)
    causal = (lax.broadcasted_iota(jnp.int32, (ce, ce), 0) >= lax.broadcasted_iota(jnp.int32, (ce, ce), 1))
    rows = lambda n: pl.ds(pl.multiple_of(n * ce, ce), ce)

    def scores(n, carry):
        sl = rows(n)
        att_ref[sl, :] = jnp.where(causal, _dot_nt(qe_ref[sl, :], ke_ref[sl, :]), 0.0).astype(BF16)
        return carry

    def within(n, carry):
        sl = rows(n)
        oacc_ref[sl, :] = _dot(att_ref[sl, :], vb_ref[sl, :])
        return carry

    def increments(n, carry):
        sl = rows(n)
        kv_ref[n] = _dot_tn(vb_ref[sl, :], ks_ref[sl, :])
        return carry

    def across(n, carry):
        sl = rows(n)
        st = st_ref[...]
        oacc_ref[sl, :] += _dot_nt(qi_ref[sl, :], st.astype(BF16))
        st_ref[...] = st * dec_ref[n] + kv_ref[n]
        return carry

    unroll = min(nc, GLA_UNROLL)
    lax.fori_loop(0, nc, scores, 0, unroll=unroll)
    lax.fori_loop(0, nc, within, 0, unroll=unroll)
    lax.fori_loop(0, nc, increments, 0, unroll=unroll)
    st_ref[...] = s0_ref[...].T
    lax.fori_loop(0, nc, across, 0, unroll=unroll)
    r = load(r_ref)
    res = _rms(oacc_ref[...], gn_ref[...]) * (r * _sigmoid(r))
    o_ref[...] = res[:nc * c if ce == c else c].astype(o_ref.dtype)
    so_ref[...] = st_ref[...].T


def _gla(zg3, al3, wa2p, ba, gn, s0, layer, t_len, hps):
    bsz = zg3.shape[0]
    c = GLA_CHUNK if t_len % GLA_CHUNK == 0 else t_len
    nc = t_len // c
    ce = max(c, BF16_SUBLANE)
    assert ce == c or nc == 1
    tp = nc * ce
    kern = functools.partial(_gla_kernel, c=c, nc=nc, hps=hps)
    zspec = lambda w, off: pl.BlockSpec((None, t_len, hps * w), lambda b, h: (b, 0, off // (hps * w) + h))
    return pl.pallas_call(
        kern,
        grid=(bsz, GLA_HEADS // hps),
        in_specs=[
            zspec(GLA_DK, 0), zspec(GLA_DK, ZG_K), zspec(GLA_DV, ZG_V), zspec(GLA_DV, ZG_R),
            pl.BlockSpec((None, t_len, LANE), lambda b, h: (b, 0, 0)),
            pl.BlockSpec((LANE, hps * GLA_DK), lambda b, h: (0, h)),
            pl.BlockSpec((1, hps * GLA_DK), lambda b, h: (0, h)),
            pl.BlockSpec((1, GLA_DV), lambda b, h: (0, 0)),
            pl.BlockSpec((None, None, hps, GLA_DK, GLA_DV), lambda b, h: (layer, b, h, 0, 0)),
        ],
        out_specs=[
            pl.BlockSpec((None, t_len, hps * GLA_DV), lambda b, h: (b, 0, h)),
            pl.BlockSpec((None, hps, GLA_DK, GLA_DV), lambda b, h: (b, h, 0, 0)),
        ],
        out_shape=[
            jax.ShapeDtypeStruct((bsz, t_len, GLA_VW), zg3.dtype),
            jax.ShapeDtypeStruct((bsz, GLA_HEADS, GLA_DK, GLA_DV), F32),
        ],
        scratch_shapes=[pltpu.VMEM((tp, GLA_DK), BF16)] * 4 + [
            pltpu.VMEM((tp, GLA_DV), BF16),
            pltpu.VMEM((tp, ce), BF16),
            pltpu.VMEM((nc, 1, GLA_DK), F32),
            pltpu.VMEM((tp, GLA_DV), F32),
            pltpu.VMEM((nc, GLA_DV, GLA_DK), F32),
            pltpu.VMEM((GLA_DV, GLA_DK), F32),
        ],
        compiler_params=_params(("parallel", "arbitrary")),
        name="gla",
    )(zg3, zg3, zg3, zg3, al3, wa2p, ba, gn, s0)


def _pool_kernel(u_ref, prev_ref, pw_ref, ps_ref, p_ref, pn_ref, *, t_len, n_prev):
    ext = jnp.concatenate([prev_ref[...], u_ref[...]], axis=0)
    t_abs = lax.broadcasted_iota(jnp.int32, (t_len, 1), 0) + n_prev
    for gi, w in enumerate(POOL_WINDOWS):
        cols = slice(gi * POOL_GROUP, (gi + 1) * POOL_GROUP)
        x = ext[:, cols]
        acc = x
        s = 1
        while s < w:
            acc = acc + pltpu.roll(acc, s, axis=0)
            s *= 2
        cnt = jnp.minimum(t_abs + 1, w).astype(F32)
        pooled = acc[POOL_HALO:] / cnt - x[POOL_HALO:]
        mixed = _dot(pooled.astype(BF16), pw_ref[gi]) * ps_ref[:, cols]
        p_ref[:, cols] = mixed.astype(p_ref.dtype)
    keep = max(0, POOL_STATE - t_len)
    if keep:
        pn_ref[0:keep, :] = prev_ref[POOL_HALO - keep:POOL_HALO, :]
    pn_ref[keep:POOL_STATE, :] = u_ref[t_len - (POOL_STATE - keep):t_len, :]


def _pool(u3, prev, pw, ps, t_len, n_prev):
    bsz = u3.shape[0]
    kern = functools.partial(_pool_kernel, t_len=t_len, n_prev=n_prev)
    return pl.pallas_call(
        kern,
        grid=(bsz,),
        in_specs=[
            pl.BlockSpec((None, t_len, D_MODEL), lambda b: (b, 0, 0)),
            pl.BlockSpec((None, POOL_HALO, D_MODEL), lambda b: (b, 0, 0)),
            pl.BlockSpec((4, POOL_GROUP, POOL_GROUP), lambda b: (0, 0, 0)),
            pl.BlockSpec((1, D_MODEL), lambda b: (0, 0)),
        ],
        out_specs=[
            pl.BlockSpec((None, t_len, D_MODEL), lambda b: (b, 0, 0)),
            pl.BlockSpec((None, POOL_STATE, D_MODEL), lambda b: (b, 0, 0)),
        ],
        out_shape=[
            jax.ShapeDtypeStruct((bsz, t_len, D_MODEL), _act_dtype(t_len)),
            jax.ShapeDtypeStruct((bsz, POOL_STATE, D_MODEL), F32),
        ],
        compiler_params=_params(("parallel",)),
        name="pool",
    )(u3, prev, pw, ps)


def _attn_prompt_kernel(q_ref, k_ref, v_ref, o_ref, l_ref, kc_ref, vc_ref,
                        qm_ref, ks_ref, vs_ref, os_ref, ls_ref, bias_ref, s_ref, m_ref, stage_ref,
                        *, t_len, keep, dil, slopes):
    n = t_len // dil
    nb = n // ATT_SPAN
    hp = pl.program_id(1)
    kc_ref[...] = k_ref[t_len - keep:t_len, :].astype(F32)
    vc_ref[...] = v_ref[t_len - keep:t_len, :].astype(F32)
    via_swap = dil % SUBLANE == 0

    def to_res(ref):
        x = ref[...].astype(F32)
        if dil == 1:
            return x[None]
        if via_swap:
            return jnp.swapaxes(x.reshape(n, dil, LANE), 0, 1)
        stage_ref[...] = x
        return jnp.stack([stage_ref[pl.ds(r, n, stride=dil), :] for r in range(dil)], axis=0)

    q3 = to_res(q_ref) * (ATT_E ** -0.5)
    lane3 = lax.broadcasted_iota(jnp.int32, q3.shape, 2)
    qm_ref[0] = jnp.where(lane3 < ATT_E, q3, 0.0).astype(BF16)
    qm_ref[1] = jnp.where(lane3 >= ATT_E, q3, 0.0).astype(BF16)
    ks_ref[...] = to_res(k_ref).astype(BF16)
    vs_ref[:, :, 0:LANE] = to_res(v_ref).astype(BF16)
    vs_ref[:, :, LANE:2 * LANE] = jnp.ones((dil, n, LANE), BF16)

    nk = ATT_SPAN if nb == 1 else 2 * ATT_SPAN
    a_idx = lax.broadcasted_iota(jnp.int32, (ATT_SPAN, nk), 0)
    c_idx = lax.broadcasted_iota(jnp.int32, (ATT_SPAN, nk), 1)
    for var in range(1 if nb == 1 else 2):
        j = a_idx - c_idx + var * ATT_SPAN
        valid = (j >= 0) & (j <= ATT_SPAN)
        for h in range(2):
            scale = jnp.where(hp == 0, -slopes[h] * dil, -slopes[2 + h] * dil)
            bias_ref[h, var] = jnp.where(valid, j.astype(F32) * scale, NEG_INF)

    lane = lax.broadcasted_iota(jnp.int32, (ATT_SPAN, LANE), 1)
    hm = [lane < ATT_E, lane >= ATT_E]

    def slices(t):
        r, qi = t // nb, t % nb
        qsl = pl.ds(pl.multiple_of(qi * ATT_SPAN, ATT_SPAN), ATT_SPAN)
        if nb > 1:
            return r, qsl, pl.ds(pl.multiple_of(jnp.maximum(qi - 1, 0) * ATT_SPAN, ATT_SPAN), nk), jnp.minimum(qi, 1)
        return r, qsl, qsl, 0

    def scores(t, carry):
        r, qsl, ksl, var = slices(t)
        kk = ks_ref[r, ksl, :]
        for h in range(2):
            s = _dot_nt(qm_ref[h, r, qsl, :], kk) + bias_ref[h, var]
            s_ref[t, h] = s
            m_ref[t, h] = jnp.max(s, axis=-1, keepdims=True)
        return carry

    def outputs(t, carry):
        r, qsl, ksl, _ = slices(t)
        vv = vs_ref[r, ksl, :]
        o_acc = jnp.zeros((ATT_SPAN, LANE), F32)
        l_acc = jnp.zeros((ATT_SPAN, LANE), F32)
        for h in range(2):
            m = m_ref[t, h]
            p = jnp.exp(s_ref[t, h] - m).astype(BF16)
            od = _dot(p, vv)
            den = od[:, LANE:]
            o_acc = jnp.where(hm[h], od[:, :LANE] / den, o_acc)
            l_acc = jnp.where(hm[h], m + jnp.log(den), l_acc)
        os_ref[r, qsl, :] = o_acc
        ls_ref[r, qsl, :] = l_acc
        return carry

    lax.fori_loop(0, dil * nb, scores, 0, unroll=ATT_UNROLL)
    lax.fori_loop(0, dil * nb, outputs, 0, unroll=ATT_UNROLL)
    for res_ref, out_ref in ((os_ref, o_ref), (ls_ref, l_ref)):
        if dil == 1:
            out_ref[...] = res_ref[0]
        elif via_swap:
            out_ref[...] = jnp.swapaxes(res_ref[...], 0, 1).reshape(t_len, LANE)
        else:
            for r in range(dil):
                out_ref[pl.ds(r, n, stride=dil), :] = res_ref[r]


def _attn_prompt(a3, gi):
    bsz, t_len, _ = a3.shape
    win, dil = ATT_GROUPS[gi]
    keep = min(win, t_len)
    kern = functools.partial(_attn_prompt_kernel, t_len=t_len, keep=keep, dil=dil,
                             slopes=[float(s) for s in _SLOPES[gi * ATT_HG:(gi + 1) * ATT_HG]])
    uspec = lambda off: pl.BlockSpec((None, t_len, LANE), lambda b, hp: (b, 0, (off + gi * ATT_GW) // LANE + hp))
    ospec = lambda rows: pl.BlockSpec((None, rows, LANE), lambda b, hp: (b, 0, hp))
    n = t_len // dil
    nk = ATT_SPAN if n == ATT_SPAN else 2 * ATT_SPAN
    nblk = t_len // ATT_SPAN
    scratch = [pltpu.VMEM((2, dil, n, LANE), BF16), pltpu.VMEM((dil, n, LANE), BF16),
               pltpu.VMEM((dil, n, 2 * LANE), BF16), pltpu.VMEM((dil, n, LANE), F32),
               pltpu.VMEM((dil, n, LANE), F32), pltpu.VMEM((2, 2, ATT_SPAN, nk), F32),
               pltpu.VMEM((nblk, 2, ATT_SPAN, nk), F32), pltpu.VMEM((nblk, 2, ATT_SPAN, 1), F32),
               pltpu.VMEM((t_len, LANE), F32)]
    o, l, kc, vc = pl.pallas_call(
        kern,
        grid=(bsz, ATT_GW // LANE),
        in_specs=[uspec(A_Q), uspec(A_K), uspec(A_V)],
        out_specs=[ospec(t_len), ospec(t_len), ospec(keep), ospec(keep)],
        out_shape=[
            jax.ShapeDtypeStruct((bsz, t_len, ATT_GW), F32),
            jax.ShapeDtypeStruct((bsz, t_len, ATT_GW), F32),
            jax.ShapeDtypeStruct((bsz, keep, ATT_GW), F32),
            jax.ShapeDtypeStruct((bsz, keep, ATT_GW), F32),
        ],
        scratch_shapes=scratch,
        compiler_params=_params(("parallel", "parallel")),
        name=f"attn_prompt_g{gi}",
    )(a3, a3, a3)
    m = bsz * t_len
    return (o.reshape(m, ATT_GW), l.reshape(m, ATT_GW),
            kc.reshape(bsz, keep, ATT_HG, ATT_E), vc.reshape(bsz, keep, ATT_HG, ATT_E))


_KX_PAD = LANE


def _head_masks(rows):
    lane = lax.broadcasted_iota(jnp.int32, (rows, ATT_GW), 1)
    return [(lane >= h * ATT_E) & (lane < (h + 1) * ATT_E) for h in range(ATT_HG)]


def _attn_sample_kernel(*refs, t_len, first, layer_slopes):
    ng = len(ATT_GROUPS)
    qkv = refs[0:3 * ng]
    caches = refs[3 * ng:5 * ng]
    outs = refs[5 * ng:] if first else refs[7 * ng:]
    o_refs, l_refs, cache_out = outs[0:ng], outs[ng:2 * ng], outs[2 * ng:4 * ng]
    kx_ref, vx_ref = outs[4 * ng:]

    def shift_in(c_ref, new_t):
        ext = jnp.concatenate([c_ref[...], new_t], axis=1)
        return ext, pltpu.roll(ext, ext.shape[1] - t_len, axis=1)

    def attend():
        hm = _head_masks(t_len)
        rows = ATT_HG * t_len
        zrows = jnp.zeros((_KX_PAD - t_len, ATT_GW), F32)
        for gi, (win, dil) in enumerate(ATT_GROUPS):
            q_ref, kn_ref, vn_ref = qkv[3 * gi:3 * gi + 3]
            wb = caches[2 * gi].shape[1]
            nk = wb + _KX_PAD
            for c_ref, n_ref, x_ref, co_ref in ((caches[2 * gi], kn_ref, kx_ref, cache_out[2 * gi]),
                                                (caches[2 * gi + 1], vn_ref, vx_ref, cache_out[2 * gi + 1])):
                new_t = jnp.concatenate([n_ref[...], zrows], axis=0).T
                ext, moved = shift_in(c_ref, new_t)
                x_ref[:, 0:nk] = ext.astype(BF16)
                co_ref[...] = moved[:, 0:wb] if first else moved[:, wb - LANE:wb]
            q = q_ref[...] * (ATT_E ** -0.5)
            qst = jnp.concatenate([jnp.where(hm[h], q, 0.0) for h in range(ATT_HG)], axis=0).astype(BF16)
            s = _dot(qst, kx_ref[:, 0:nk])
            ri = lax.broadcasted_iota(jnp.int32, (rows, nk), 0)
            ci = lax.broadcasted_iota(jnp.int32, (rows, nk), 1)
            dist = wb + (ri % t_len) - ci
            valid = (dist >= 0) & (dist <= win) & ((dist & (dil - 1)) == 0)
            hrow = lax.broadcasted_iota(jnp.int32, (rows, 1), 0) // t_len
            slope = jnp.zeros((rows, 1), F32)
            for h in range(ATT_HG):
                slope = jnp.where(hrow == h, float(layer_slopes[gi * ATT_HG + h]), slope)
            s = jnp.where(valid, s - slope * dist.astype(F32), NEG_INF)
            m = jnp.max(s, axis=-1, keepdims=True)
            p = jnp.exp(s - m)
            den = jnp.sum(p, axis=-1, keepdims=True)
            ost = _dot_nt(p.astype(BF16), vx_ref[:, 0:nk]) / den
            lst = m + jnp.log(den)
            o = jnp.zeros((t_len, ATT_GW), F32)
            l = jnp.zeros((t_len, ATT_GW), F32)
            for h in range(ATT_HG):
                o = jnp.where(hm[h], ost[h * t_len:(h + 1) * t_len], o)
                l = jnp.where(hm[h], lst[h * t_len:(h + 1) * t_len], l)
            o_refs[gi][...] = o
            l_refs[gi][...] = l

    if not first:
        attend()
        return
    pl.when(pl.program_id(1) == 0)(attend)

    @pl.when(pl.program_id(1) > 0)
    def _():
        for c_ref, co_ref in zip(caches, cache_out):
            _, moved = shift_in(c_ref, jnp.zeros((ATT_GW, _KX_PAD), F32))
            co_ref[...] = moved[:, 0:c_ref.shape[1]]


def _attn_sample(a3, caches_t, layer, cache_prev):
    bsz, t_len, _ = a3.shape
    ng = len(ATT_GROUPS)
    first = cache_prev is None
    kern = functools.partial(_attn_sample_kernel, t_len=t_len, first=first,
                             layer_slopes=[float(s) for s in _SLOPES])
    in_specs, args = [], []
    for gi in range(ng):
        for off in (A_Q, A_K, A_V):
            in_specs.append(pl.BlockSpec((None, t_len, ATT_GW), lambda b, *_, o=off // ATT_GW + gi: (b, 0, o)))
            args.append(a3)
    small = pl.BlockSpec((None, t_len, ATT_GW), lambda b, *_: (b, 0, 0))
    out_specs = [small] * (2 * ng)
    out_shape = [jax.ShapeDtypeStruct((bsz, t_len, ATT_GW), F32)] * (2 * ng)
    wbs = [c.shape[3] for c in caches_t]
    for c, wb in zip(caches_t, wbs):
        whole = (None, None, ATT_GW, wb)
        if first:
            in_specs.append(pl.BlockSpec(whole, lambda b, l: (l, b, 0, 0)))
            out_specs.append(pl.BlockSpec(whole, lambda b, l: (l, b, 0, 0)))
        else:
            in_specs.append(pl.BlockSpec(whole, lambda b: (layer, b, 0, 0)))
            out_specs.append(pl.BlockSpec((None, None, ATT_GW, LANE), lambda b, t=wb // LANE - 1: (layer, b, 0, t)))
        args.append(c)
        out_shape.append(jax.ShapeDtypeStruct(c.shape, F32))
    aliases = {}
    if not first:
        for i, c in enumerate(cache_prev):
            in_specs.append(pl.BlockSpec(memory_space=pl.ANY))
            args.append(c)
            aliases[5 * ng + i] = 2 * ng + i
    res = pl.pallas_call(
        kern,
        grid=(bsz, DEPTH) if first else (bsz,),
        in_specs=in_specs,
        out_specs=out_specs,
        out_shape=out_shape,
        scratch_shapes=[pltpu.VMEM((ATT_GW, max(wbs) + _KX_PAD), BF16)] * 2,
        input_output_aliases=aliases,
        compiler_params=_params(("parallel", "arbitrary") if first else ("parallel",)),
        name="attn_sample",
    )(*args)
    m = bsz * t_len
    o = [r.reshape(m, ATT_GW) for r in res[0:ng]]
    l = [r.reshape(m, ATT_GW) for r in res[ng:2 * ng]]
    return o, l, list(res[2 * ng:])


def _merge_kernel(x_ref, og_ref, p_ref, o0_ref, o1_ref, o2_ref, l0_ref, l1_ref, l2_ref, g_ref, wg_ref,
                  woa_ref, wob_ref, woc_ref, wout_ref, h_ref):
    x = x_ref[...]
    xn = _rms(x, g_ref[...]).astype(BF16)

    def gate(i):
        return _sigmoid(_dot(xn, wg_ref[:, i * D_MODEL:(i + 1) * D_MODEL]))

    merged = gate(0) * _dot(og_ref[...].astype(BF16), woa_ref[...])
    merged = merged + gate(1) * _dot(p_ref[...].astype(BF16), wob_ref[...])
    ls = [l0_ref[...], l1_ref[...], l2_ref[...]]
    m = jnp.maximum(jnp.maximum(ls[0], ls[1]), ls[2])
    es = [jnp.exp(l - m) for l in ls]
    inv = 1.0 / (es[0] + es[1] + es[2])
    yc = jnp.zeros_like(x)
    for gi, o_ref in enumerate((o0_ref, o1_ref, o2_ref)):
        oc = (o_ref[...] * (es[gi] * inv)).astype(BF16)
        yc = yc + _dot(oc, woc_ref[gi * ATT_GW:(gi + 1) * ATT_GW, :])
    merged = merged + gate(2) * yc
    h_ref[...] = x + _dot(merged.astype(BF16), wout_ref[...])


def _merge(x2, og, p, o, l, g, wg, woa, wob, woc, wout, tm):
    m = x2.shape[0]
    row = lambda w: pl.BlockSpec((tm, w), lambda i: (i, 0))
    return pl.pallas_call(
        _merge_kernel,
        grid=(m // tm,),
        in_specs=[row(D_MODEL), row(GLA_VW), row(D_MODEL)] + [row(ATT_GW)] * 6
                 + [_resident(a.shape) for a in (g, wg, woa, wob, woc, wout)],
        out_specs=row(D_MODEL),
        out_shape=jax.ShapeDtypeStruct((m, D_MODEL), F32),
        compiler_params=_params(("parallel",)),
        name="merge",
    )(x2, og, p, *o, *l, g, wg, woa, wob, woc, wout)


FFN_TF = 256
FFN_HALO = 16


def _ffn_kernel(*refs, t_len, tm, use_halo, has_prev, final):
    it = iter(refs)
    h_ref = next(it)
    halo_ref = next(it) if use_halo else None
    g_ref, wup_ref, cw_ref, cb_ref, wd_ref, fg_ref = (next(it) for _ in range(6))
    c1_ref, c2_ref = (next(it), next(it)) if has_prev else (None, None)
    y_ref, cv_ref = next(it), next(it)
    hn_ref, hh_ref = next(it), next(it)
    a_scr = None if use_halo else next(it)
    ho = FFN_HALO if use_halo else 0
    if use_halo:
        hn_ref[0:ho, :] = _rms(halo_ref[...], g_ref[...]).astype(BF16)
    hn_ref[ho:ho + tm, :] = _rms(h_ref[...], g_ref[...]).astype(BF16)
    t_loc = (lax.broadcasted_iota(jnp.int32, (tm, 1), 0) + pl.program_id(0) * tm) % t_len
    for jt in range(D_FF // FFN_TF):
        cols = slice(jt * FFN_TF, (jt + 1) * FFN_TF)
        a_ext = _dot(hn_ref[...], wup_ref[:, cols])
        a = a_ext[ho:]
        bg = _dot(hn_ref[ho:ho + tm, :], wup_ref[:, D_FF + jt * FFN_TF:D_FF + (jt + 1) * FFN_TF])
        if use_halo:
            a1, a2 = a_ext[ho - 1:ho - 1 + tm], a_ext[ho - 2:ho - 2 + tm]
        else:
            a1, a2 = pltpu.roll(a, 1, axis=0), pltpu.roll(a, 2, axis=0)
        a1 = jnp.where(t_loc >= 1, a1, c1_ref[:, cols] if has_prev else 0.0)
        a2 = jnp.where(t_loc >= 2, a2, c2_ref[:, cols] if has_prev else 0.0)
        y = cb_ref[:, cols] + cw_ref[0:1, cols] * a2 + cw_ref[1:2, cols] * a1 + cw_ref[2:3, cols] * a
        hh_ref[:, cols] = ((y * _sigmoid(y)) * bg).astype(BF16)
        if use_halo:
            cv_ref[:, cols] = a[tm - 2:tm]
        else:
            nseq = tm // t_len
            for kk in range(FFN_TF // LANE):
                lc = slice(kk * LANE, (kk + 1) * LANE)
                oc = slice(jt * FFN_TF + kk * LANE, jt * FFN_TF + (kk + 1) * LANE)
                a_scr[kk] = a[:, lc]
                cv_ref[0, :, oc] = a_scr[kk, pl.ds(t_len - 2, nseq, stride=t_len), :]
                cv_ref[1, :, oc] = a_scr[kk, pl.ds(t_len - 1, nseq, stride=t_len), :]
    out = h_ref[...] + _dot(hh_ref[...], wd_ref[...])
    if final:
        out = _rms(out, fg_ref[...])
    y_ref[...] = out


def _ffn(h2, g, wup, cw, cb, wd, fg, prev, t_len, tm, final):
    m = h2.shape[0]
    bsz = m // t_len
    use_halo = tm % t_len != 0
    has_prev = prev is not None
    kern = functools.partial(_ffn_kernel, t_len=t_len, tm=tm, use_halo=use_halo, has_prev=has_prev, final=final)
    in_specs = [pl.BlockSpec((tm, D_MODEL), lambda i: (i, 0))]
    args = [h2]
    if use_halo:
        hb = tm // FFN_HALO
        in_specs.append(pl.BlockSpec((FFN_HALO, D_MODEL), lambda i: (jnp.maximum(i * hb - 1, 0), 0)))
        args.append(h2)
    in_specs += [_resident(a.shape) for a in (g, wup, cw, cb, wd, fg)]
    args += [g, wup, cw, cb, wd, fg]
    if has_prev:
        z = jnp.zeros((bsz, t_len - 2, D_FF), F32)
        c1 = jnp.concatenate([prev[:, 1:2], jnp.zeros((bsz, 1, D_FF), F32), z], axis=1).reshape(m, D_FF)
        c2 = jnp.concatenate([prev, z], axis=1).reshape(m, D_FF)
        in_specs += [pl.BlockSpec((tm, D_FF), lambda i: (i, 0))] * 2
        args += [c1, c2]
    scratch = [pltpu.VMEM(((FFN_HALO if use_halo else 0) + tm, D_MODEL), BF16), pltpu.VMEM((tm, D_FF), BF16)]
    if use_halo:
        cv_spec = pl.BlockSpec((None, 2, D_FF), lambda i: (i, 0, 0))
        cv_shape = jax.ShapeDtypeStruct((m // tm, 2, D_FF), F32)
    else:
        nseq = tm // t_len
        cv_spec = pl.BlockSpec((2, nseq, D_FF), lambda i: (0, i, 0))
        cv_shape = jax.ShapeDtypeStruct((2, bsz, D_FF), F32)
        scratch.append(pltpu.VMEM((FFN_TF // LANE, tm, LANE), F32))
    y, cv = pl.pallas_call(
        kern,
        grid=(m // tm,),
        in_specs=in_specs,
        out_specs=[pl.BlockSpec((tm, D_MODEL), lambda i: (i, 0)), cv_spec],
        out_shape=[jax.ShapeDtypeStruct((m, D_MODEL), F32), cv_shape],
        scratch_shapes=scratch,
        compiler_params=_params(("parallel",)),
        name="ffn",
    )(*args)
    if use_halo:
        per = t_len // tm
        cv = cv[per - 1::per]
    else:
        cv = jnp.swapaxes(cv, 0, 1)
    return y, cv


def _prep_weights(w_in, gla_wa2, pool_w, w_oa, w_ob, w_oc, w_out, ffn_w_up, ffn_w_down):
    o = _IN_OFF
    cast = lambda a: a.astype(BF16)
    w1 = cast(w_in[:, :, o[0]:o[4]])
    wal = cast(jnp.pad(w_in[:, :, o[4]:o[5]], ((0, 0), (0, 0), (0, LANE - GLA_GATE_RANK))))
    wu = cast(w_in[:, :, o[5]:o[6]])
    wa = cast(w_in[:, :, o[6]:o[9]])
    wg = cast(w_in[:, :, o[9]:o[12]])
    wa2p = cast(jnp.pad(gla_wa2, ((0, 0), (0, LANE - GLA_GATE_RANK), (0, 0))))
    return (w1, wal, wu, wa, wg, wa2p, cast(pool_w), cast(w_oa), cast(w_ob), cast(w_oc), cast(w_out),
            cast(ffn_w_up), cast(ffn_w_down))


def _cache_channel_major(c):
    d, b, w = c.shape[:3]
    return jnp.transpose(c, (0, 1, 3, 4, 2)).reshape(d, b, ATT_GW, w)


def _cache_token_major(ct):
    d, b, _, w = ct.shape
    return jnp.transpose(ct.reshape(d, b, ATT_HG, ATT_E, w), (0, 1, 4, 2, 3))


def _run_group(x, states, weights, small, tiles):
    bsz, t_len, _ = x.shape
    m = bsz * t_len
    prompt = states is None
    act = _act_dtype(t_len)
    w1, wal, wu, wa, wg, wa2p, pw, woa, wob, woc, wout, wup, wdn = weights
    norm1_g, norm2_g, gla_ba, gla_norm_g, pool_scale, conv_w, conv_b, final_g = small
    x2 = x.reshape(m, D_MODEL)
    new_gla, new_pool, new_kv, new_conv = [], [], [], []
    kv_t = None if prompt else [_cache_channel_major(c) for c in states["kv"]]
    kv_run = None
    for l in range(DEPTH):
        al, zg, u, a = _in_proj(x2, norm1_g[l][None], wal[l],
                                [(w1[l], ZG_TN, act), (wu[l], U_TN, F32), (wa[l], A_TN, act)], tiles["in_tm"], act)
        zg3 = zg.reshape(bsz, t_len, ZG_W)
        al3 = al.reshape(bsz, t_len, LANE)
        u3 = u.reshape(bsz, t_len, D_MODEL)
        a3 = a.reshape(bsz, t_len, A_W)
        if prompt:
            s0, s0_layer = jnp.zeros((1, bsz, GLA_HEADS, GLA_DK, GLA_DV), F32), 0
            prev = jnp.zeros((bsz, POOL_HALO, D_MODEL), F32)
            n_prev = 0
        else:
            s0, s0_layer = states["gla"], l
            prev = jnp.pad(states["pool"][l], ((0, 0), (POOL_HALO - POOL_STATE, 0), (0, 0)))
            n_prev = POOL_STATE
        og, g_new = _gla(zg3, al3, wa2p[l], gla_ba[l][None], gla_norm_g[l][None], s0, s0_layer, t_len,
                         tiles["gla_hps"])
        pb, p_new = _pool(u3, prev, pw[l], pool_scale[l][None], t_len, n_prev)
        if prompt:
            o, lse, kv = [], [], []
            for gi in range(len(ATT_GROUPS)):
                og_i, l_i, kc, vc = _attn_prompt(a3, gi)
                o.append(og_i)
                lse.append(l_i)
                kv += [kc, vc]
        else:
            o, lse, kv_run = _attn_sample(a3, kv_t, l, kv_run)
            kv = None
        h2 = _merge(x2, og.reshape(m, GLA_VW), pb.reshape(m, D_MODEL), o, lse, norm1_g[l][None], wg[l],
                    woa[l], wob[l], woc[l], wout[l], tiles["merge_tm"])
        x2, c_new = _ffn(h2, norm2_g[l][None], wup[l], conv_w[l], conv_b[l][None], wdn[l], final_g[None],
                         None if prompt else states["conv"][l], t_len, tiles["ffn_tm"], l == DEPTH - 1)
        new_gla.append(g_new)
        new_pool.append(p_new)
        new_kv.append(kv)
        new_conv.append(c_new)
    if prompt:
        kv_out = [jnp.stack([kvl[i] for kvl in new_kv], axis=0) for i in range(2 * len(ATT_GROUPS))]
    else:
        kv_out = [_cache_token_major(c) for c in kv_run]
    return (x2.reshape(bsz, t_len, D_MODEL), jnp.stack(new_gla, 0), jnp.stack(new_pool, 0), kv_out,
            jnp.stack(new_conv, 0))


def kernel(x_prompt, x_sample, state_gla, state_pool, cache_k_w128, cache_v_w128, cache_k_w512, cache_v_w512,
           cache_k_w2048, cache_v_w2048, state_ffn_conv, norm1_g, norm2_g, w_in, gla_wa2, gla_ba, gla_norm_g,
           pool_w, pool_scale, w_oa, w_ob, w_oc, w_out, ffn_w_up, ffn_conv_w, ffn_conv_b, ffn_w_down,
           final_norm_g):
    weights = _prep_weights(w_in, gla_wa2, pool_w, w_oa, w_ob, w_oc, w_out, ffn_w_up, ffn_w_down)
    small = (norm1_g, norm2_g, gla_ba, gla_norm_g, pool_scale, ffn_conv_w, ffn_conv_b, final_norm_g)
    y_p, gla_p, pool_p, kv_p, conv_p = _run_group(
        x_prompt, None, weights, small, dict(in_tm=1024, merge_tm=512, ffn_tm=1024, gla_hps=1))
    states = dict(gla=state_gla, pool=state_pool, conv=state_ffn_conv,
                  kv=[cache_k_w128, cache_v_w128, cache_k_w512, cache_v_w512, cache_k_w2048, cache_v_w2048])
    m_s = x_sample.shape[0] * x_sample.shape[1]
    y_s, gla_s, pool_s, kv_s, conv_s = _run_group(
        x_sample, states, weights, small, dict(in_tm=m_s, merge_tm=m_s, ffn_tm=m_s, gla_hps=GLA_HEADS))
    k128_p, v128_p, k512_p, v512_p, k2048_p, v2048_p = kv_p
    k128_s, v128_s, k512_s, v512_s, k2048_s, v2048_s = kv_s
    return (y_p, y_s, gla_p, gla_s, pool_p, pool_s, k128_p, k128_s, v128_p, v128_s,
            k512_p, k512_s, v512_p, v512_s, k2048_p, k2048_s, v2048_p, v2048_s, conv_p, conv_s)
```

```python
import functools

import jax
import jax.numpy as jnp
import numpy as np
from jax import lax
from jax.experimental import pallas as pl
from jax.experimental.pallas import tpu as pltpu

F32 = jnp.float32
BF16 = jnp.bfloat16

D_MODEL = 1024
DEPTH = 2
GLA_HEADS = 4
GLA_DK = 128
GLA_DV = 256
GLA_QK = GLA_HEADS * GLA_DK
GLA_VW = GLA_HEADS * GLA_DV
GLA_GATE_RANK = 16
GLA_GATE_NORM = 16.0
GLA_CHUNK = 64
GLA_SCAN_ROWS = 256
GLA_UNROLL = 16
POOL_WINDOWS = (2, 4, 8, 16)
POOL_GROUP = 256
POOL_STATE = 15
POOL_HALO = 16
ATT_GROUPS = ((128, 1), (512, 4), (2048, 16))
ATT_HG = 4
ATT_E = 64
ATT_GW = ATT_HG * ATT_E
ATT_SPAN = 128
ATT_UNROLL = 8
ATT_HEADS = 12
ATT_WIDTH = ATT_HEADS * ATT_E
D_FF = 2816
NORM_EPS = 1e-6
NEG_INF = -1e30

LANE = 128
SUBLANE = 8
BF16_SUBLANE = 16
VMEM_LIMIT = 56 * 1024 * 1024

_IN_SPLITS = (GLA_QK, GLA_QK, GLA_VW, GLA_VW, GLA_GATE_RANK, D_MODEL, ATT_WIDTH, ATT_WIDTH, ATT_WIDTH,
              D_MODEL, D_MODEL, D_MODEL)
_IN_OFF = [0] + [int(v) for v in np.cumsum(_IN_SPLITS)]
ZG_W = 2 * GLA_QK + 2 * GLA_VW
ZG_K, ZG_V, ZG_R = GLA_QK, 2 * GLA_QK, 2 * GLA_QK + GLA_VW
A_W = 3 * ATT_WIDTH
A_Q, A_K, A_V = 0, ATT_WIDTH, 2 * ATT_WIDTH
ZG_TN, U_TN, A_TN = 1024, 512, 768

_SLOPES = (2.0 ** (-8.0 * np.arange(1, ATT_HEADS + 1) / ATT_HEADS)).astype(np.float32)


def _params(sem):
    return pltpu.CompilerParams(dimension_semantics=sem, vmem_limit_bytes=VMEM_LIMIT)


def _resident(shape):
    return pl.BlockSpec(shape, lambda *_: (0,) * len(shape), pipeline_mode=pl.Buffered(1))


def _act_dtype(t_len):
    return BF16 if t_len % BF16_SUBLANE == 0 else F32


def _dot(a, b):
    return jnp.dot(a, b, preferred_element_type=F32)


def _dot_nt(a, b):
    return lax.dot_general(a, b, (((1,), (1,)), ((), ())), preferred_element_type=F32)


def _dot_tn(a, b):
    return lax.dot_general(a, b, (((0,), (0,)), ((), ())), preferred_element_type=F32)


def _rms(x, g):
    return x * lax.rsqrt(jnp.mean(x * x, axis=-1, keepdims=True) + NORM_EPS) * g


def _sigmoid(x):
    return 1.0 / (1.0 + jnp.exp(-x))


def _in_proj_kernel(x_ref, g_ref, wal_ref, *refs, tiles):
    nseg = len(tiles)
    w_refs, al_ref, out_refs, xn_ref = refs[:nseg], refs[nseg], refs[nseg + 1:2 * nseg + 1], refs[-1]
    xn_ref[...] = _rms(x_ref[...], g_ref[...]).astype(BF16)
    al_ref[...] = _dot(xn_ref[...], wal_ref[...]).astype(al_ref.dtype)
    for w_ref, o_ref, tn in zip(w_refs, out_refs, tiles):
        for c in range(w_ref.shape[1] // tn):
            cols = slice(c * tn, (c + 1) * tn)
            o_ref[:, cols] = _dot(xn_ref[...], w_ref[:, cols]).astype(o_ref.dtype)


def _in_proj(x2, g, wal, segs, tm, act):
    m = x2.shape[0]
    row = lambda w: pl.BlockSpec((tm, w), lambda i: (i, 0))
    return pl.pallas_call(
        functools.partial(_in_proj_kernel, tiles=tuple(tn for _, tn, _ in segs)),
        grid=(m // tm,),
        in_specs=[row(D_MODEL), _resident(g.shape), _resident(wal.shape)] + [_resident(w.shape) for w, _, _ in segs],
        out_specs=[row(LANE)] + [row(w.shape[1]) for w, _, _ in segs],
        out_shape=[jax.ShapeDtypeStruct((m, LANE), act)]
                  + [jax.ShapeDtypeStruct((m, w.shape[1]), dt) for w, _, dt in segs],
        scratch_shapes=[pltpu.VMEM((tm, D_MODEL), BF16)],
        compiler_params=_params(("parallel",)),
        name="in_proj",
    )(x2, g, wal, *[w for w, _, _ in segs])


def _gla_kernel(q_ref, k_ref, v_ref, r_ref, a_ref, wa_ref, ba_ref, gn_ref, s0_ref, o_ref, so_ref, *scratch,
                c, nc, hps):
    for hh in range(hps):
        dk = slice(hh * GLA_DK, (hh + 1) * GLA_DK)
        dv = slice(hh * GLA_DV, (hh + 1) * GLA_DV)
        _gla_head(q_ref.at[:, dk], k_ref.at[:, dk], v_ref.at[:, dv], r_ref.at[:, dv], a_ref, wa_ref.at[:, dk],
                  ba_ref.at[:, dk], gn_ref, s0_ref.at[hh], o_ref.at[:, dv], so_ref.at[hh], *scratch, c=c, nc=nc)


def _gla_head(q_ref, k_ref, v_ref, r_ref, a_ref, wa_ref, ba_ref, gn_ref, s0_ref, o_ref, so_ref,
              qe_ref, ke_ref, qi_ref, ks_ref, vb_ref, att_ref, dec_ref, oacc_ref, kv_ref, st_ref, *, c, nc):
    ce = max(c, BF16_SUBLANE)
    tp = nc * ce
    mid = (c - 1) // 2

    def load(ref):
        x = ref[...].astype(F32)
        if ce != c:
            x = jnp.concatenate([x, jnp.zeros((ce - c, x.shape[1]), F32)], axis=0)
        return x

    z = _dot(load(a_ref).astype(BF16), wa_ref[...]) + ba_ref[...]
    la = (jnp.minimum(z, 0.0) - jnp.log(1.0 + jnp.exp(-jnp.abs(z)))) * (1.0 / GLA_GATE_NORM)
    if ce != c:
        la = jnp.where(lax.broadcasted_iota(jnp.int32, (tp, GLA_DK), 0) < c, la, 0.0)
    slab = min(tp, GLA_SCAN_ROWS)
    ri = lax.broadcasted_iota(jnp.int32, (slab, slab), 0)
    ci = lax.broadcasted_iota(jnp.int32, (slab, slab), 1)
    tri = jnp.where((ri >= ci) & ((ri & -ce) == (ci & -ce)), 1.0, 0.0).astype(BF16)
    la_hi = la.astype(BF16)
    la_lo = (la - la_hi.astype(F32)).astype(BF16)
    b = jnp.concatenate(
        [_dot(tri, la_hi[i * slab:(i + 1) * slab]) + _dot(tri, la_lo[i * slab:(i + 1) * slab])
         for i in range(tp // slab)], axis=0)
    b3 = b.reshape(nc, ce, GLA_DK)
    b_mid = b3[:, mid:mid + 1, :]
    b_last = b3[:, c - 1:c, :]
    q3 = (load(q_ref) * (GLA_DK ** -0.5)).reshape(nc, ce, GLA_DK)
    k3 = load(k_ref).reshape(nc, ce, GLA_DK)
    flat = lambda x: x.reshape(tp, GLA_DK).astype(BF16)
    qe_ref[...] = flat(q3 * jnp.exp(b3 - b_mid))
    ke_ref[...] = flat(k3 * jnp.exp(b_mid - b3))
    qi_ref[...] = flat(q3 * jnp.exp(b3))
    ks_ref[...] = flat(k3 * jnp.exp(b_last - b3))
    dec_ref[...] = jnp.exp(b_last)
    vb_ref[...] = load(v_ref).astype(BF16)
    causal = (lax.broadcasted_iota(jnp.int32, (ce, ce), 0) >= lax.broadcasted_iota(jnp.int32, (ce, ce), 1))
    rows = lambda n: pl.ds(pl.multiple_of(n * ce, ce), ce)

    def scores(n, carry):
        sl = rows(n)
        att_ref[sl, :] = jnp.where(causal, _dot_nt(qe_ref[sl, :], ke_ref[sl, :]), 0.0).astype(BF16)
        return carry

    def within(n, carry):
        sl = rows(n)
        oacc_ref[sl, :] = _dot(att_ref[sl, :], vb_ref[sl, :])
        return carry

    def increments(n, carry):
        sl = rows(n)
        kv_ref[n] = _dot_tn(vb_ref[sl, :], ks_ref[sl, :])
        return carry

    def across(n, carry):
        sl = rows(n)
        st = st_ref[...]
        oacc_ref[sl, :] += _dot_nt(qi_ref[sl, :], st.astype(BF16))
        st_ref[...] = st * dec_ref[n] + kv_ref[n]
        return carry

    unroll = min(nc, GLA_UNROLL)
    lax.fori_loop(0, nc, scores, 0, unroll=unroll)
    lax.fori_loop(0, nc, within, 0, unroll=unroll)
    lax.fori_loop(0, nc, increments, 0, unroll=unroll)
    st_ref[...] = s0_ref[...].T
    lax.fori_loop(0, nc, across, 0, unroll=unroll)
    r = load(r_ref)
    res = _rms(oacc_ref[...], gn_ref[...]) * (r * _sigmoid(r))
    o_ref[...] = res[:nc * c if ce == c else c].astype(o_ref.dtype)
    so_ref[...] = st_ref[...].T


def _gla(zg3, al3, wa2p, ba, gn, s0, layer, t_len, hps):
    bsz = zg3.shape[0]
    c = GLA_CHUNK if t_len % GLA_CHUNK == 0 else t_len
    nc = t_len // c
    ce = max(c, BF16_SUBLANE)
    assert ce == c or nc == 1
    tp = nc * ce
    kern = functools.partial(_gla_kernel, c=c, nc=nc, hps=hps)
    zspec = lambda w, off: pl.BlockSpec((None, t_len, hps * w), lambda b, h: (b, 0, off // (hps * w) + h))
    return pl.pallas_call(
        kern,
        grid=(bsz, GLA_HEADS // hps),
        in_specs=[
            zspec(GLA_DK, 0), zspec(GLA_DK, ZG_K), zspec(GLA_DV, ZG_V), zspec(GLA_DV, ZG_R),
            pl.BlockSpec((None, t_len, LANE), lambda b, h: (b, 0, 0)),
            pl.BlockSpec((LANE, hps * GLA_DK), lambda b, h: (0, h)),
            pl.BlockSpec((1, hps * GLA_DK), lambda b, h: (0, h)),
            pl.BlockSpec((1, GLA_DV), lambda b, h: (0, 0)),
            pl.BlockSpec((None, None, hps, GLA_DK, GLA_DV), lambda b, h: (layer, b, h, 0, 0)),
        ],
        out_specs=[
            pl.BlockSpec((None, t_len, hps * GLA_DV), lambda b, h: (b, 0, h)),
            pl.BlockSpec((None, hps, GLA_DK, GLA_DV), lambda b, h: (b, h, 0, 0)),
        ],
        out_shape=[
            jax.ShapeDtypeStruct((bsz, t_len, GLA_VW), zg3.dtype),
            jax.ShapeDtypeStruct((bsz, GLA_HEADS, GLA_DK, GLA_DV), F32),
        ],
        scratch_shapes=[pltpu.VMEM((tp, GLA_DK), BF16)] * 4 + [
            pltpu.VMEM((tp, GLA_DV), BF16),
            pltpu.VMEM((tp, ce), BF16),
            pltpu.VMEM((nc, 1, GLA_DK), F32),
            pltpu.VMEM((tp, GLA_DV), F32),
            pltpu.VMEM((nc, GLA_DV, GLA_DK), F32),
            pltpu.VMEM((GLA_DV, GLA_DK), F32),
        ],
        compiler_params=_params(("parallel", "arbitrary")),
        name="gla",
    )(zg3, zg3, zg3, zg3, al3, wa2p, ba, gn, s0)


def _pool_kernel(u_ref, prev_ref, pw_ref, ps_ref, p_ref, pn_ref, *, t_len, n_prev):
    ext = jnp.concatenate([prev_ref[...], u_ref[...]], axis=0)
    t_abs = lax.broadcasted_iota(jnp.int32, (t_len, 1), 0) + n_prev
    for gi, w in enumerate(POOL_WINDOWS):
        cols = slice(gi * POOL_GROUP, (gi + 1) * POOL_GROUP)
        x = ext[:, cols]
        acc = x
        s = 1
        while s < w:
            acc = acc + pltpu.roll(acc, s, axis=0)
            s *= 2
        cnt = jnp.minimum(t_abs + 1, w).astype(F32)
        pooled = acc[POOL_HALO:] / cnt - x[POOL_HALO:]
        mixed = _dot(pooled.astype(BF16), pw_ref[gi]) * ps_ref[:, cols]
        p_ref[:, cols] = mixed.astype(p_ref.dtype)
    keep = max(0, POOL_STATE - t_len)
    if keep:
        pn_ref[0:keep, :] = prev_ref[POOL_HALO - keep:POOL_HALO, :]
    pn_ref[keep:POOL_STATE, :] = u_ref[t_len - (POOL_STATE - keep):t_len, :]


def _pool(u3, prev, pw, ps, t_len, n_prev):
    bsz = u3.shape[0]
    kern = functools.partial(_pool_kernel, t_len=t_len, n_prev=n_prev)
    return pl.pallas_call(
        kern,
        grid=(bsz,),
        in_specs=[
            pl.BlockSpec((None, t_len, D_MODEL), lambda b: (b, 0, 0)),
            pl.BlockSpec((None, POOL_HALO, D_MODEL), lambda b: (b, 0, 0)),
            pl.BlockSpec((4, POOL_GROUP, POOL_GROUP), lambda b: (0, 0, 0)),
            pl.BlockSpec((1, D_MODEL), lambda b: (0, 0)),
        ],
        out_specs=[
            pl.BlockSpec((None, t_len, D_MODEL), lambda b: (b, 0, 0)),
            pl.BlockSpec((None, POOL_STATE, D_MODEL), lambda b: (b, 0, 0)),
        ],
        out_shape=[
            jax.ShapeDtypeStruct((bsz, t_len, D_MODEL), _act_dtype(t_len)),
            jax.ShapeDtypeStruct((bsz, POOL_STATE, D_MODEL), F32),
        ],
        compiler_params=_params(("parallel",)),
        name="pool",
    )(u3, prev, pw, ps)


def _attn_prompt_kernel(q_ref, k_ref, v_ref, o_ref, l_ref, kc_ref, vc_ref,
                        qm_ref, ks_ref, vs_ref, os_ref, ls_ref, bias_ref, s_ref, m_ref, stage_ref,
                        *, t_len, keep, dil, slopes):
    n = t_len // dil
    nb = n // ATT_SPAN
    hp = pl.program_id(1)
    kc_ref[...] = k_ref[t_len - keep:t_len, :].astype(F32)
    vc_ref[...] = v_ref[t_len - keep:t_len, :].astype(F32)
    via_swap = dil % SUBLANE == 0

    def to_res(ref):
        x = ref[...].astype(F32)
        if dil == 1:
            return x[None]
        if via_swap:
            return jnp.swapaxes(x.reshape(n, dil, LANE), 0, 1)
        stage_ref[...] = x
        return jnp.stack([stage_ref[pl.ds(r, n, stride=dil), :] for r in range(dil)], axis=0)

    q3 = to_res(q_ref) * (ATT_E ** -0.5)
    lane3 = lax.broadcasted_iota(jnp.int32, q3.shape, 2)
    qm_ref[0] = jnp.where(lane3 < ATT_E, q3, 0.0).astype(BF16)
    qm_ref[1] = jnp.where(lane3 >= ATT_E, q3, 0.0).astype(BF16)
    ks_ref[...] = to_res(k_ref).astype(BF16)
    vs_ref[:, :, 0:LANE] = to_res(v_ref).astype(BF16)
    vs_ref[:, :, LANE:2 * LANE] = jnp.ones((dil, n, LANE), BF16)

    nk = ATT_SPAN if nb == 1 else 2 * ATT_SPAN
    a_idx = lax.broadcasted_iota(jnp.int32, (ATT_SPAN, nk), 0)
    c_idx = lax.broadcasted_iota(jnp.int32, (ATT_SPAN, nk), 1)
    for var in range(1 if nb == 1 else 2):
        j = a_idx - c_idx + var * ATT_SPAN
        valid = (j >= 0) & (j <= ATT_SPAN)
        for h in range(2):
            scale = jnp.where(hp == 0, -slopes[h] * dil, -slopes[2 + h] * dil)
            bias_ref[h, var] = jnp.where(valid, j.astype(F32) * scale, NEG_INF)

    lane = lax.broadcasted_iota(jnp.int32, (ATT_SPAN, LANE), 1)
    hm = [lane < ATT_E, lane >= ATT_E]

    def slices(t):
        r, qi = t // nb, t % nb
        qsl = pl.ds(pl.multiple_of(qi * ATT_SPAN, ATT_SPAN), ATT_SPAN)
        if nb > 1:
            return r, qsl, pl.ds(pl.multiple_of(jnp.maximum(qi - 1, 0) * ATT_SPAN, ATT_SPAN), nk), jnp.minimum(qi, 1)
        return r, qsl, qsl, 0

    def scores(t, carry):
        r, qsl, ksl, var = slices(t)
        kk = ks_ref[r, ksl, :]
        for h in range(2):
            s = _dot_nt(qm_ref[h, r, qsl, :], kk) + bias_ref[h, var]
            s_ref[t, h] = s
            m_ref[t, h] = jnp.max(s, axis=-1, keepdims=True)
        return carry

    def outputs(t, carry):
        r, qsl, ksl, _ = slices(t)
        vv = vs_ref[r, ksl, :]
        o_acc = jnp.zeros((ATT_SPAN, LANE), F32)
        l_acc = jnp.zeros((ATT_SPAN, LANE), F32)
        for h in range(2):
            m = m_ref[t, h]
            p = jnp.exp(s_ref[t, h] - m).astype(BF16)
            od = _dot(p, vv)
            den = od[:, LANE:]
            o_acc = jnp.where(hm[h], od[:, :LANE] / den, o_acc)
            l_acc = jnp.where(hm[h], m + jnp.log(den), l_acc)
        os_ref[r, qsl, :] = o_acc
        ls_ref[r, qsl, :] = l_acc
        return carry

    lax.fori_loop(0, dil * nb, scores, 0, unroll=ATT_UNROLL)
    lax.fori_loop(0, dil * nb, outputs, 0, unroll=ATT_UNROLL)
    for res_ref, out_ref in ((os_ref, o_ref), (ls_ref, l_ref)):
        if dil == 1:
            out_ref[...] = res_ref[0]
        elif via_swap:
            out_ref[...] = jnp.swapaxes(res_ref[...], 0, 1).reshape(t_len, LANE)
        else:
            for r in range(dil):
                out_ref[pl.ds(r, n, stride=dil), :] = res_ref[r]


def _attn_prompt(a3, gi):
    bsz, t_len, _ = a3.shape
    win, dil = ATT_GROUPS[gi]
    keep = min(win, t_len)
    kern = functools.partial(_attn_prompt_kernel, t_len=t_len, keep=keep, dil=dil,
                             slopes=[float(s) for s in _SLOPES[gi * ATT_HG:(gi + 1) * ATT_HG]])
    uspec = lambda off: pl.BlockSpec((None, t_len, LANE), lambda b, hp: (b, 0, (off + gi * ATT_GW) // LANE + hp))
    ospec = lambda rows: pl.BlockSpec((None, rows, LANE), lambda b, hp: (b, 0, hp))
    n = t_len // dil
    nk = ATT_SPAN if n == ATT_SPAN else 2 * ATT_SPAN
    nblk = t_len // ATT_SPAN
    scratch = [pltpu.VMEM((2, dil, n, LANE), BF16), pltpu.VMEM((dil, n, LANE), BF16),
               pltpu.VMEM((dil, n, 2 * LANE), BF16), pltpu.VMEM((dil, n, LANE), F32),
               pltpu.VMEM((dil, n, LANE), F32), pltpu.VMEM((2, 2, ATT_SPAN, nk), F32),
               pltpu.VMEM((nblk, 2, ATT_SPAN, nk), F32), pltpu.VMEM((nblk, 2, ATT_SPAN, 1), F32),
               pltpu.VMEM((t_len, LANE), F32)]
    o, l, kc, vc = pl.pallas_call(
        kern,
        grid=(bsz, ATT_GW // LANE),
        in_specs=[uspec(A_Q), uspec(A_K), uspec(A_V)],
        out_specs=[ospec(t_len), ospec(t_len), ospec(keep), ospec(keep)],
        out_shape=[
            jax.ShapeDtypeStruct((bsz, t_len, ATT_GW), F32),
            jax.ShapeDtypeStruct((bsz, t_len, ATT_GW), F32),
            jax.ShapeDtypeStruct((bsz, keep, ATT_GW), F32),
            jax.ShapeDtypeStruct((bsz, keep, ATT_GW), F32),
        ],
        scratch_shapes=scratch,
        compiler_params=_params(("parallel", "parallel")),
        name=f"attn_prompt_g{gi}",
    )(a3, a3, a3)
    m = bsz * t_len
    return (o.reshape(m, ATT_GW), l.reshape(m, ATT_GW),
            kc.reshape(bsz, keep, ATT_HG, ATT_E), vc.reshape(bsz, keep, ATT_HG, ATT_E))


_KX_PAD = LANE


def _head_masks(rows):
    lane = lax.broadcasted_iota(jnp.int32, (rows, ATT_GW), 1)
    return [(lane >= h * ATT_E) & (lane < (h + 1) * ATT_E) for h in range(ATT_HG)]


def _attn_sample_kernel(*refs, t_len, first, layer_slopes):
    ng = len(ATT_GROUPS)
    qkv = refs[0:3 * ng]
    caches = refs[3 * ng:5 * ng]
    outs = refs[5 * ng:] if first else refs[7 * ng:]
    o_refs, l_refs, cache_out = outs[0:ng], outs[ng:2 * ng], outs[2 * ng:4 * ng]
    kx_ref, vx_ref = outs[4 * ng:]

    def shift_in(c_ref, new_t):
        ext = jnp.concatenate([c_ref[...], new_t], axis=1)
        return ext, pltpu.roll(ext, ext.shape[1] - t_len, axis=1)

    def attend():
        hm = _head_masks(t_len)
        rows = ATT_HG * t_len
        zrows = jnp.zeros((_KX_PAD - t_len, ATT_GW), F32)
        for gi, (win, dil) in enumerate(ATT_GROUPS):
            q_ref, kn_ref, vn_ref = qkv[3 * gi:3 * gi + 3]
            wb = caches[2 * gi].shape[1]
            nk = wb + _KX_PAD
            for c_ref, n_ref, x_ref, co_ref in ((caches[2 * gi], kn_ref, kx_ref, cache_out[2 * gi]),
                                                (caches[2 * gi + 1], vn_ref, vx_ref, cache_out[2 * gi + 1])):
                new_t = jnp.concatenate([n_ref[...], zrows], axis=0).T
                ext, moved = shift_in(c_ref, new_t)
                x_ref[:, 0:nk] = ext.astype(BF16)
                co_ref[...] = moved[:, 0:wb] if first else moved[:, wb - LANE:wb]
            q = q_ref[...] * (ATT_E ** -0.5)
            qst = jnp.concatenate([jnp.where(hm[h], q, 0.0) for h in range(ATT_HG)], axis=0).astype(BF16)
            s = _dot(qst, kx_ref[:, 0:nk])
            ri = lax.broadcasted_iota(jnp.int32, (rows, nk), 0)
            ci = lax.broadcasted_iota(jnp.int32, (rows, nk), 1)
            dist = wb + (ri % t_len) - ci
            valid = (dist >= 0) & (dist <= win) & ((dist & (dil - 1)) == 0)
            hrow = lax.broadcasted_iota(jnp.int32, (rows, 1), 0) // t_len
            slope = jnp.zeros((rows, 1), F32)
            for h in range(ATT_HG):
                slope = jnp.where(hrow == h, float(layer_slopes[gi * ATT_HG + h]), slope)
            s = jnp.where(valid, s - slope * dist.astype(F32), NEG_INF)
            m = jnp.max(s, axis=-1, keepdims=True)
            p = jnp.exp(s - m)
            den = jnp.sum(p, axis=-1, keepdims=True)
            ost = _dot_nt(p.astype(BF16), vx_ref[:, 0:nk]) / den
            lst = m + jnp.log(den)
            o = jnp.zeros((t_len, ATT_GW), F32)
            l = jnp.zeros((t_len, ATT_GW), F32)
            for h in range(ATT_HG):
                o = jnp.where(hm[h], ost[h * t_len:(h + 1) * t_len], o)
                l = jnp.where(hm[h], lst[h * t_len:(h + 1) * t_len], l)
            o_refs[gi][...] = o
            l_refs[gi][...] = l

    if not first:
        attend()
        return
    pl.when(pl.program_id(1) == 0)(attend)

    @pl.when(pl.program_id(1) > 0)
    def _():
        for c_ref, co_ref in zip(caches, cache_out):
            _, moved = shift_in(c_ref, jnp.zeros((ATT_GW, _KX_PAD), F32))
            co_ref[...] = moved[:, 0:c_ref.shape[1]]


def _attn_sample(a3, caches_t, layer, cache_prev):
    bsz, t_len, _ = a3.shape
    ng = len(ATT_GROUPS)
    first = cache_prev is None
    kern = functools.partial(_attn_sample_kernel, t_len=t_len, first=first,
                             layer_slopes=[float(s) for s in _SLOPES])
    in_specs, args = [], []
    for gi in range(ng):
        for off in (A_Q, A_K, A_V):
            in_specs.append(pl.BlockSpec((None, t_len, ATT_GW), lambda b, *_, o=off // ATT_GW + gi: (b, 0, o)))
            args.append(a3)
    small = pl.BlockSpec((None, t_len, ATT_GW), lambda b, *_: (b, 0, 0))
    out_specs = [small] * (2 * ng)
    out_shape = [jax.ShapeDtypeStruct((bsz, t_len, ATT_GW), F32)] * (2 * ng)
    wbs = [c.shape[3] for c in caches_t]
    for c, wb in zip(caches_t, wbs):
        whole = (None, None, ATT_GW, wb)
        if first:
            in_specs.append(pl.BlockSpec(whole, lambda b, l: (l, b, 0, 0)))
            out_specs.append(pl.BlockSpec(whole, lambda b, l: (l, b, 0, 0)))
        else:
            in_specs.append(pl.BlockSpec(whole, lambda b: (layer, b, 0, 0)))
            out_specs.append(pl.BlockSpec((None, None, ATT_GW, LANE), lambda b, t=wb // LANE - 1: (layer, b, 0, t)))
        args.append(c)
        out_shape.append(jax.ShapeDtypeStruct(c.shape, F32))
    aliases = {}
    if not first:
        for i, c in enumerate(cache_prev):
            in_specs.append(pl.BlockSpec(memory_space=pl.ANY))
            args.append(c)
            aliases[5 * ng + i] = 2 * ng + i
    res = pl.pallas_call(
        kern,
        grid=(bsz, DEPTH) if first else (bsz,),
        in_specs=in_specs,
        out_specs=out_specs,
        out_shape=out_shape,
        scratch_shapes=[pltpu.VMEM((ATT_GW, max(wbs) + _KX_PAD), BF16)] * 2,
        input_output_aliases=aliases,
        compiler_params=_params(("parallel", "arbitrary") if first else ("parallel",)),
        name="attn_sample",
    )(*args)
    m = bsz * t_len
    o = [r.reshape(m, ATT_GW) for r in res[0:ng]]
    l = [r.reshape(m, ATT_GW) for r in res[ng:2 * ng]]
    return o, l, list(res[2 * ng:])


def _merge_kernel(x_ref, og_ref, p_ref, o0_ref, o1_ref, o2_ref, l0_ref, l1_ref, l2_ref, g_ref, wg_ref,
                  woa_ref, wob_ref, woc_ref, wout_ref, h_ref):
    x = x_ref[...]
    xn = _rms(x, g_ref[...]).astype(BF16)

    def gate(i):
        return _sigmoid(_dot(xn, wg_ref[:, i * D_MODEL:(i + 1) * D_MODEL]))

    merged = gate(0) * _dot(og_ref[...].astype(BF16), woa_ref[...])
    merged = merged + gate(1) * _dot(p_ref[...].astype(BF16), wob_ref[...])
    ls = [l0_ref[...], l1_ref[...], l2_ref[...]]
    m = jnp.maximum(jnp.maximum(ls[0], ls[1]), ls[2])
    es = [jnp.exp(l - m) for l in ls]
    inv = 1.0 / (es[0] + es[1] + es[2])
    yc = jnp.zeros_like(x)
    for gi, o_ref in enumerate((o0_ref, o1_ref, o2_ref)):
        oc = (o_ref[...] * (es[gi] * inv)).astype(BF16)
        yc = yc + _dot(oc, woc_ref[gi * ATT_GW:(gi + 1) * ATT_GW, :])
    merged = merged + gate(2) * yc
    h_ref[...] = x + _dot(merged.astype(BF16), wout_ref[...])


def _merge(x2, og, p, o, l, g, wg, woa, wob, woc, wout, tm):
    m = x2.shape[0]
    row = lambda w: pl.BlockSpec((tm, w), lambda i: (i, 0))
    return pl.pallas_call(
        _merge_kernel,
        grid=(m // tm,),
        in_specs=[row(D_MODEL), row(GLA_VW), row(D_MODEL)] + [row(ATT_GW)] * 6
                 + [_resident(a.shape) for a in (g, wg, woa, wob, woc, wout)],
        out_specs=row(D_MODEL),
        out_shape=jax.ShapeDtypeStruct((m, D_MODEL), F32),
        compiler_params=_params(("parallel",)),
        name="merge",
    )(x2, og, p, *o, *l, g, wg, woa, wob, woc, wout)


FFN_TF = 256
FFN_HALO = 16


def _ffn_kernel(*refs, t_len, tm, use_halo, has_prev, final):
    it = iter(refs)
    h_ref = next(it)
    halo_ref = next(it) if use_halo else None
    g_ref, wup_ref, cw_ref, cb_ref, wd_ref, fg_ref = (next(it) for _ in range(6))
    c1_ref, c2_ref = (next(it), next(it)) if has_prev else (None, None)
    y_ref, cv_ref = next(it), next(it)
    hn_ref, hh_ref = next(it), next(it)
    a_scr = None if use_halo else next(it)
    ho = FFN_HALO if use_halo else 0
    if use_halo:
        hn_ref[0:ho, :] = _rms(halo_ref[...], g_ref[...]).astype(BF16)
    hn_ref[ho:ho + tm, :] = _rms(h_ref[...], g_ref[...]).astype(BF16)
    t_loc = (lax.broadcasted_iota(jnp.int32, (tm, 1), 0) + pl.program_id(0) * tm) % t_len
    for jt in range(D_FF // FFN_TF):
        cols = slice(jt * FFN_TF, (jt + 1) * FFN_TF)
        a_ext = _dot(hn_ref[...], wup_ref[:, cols])
        a = a_ext[ho:]
        bg = _dot(hn_ref[ho:ho + tm, :], wup_ref[:, D_FF + jt * FFN_TF:D_FF + (jt + 1) * FFN_TF])
        if use_halo:
            a1, a2 = a_ext[ho - 1:ho - 1 + tm], a_ext[ho - 2:ho - 2 + tm]
        else:
            a1, a2 = pltpu.roll(a, 1, axis=0), pltpu.roll(a, 2, axis=0)
        a1 = jnp.where(t_loc >= 1, a1, c1_ref[:, cols] if has_prev else 0.0)
        a2 = jnp.where(t_loc >= 2, a2, c2_ref[:, cols] if has_prev else 0.0)
        y = cb_ref[:, cols] + cw_ref[0:1, cols] * a2 + cw_ref[1:2, cols] * a1 + cw_ref[2:3, cols] * a
        hh_ref[:, cols] = ((y * _sigmoid(y)) * bg).astype(BF16)
        if use_halo:
            cv_ref[:, cols] = a[tm - 2:tm]
        else:
            nseq = tm // t_len
            for kk in range(FFN_TF // LANE):
                lc = slice(kk * LANE, (kk + 1) * LANE)
                oc = slice(jt * FFN_TF + kk * LANE, jt * FFN_TF + (kk + 1) * LANE)
                a_scr[kk] = a[:, lc]
                cv_ref[0, :, oc] = a_scr[kk, pl.ds(t_len - 2, nseq, stride=t_len), :]
                cv_ref[1, :, oc] = a_scr[kk, pl.ds(t_len - 1, nseq, stride=t_len), :]
    out = h_ref[...] + _dot(hh_ref[...], wd_ref[...])
    if final:
        out = _rms(out, fg_ref[...])
    y_ref[...] = out


def _ffn(h2, g, wup, cw, cb, wd, fg, prev, t_len, tm, final):
    m = h2.shape[0]
    bsz = m // t_len
    use_halo = tm % t_len != 0
    has_prev = prev is not None
    kern = functools.partial(_ffn_kernel, t_len=t_len, tm=tm, use_halo=use_halo, has_prev=has_prev, final=final)
    in_specs = [pl.BlockSpec((tm, D_MODEL), lambda i: (i, 0))]
    args = [h2]
    if use_halo:
        hb = tm // FFN_HALO
        in_specs.append(pl.BlockSpec((FFN_HALO, D_MODEL), lambda i: (jnp.maximum(i * hb - 1, 0), 0)))
        args.append(h2)
    in_specs += [_resident(a.shape) for a in (g, wup, cw, cb, wd, fg)]
    args += [g, wup, cw, cb, wd, fg]
    if has_prev:
        z = jnp.zeros((bsz, t_len - 2, D_FF), F32)
        c1 = jnp.concatenate([prev[:, 1:2], jnp.zeros((bsz, 1, D_FF), F32), z], axis=1).reshape(m, D_FF)
        c2 = jnp.concatenate([prev, z], axis=1).reshape(m, D_FF)
        in_specs += [pl.BlockSpec((tm, D_FF), lambda i: (i, 0))] * 2
        args += [c1, c2]
    scratch = [pltpu.VMEM(((FFN_HALO if use_halo else 0) + tm, D_MODEL), BF16), pltpu.VMEM((tm, D_FF), BF16)]
    if use_halo:
        cv_spec = pl.BlockSpec((None, 2, D_FF), lambda i: (i, 0, 0))
        cv_shape = jax.ShapeDtypeStruct((m // tm, 2, D_FF), F32)
    else:
        nseq = tm // t_len
        cv_spec = pl.BlockSpec((2, nseq, D_FF), lambda i: (0, i, 0))
        cv_shape = jax.ShapeDtypeStruct((2, bsz, D_FF), F32)
        scratch.append(pltpu.VMEM((FFN_TF // LANE, tm, LANE), F32))
    y, cv = pl.pallas_call(
        kern,
        grid=(m // tm,),
        in_specs=in_specs,
        out_specs=[pl.BlockSpec((tm, D_MODEL), lambda i: (i, 0)), cv_spec],
        out_shape=[jax.ShapeDtypeStruct((m, D_MODEL), F32), cv_shape],
        scratch_shapes=scratch,
        compiler_params=_params(("parallel",)),
        name="ffn",
    )(*args)
    if use_halo:
        per = t_len // tm
        cv = cv[per - 1::per]
    else:
        cv = jnp.swapaxes(cv, 0, 1)
    return y, cv


def _prep_weights(w_in, gla_wa2, pool_w, w_oa, w_ob, w_oc, w_out, ffn_w_up, ffn_w_down):
    o = _IN_OFF
    cast = lambda a: a.astype(BF16)
    w1 = cast(w_in[:, :, o[0]:o[4]])
    wal = cast(jnp.pad(w_in[:, :, o[4]:o[5]], ((0, 0), (0, 0), (0, LANE - GLA_GATE_RANK))))
    wu = cast(w_in[:, :, o[5]:o[6]])
    wa = cast(w_in[:, :, o[6]:o[9]])
    wg = cast(w_in[:, :, o[9]:o[12]])
    wa2p = cast(jnp.pad(gla_wa2, ((0, 0), (0, LANE - GLA_GATE_RANK), (0, 0))))
    return (w1, wal, wu, wa, wg, wa2p, cast(pool_w), cast(w_oa), cast(w_ob), cast(w_oc), cast(w_out),
            cast(ffn_w_up), cast(ffn_w_down))


def _cache_channel_major(c):
    d, b, w = c.shape[:3]
    return jnp.transpose(c, (0, 1, 3, 4, 2)).reshape(d, b, ATT_GW, w)


def _cache_token_major(ct):
    d, b, _, w = ct.shape
    return jnp.transpose(ct.reshape(d, b, ATT_HG, ATT_E, w), (0, 1, 4, 2, 3))


def _run_group(x, states, weights, small, tiles):
    bsz, t_len, _ = x.shape
    m = bsz * t_len
    prompt = states is None
    act = _act_dtype(t_len)
    w1, wal, wu, wa, wg, wa2p, pw, woa, wob, woc, wout, wup, wdn = weights
    norm1_g, norm2_g, gla_ba, gla_norm_g, pool_scale, conv_w, conv_b, final_g = small
    x2 = x.reshape(m, D_MODEL)
    new_gla, new_pool, new_kv, new_conv = [], [], [], []
    kv_t = None if prompt else [_cache_channel_major(c) for c in states["kv"]]
    kv_run = None
    for l in range(DEPTH):
        al, zg, u, a = _in_proj(x2, norm1_g[l][None], wal[l],
                                [(w1[l], ZG_TN, act), (wu[l], U_TN, F32), (wa[l], A_TN, act)], tiles["in_tm"], act)
        zg3 = zg.reshape(bsz, t_len, ZG_W)
        al3 = al.reshape(bsz, t_len, LANE)
        u3 = u.reshape(bsz, t_len, D_MODEL)
        a3 = a.reshape(bsz, t_len, A_W)
        if prompt:
            s0, s0_layer = jnp.zeros((1, bsz, GLA_HEADS, GLA_DK, GLA_DV), F32), 0
            prev = jnp.zeros((bsz, POOL_HALO, D_MODEL), F32)
            n_prev = 0
        else:
            s0, s0_layer = states["gla"], l
            prev = jnp.pad(states["pool"][l], ((0, 0), (POOL_HALO - POOL_STATE, 0), (0, 0)))
            n_prev = POOL_STATE
        og, g_new = _gla(zg3, al3, wa2p[l], gla_ba[l][None], gla_norm_g[l][None], s0, s0_layer, t_len,
                         tiles["gla_hps"])
        pb, p_new = _pool(u3, prev, pw[l], pool_scale[l][None], t_len, n_prev)
        if prompt:
            o, lse, kv = [], [], []
            for gi in range(len(ATT_GROUPS)):
                og_i, l_i, kc, vc = _attn_prompt(a3, gi)
                o.append(og_i)
                lse.append(l_i)
                kv += [kc, vc]
        else:
            o, lse, kv_run = _attn_sample(a3, kv_t, l, kv_run)
            kv = None
        h2 = _merge(x2, og.reshape(m, GLA_VW), pb.reshape(m, D_MODEL), o, lse, norm1_g[l][None], wg[l],
                    woa[l], wob[l], woc[l], wout[l], tiles["merge_tm"])
        x2, c_new = _ffn(h2, norm2_g[l][None], wup[l], conv_w[l], conv_b[l][None], wdn[l], final_g[None],
                         None if prompt else states["conv"][l], t_len, tiles["ffn_tm"], l == DEPTH - 1)
        new_gla.append(g_new)
        new_pool.append(p_new)
        new_kv.append(kv)
        new_conv.append(c_new)
    if prompt:
        kv_out = [jnp.stack([kvl[i] for kvl in new_kv], axis=0) for i in range(2 * len(ATT_GROUPS))]
    else:
        kv_out = [_cache_token_major(c) for c in kv_run]
    return (x2.reshape(bsz, t_len, D_MODEL), jnp.stack(new_gla, 0), jnp.stack(new_pool, 0), kv_out,
            jnp.stack(new_conv, 0))


def kernel(x_prompt, x_sample, state_gla, state_pool, cache_k_w128, cache_v_w128, cache_k_w512, cache_v_w512,
           cache_k_w2048, cache_v_w2048, state_ffn_conv, norm1_g, norm2_g, w_in, gla_wa2, gla_ba, gla_norm_g,
           pool_w, pool_scale, w_oa, w_ob, w_oc, w_out, ffn_w_up, ffn_conv_w, ffn_conv_b, ffn_w_down,
           final_norm_g):
    weights = _prep_weights(w_in, gla_wa2, pool_w, w_oa, w_ob, w_oc, w_out, ffn_w_up, ffn_w_down)
    small = (norm1_g, norm2_g, gla_ba, gla_norm_g, pool_scale, ffn_conv_w, ffn_conv_b, final_norm_g)
    y_p, gla_p, pool_p, kv_p, conv_p = _run_group(
        x_prompt, None, weights, small, dict(in_tm=512, merge_tm=512, ffn_tm=1024, gla_hps=1))
    states = dict(gla=state_gla, pool=state_pool, conv=state_ffn_conv,
                  kv=[cache_k_w128, cache_v_w128, cache_k_w512, cache_v_w512, cache_k_w2048, cache_v_w2048])
    m_s = x_sample.shape[0] * x_sample.shape[1]
    y_s, gla_s, pool_s, kv_s, conv_s = _run_group(
        x_sample, states, weights, small, dict(in_tm=m_s, merge_tm=m_s, ffn_tm=m_s, gla_hps=GLA_HEADS))
    k128_p, v128_p, k512_p, v512_p, k2048_p, v2048_p = kv_p
    k128_s, v128_s, k512_s, v512_s, k2048_s, v2048_s = kv_s
    return (y_p, y_s, gla_p, gla_s, pool_p, pool_s, k128_p, k128_s, v128_p, v128_s,
            k512_p, k512_s, v512_p, v512_s, k2048_p, k2048_s, v2048_p, v2048_s, conv_p, conv_s)
```

```python
import functools

import jax
import jax.numpy as jnp
import numpy as np
from jax import lax
from jax.experimental import pallas as pl
from jax.experimental.pallas import tpu as pltpu

F32 = jnp.float32
BF16 = jnp.bfloat16

D_MODEL = 1024
DEPTH = 2
GLA_HEADS = 4
GLA_DK = 128
GLA_DV = 256
GLA_QK = GLA_HEADS * GLA_DK
GLA_VW = GLA_HEADS * GLA_DV
GLA_GATE_RANK = 16
GLA_GATE_NORM = 16.0
GLA_CHUNK = 64
GLA_SCAN_ROWS = 256
GLA_UNROLL = 16
POOL_WINDOWS = (2, 4, 8, 16)
POOL_GROUP = 256
POOL_STATE = 15
POOL_HALO = 16
ATT_GROUPS = ((128, 1), (512, 4), (2048, 16))
ATT_HG = 4
ATT_E = 64
ATT_GW = ATT_HG * ATT_E
ATT_SPAN = 128
ATT_UNROLL = 8
ATT_HEADS = 12
ATT_WIDTH = ATT_HEADS * ATT_E
D_FF = 2816
NORM_EPS = 1e-6
NEG_INF = -1e30

LANE = 128
SUBLANE = 8
BF16_SUBLANE = 16
VMEM_LIMIT = 56 * 1024 * 1024

_IN_SPLITS = (GLA_QK, GLA_QK, GLA_VW, GLA_VW, GLA_GATE_RANK, D_MODEL, ATT_WIDTH, ATT_WIDTH, ATT_WIDTH,
              D_MODEL, D_MODEL, D_MODEL)
_IN_OFF = [0] + [int(v) for v in np.cumsum(_IN_SPLITS)]
ZG_W = 2 * GLA_QK + 2 * GLA_VW
ZG_K, ZG_V, ZG_R = GLA_QK, 2 * GLA_QK, 2 * GLA_QK + GLA_VW
A_W = 3 * ATT_WIDTH
A_Q, A_K, A_V = 0, ATT_WIDTH, 2 * ATT_WIDTH
ZG_TN, U_TN, A_TN = 1024, 512, 768

_SLOPES = (2.0 ** (-8.0 * np.arange(1, ATT_HEADS + 1) / ATT_HEADS)).astype(np.float32)


def _params(sem):
    return pltpu.CompilerParams(dimension_semantics=sem, vmem_limit_bytes=VMEM_LIMIT)


def _resident(shape):
    return pl.BlockSpec(shape, lambda *_: (0,) * len(shape), pipeline_mode=pl.Buffered(1))


def _act_dtype(t_len):
    return BF16 if t_len % BF16_SUBLANE == 0 else F32


def _dot(a, b):
    return jnp.dot(a, b, preferred_element_type=F32)


def _dot_nt(a, b):
    return lax.dot_general(a, b, (((1,), (1,)), ((), ())), preferred_element_type=F32)


def _dot_tn(a, b):
    return lax.dot_general(a, b, (((0,), (0,)), ((), ())), preferred_element_type=F32)


def _rms(x, g):
    return x * lax.rsqrt(jnp.mean(x * x, axis=-1, keepdims=True) + NORM_EPS) * g


def _sigmoid(x):
    return 0.5 * jnp.tanh(0.5 * x) + 0.5


def _in_proj_kernel(x_ref, g_ref, wal_ref, *refs, tiles):
    nseg = len(tiles)
    w_refs, al_ref, out_refs, xn_ref = refs[:nseg], refs[nseg], refs[nseg + 1:2 * nseg + 1], refs[-1]
    xn_ref[...] = _rms(x_ref[...], g_ref[...]).astype(BF16)
    al_ref[...] = _dot(xn_ref[...], wal_ref[...]).astype(al_ref.dtype)
    for w_ref, o_ref, tn in zip(w_refs, out_refs, tiles):
        for c in range(w_ref.shape[1] // tn):
            cols = slice(c * tn, (c + 1) * tn)
            o_ref[:, cols] = _dot(xn_ref[...], w_ref[:, cols]).astype(o_ref.dtype)


def _in_proj(x2, g, wal, segs, tm, act):
    m = x2.shape[0]
    row = lambda w: pl.BlockSpec((tm, w), lambda i: (i, 0))
    return pl.pallas_call(
        functools.partial(_in_proj_kernel, tiles=tuple(tn for _, tn, _ in segs)),
        grid=(m // tm,),
        in_specs=[row(D_MODEL), _resident(g.shape), _resident(wal.shape)] + [_resident(w.shape) for w, _, _ in segs],
        out_specs=[row(LANE)] + [row(w.shape[1]) for w, _, _ in segs],
        out_shape=[jax.ShapeDtypeStruct((m, LANE), act)]
                  + [jax.ShapeDtypeStruct((m, w.shape[1]), dt) for w, _, dt in segs],
        scratch_shapes=[pltpu.VMEM((tm, D_MODEL), BF16)],
        compiler_params=_params(("parallel",)),
        name="in_proj",
    )(x2, g, wal, *[w for w, _, _ in segs])


def _gla_kernel(q_ref, k_ref, v_ref, r_ref, a_ref, wa_ref, ba_ref, gn_ref, s0_ref, o_ref, so_ref, *scratch,
                c, nc, hps):
    for hh in range(hps):
        dk = slice(hh * GLA_DK, (hh + 1) * GLA_DK)
        dv = slice(hh * GLA_DV, (hh + 1) * GLA_DV)
        _gla_head(q_ref.at[:, dk], k_ref.at[:, dk], v_ref.at[:, dv], r_ref.at[:, dv], a_ref, wa_ref.at[:, dk],
                  ba_ref.at[:, dk], gn_ref, s0_ref.at[hh], o_ref.at[:, dv], so_ref.at[hh], *scratch, c=c, nc=nc)


def _gla_head(q_ref, k_ref, v_ref, r_ref, a_ref, wa_ref, ba_ref, gn_ref, s0_ref, o_ref, so_ref,
              qe_ref, ke_ref, qi_ref, ks_ref, vb_ref, att_ref, dec_ref, oacc_ref, kv_ref, st_ref, *, c, nc):
    ce = max(c, BF16_SUBLANE)
    tp = nc * ce
    mid = (c - 1) // 2

    def load(ref):
        x = ref[...].astype(F32)
        if ce != c:
            x = jnp.concatenate([x, jnp.zeros((ce - c, x.shape[1]), F32)], axis=0)
        return x

    z = _dot(load(a_ref).astype(BF16), wa_ref[...]) + ba_ref[...]
    la = (jnp.minimum(z, 0.0) - jnp.log(1.0 + jnp.exp(-jnp.abs(z)))) * (1.0 / GLA_GATE_NORM)
    if ce != c:
        la = jnp.where(lax.broadcasted_iota(jnp.int32, (tp, GLA_DK), 0) < c, la, 0.0)
    slab = min(tp, GLA_SCAN_ROWS)
    ri = lax.broadcasted_iota(jnp.int32, (slab, slab), 0)
    ci = lax.broadcasted_iota(jnp.int32, (slab, slab), 1)
    tri = jnp.where((ri >= ci) & ((ri & -ce) == (ci & -ce)), 1.0, 0.0).astype(BF16)
    la_hi = la.astype(BF16)
    la_lo = (la - la_hi.astype(F32)).astype(BF16)
    b = jnp.concatenate(
        [_dot(tri, la_hi[i * slab:(i + 1) * slab]) + _dot(tri, la_lo[i * slab:(i + 1) * slab])
         for i in range(tp // slab)], axis=0)
    b3 = b.reshape(nc, ce, GLA_DK)
    b_mid = b3[:, mid:mid + 1, :]
    b_last = b3[:, c - 1:c, :]
    q3 = (load(q_ref) * (GLA_DK ** -0.5)).reshape(nc, ce, GLA_DK)
    k3 = load(k_ref).reshape(nc, ce, GLA_DK)
    flat = lambda x: x.reshape(tp, GLA_DK).astype(BF16)
    qe = q3 * jnp.exp(b3 - b_mid)
    ke = k3 * jnp.exp(b_mid - b3)
    qe_ref[...] = flat(qe)
    ke_ref[...] = flat(ke)
    qi_ref[...] = flat(qe * jnp.exp(b_mid))
    ks_ref[...] = flat(ke * jnp.exp(b_last - b_mid))
    dec_ref[...] = jnp.exp(b_last)
    vb_ref[...] = load(v_ref).astype(BF16)
    causal = (lax.broadcasted_iota(jnp.int32, (ce, ce), 0) >= lax.broadcasted_iota(jnp.int32, (ce, ce), 1))
    rows = lambda n: pl.ds(pl.multiple_of(n * ce, ce), ce)

    def scores(n, carry):
        sl = rows(n)
        att_ref[sl, :] = jnp.where(causal, _dot_nt(qe_ref[sl, :], ke_ref[sl, :]), 0.0).astype(BF16)
        return carry

    def within(n, carry):
        sl = rows(n)
        oacc_ref[sl, :] = _dot(att_ref[sl, :], vb_ref[sl, :])
        return carry

    def increments(n, carry):
        sl = rows(n)
        kv_ref[n] = _dot_tn(vb_ref[sl, :], ks_ref[sl, :])
        return carry

    def across(n, carry):
        sl = rows(n)
        st = st_ref[...]
        oacc_ref[sl, :] += _dot_nt(qi_ref[sl, :], st.astype(BF16))
        st_ref[...] = st * dec_ref[n] + kv_ref[n]
        return carry

    unroll = min(nc, GLA_UNROLL)
    lax.fori_loop(0, nc, scores, 0, unroll=unroll)
    lax.fori_loop(0, nc, within, 0, unroll=unroll)
    lax.fori_loop(0, nc, increments, 0, unroll=unroll)
    st_ref[...] = s0_ref[...].T
    lax.fori_loop(0, nc, across, 0, unroll=unroll)
    r = load(r_ref)
    res = _rms(oacc_ref[...], gn_ref[...]) * (r * _sigmoid(r))
    o_ref[...] = res[:nc * c if ce == c else c].astype(o_ref.dtype)
    so_ref[...] = st_ref[...].T


def _gla(zg3, al3, wa2p, ba, gn, s0, layer, t_len, hps):
    bsz = zg3.shape[0]
    c = GLA_CHUNK if t_len % GLA_CHUNK == 0 else t_len
    nc = t_len // c
    ce = max(c, BF16_SUBLANE)
    assert ce == c or nc == 1
    tp = nc * ce
    kern = functools.partial(_gla_kernel, c=c, nc=nc, hps=hps)
    zspec = lambda w, off: pl.BlockSpec((None, t_len, hps * w), lambda b, h: (b, 0, off // (hps * w) + h))
    return pl.pallas_call(
        kern,
        grid=(bsz, GLA_HEADS // hps),
        in_specs=[
            zspec(GLA_DK, 0), zspec(GLA_DK, ZG_K), zspec(GLA_DV, ZG_V), zspec(GLA_DV, ZG_R),
            pl.BlockSpec((None, t_len, LANE), lambda b, h: (b, 0, 0)),
            pl.BlockSpec((LANE, hps * GLA_DK), lambda b, h: (0, h)),
            pl.BlockSpec((1, hps * GLA_DK), lambda b, h: (0, h)),
            pl.BlockSpec((1, GLA_DV), lambda b, h: (0, 0)),
            pl.BlockSpec((None, None, hps, GLA_DK, GLA_DV), lambda b, h: (layer, b, h, 0, 0)),
        ],
        out_specs=[
            pl.BlockSpec((None, t_len, hps * GLA_DV), lambda b, h: (b, 0, h)),
            pl.BlockSpec((None, hps, GLA_DK, GLA_DV), lambda b, h: (b, h, 0, 0)),
        ],
        out_shape=[
            jax.ShapeDtypeStruct((bsz, t_len, GLA_VW), zg3.dtype),
            jax.ShapeDtypeStruct((bsz, GLA_HEADS, GLA_DK, GLA_DV), F32),
        ],
        scratch_shapes=[pltpu.VMEM((tp, GLA_DK), BF16)] * 4 + [
            pltpu.VMEM((tp, GLA_DV), BF16),
            pltpu.VMEM((tp, ce), BF16),
            pltpu.VMEM((nc, 1, GLA_DK), F32),
            pltpu.VMEM((tp, GLA_DV), F32),
            pltpu.VMEM((nc, GLA_DV, GLA_DK), F32),
            pltpu.VMEM((GLA_DV, GLA_DK), F32),
        ],
        compiler_params=_params(("parallel", "arbitrary")),
        name="gla",
    )(zg3, zg3, zg3, zg3, al3, wa2p, ba, gn, s0)


def _pool_mix(ext, t_abs, pw_ref, ps_ref):
    outs = []
    for gi, w in enumerate(POOL_WINDOWS):
        cols = slice(gi * POOL_GROUP, (gi + 1) * POOL_GROUP)
        x = ext[:, cols]
        acc = x
        s = 1
        while s < w:
            acc = acc + pltpu.roll(acc, s, axis=0)
            s *= 2
        cnt = jnp.minimum(t_abs + 1, w).astype(F32)
        pooled = acc[POOL_HALO:] / cnt - x[POOL_HALO:]
        outs.append(_dot(pooled.astype(BF16), pw_ref[gi]) * ps_ref[:, cols])
    return outs


def _pool_kernel(u_ref, prev_ref, pw_ref, ps_ref, p_ref, pn_ref, *, t_len, n_prev):
    ext = jnp.concatenate([prev_ref[...], u_ref[...]], axis=0)
    t_abs = lax.broadcasted_iota(jnp.int32, (t_len, 1), 0) + n_prev
    for gi, mixed in enumerate(_pool_mix(ext, t_abs, pw_ref, ps_ref)):
        p_ref[:, gi * POOL_GROUP:(gi + 1) * POOL_GROUP] = mixed.astype(p_ref.dtype)
    keep = max(0, POOL_STATE - t_len)
    if keep:
        pn_ref[0:keep, :] = prev_ref[POOL_HALO - keep:POOL_HALO, :]
    pn_ref[keep:POOL_STATE, :] = u_ref[t_len - (POOL_STATE - keep):t_len, :]


def _pool(u3, prev, pw, ps, t_len, n_prev):
    bsz = u3.shape[0]
    kern = functools.partial(_pool_kernel, t_len=t_len, n_prev=n_prev)
    return pl.pallas_call(
        kern,
        grid=(bsz,),
        in_specs=[
            pl.BlockSpec((None, t_len, D_MODEL), lambda b: (b, 0, 0)),
            pl.BlockSpec((None, POOL_HALO, D_MODEL), lambda b: (b, 0, 0)),
            pl.BlockSpec((4, POOL_GROUP, POOL_GROUP), lambda b: (0, 0, 0)),
            pl.BlockSpec((1, D_MODEL), lambda b: (0, 0)),
        ],
        out_specs=[
            pl.BlockSpec((None, t_len, D_MODEL), lambda b: (b, 0, 0)),
            pl.BlockSpec((None, POOL_STATE, D_MODEL), lambda b: (b, 0, 0)),
        ],
        out_shape=[
            jax.ShapeDtypeStruct((bsz, t_len, D_MODEL), _act_dtype(t_len)),
            jax.ShapeDtypeStruct((bsz, POOL_STATE, D_MODEL), F32),
        ],
        compiler_params=_params(("parallel",)),
        name="pool",
    )(u3, prev, pw, ps)


def _attn_prompt_kernel(q_ref, k_ref, v_ref, o_ref, l_ref, kc_ref, vc_ref,
                        qm_ref, ks_ref, vs_ref, os_ref, ls_ref, bias_ref, s_ref, m_ref, stage_ref,
                        *, t_len, keep, dil, slopes):
    n = t_len // dil
    nb = n // ATT_SPAN
    hp = pl.program_id(1)
    kc_ref[...] = k_ref[t_len - keep:t_len, :].astype(F32)
    vc_ref[...] = v_ref[t_len - keep:t_len, :].astype(F32)
    via_swap = dil % SUBLANE == 0

    step = BF16_SUBLANE if via_swap else n
    slabs = [(slice(i * step, (i + 1) * step), slice(i * step * dil, (i + 1) * step * dil))
             for i in range(n // step)]

    def to_res(ref, tok):
        if dil == 1:
            return ref[...].astype(F32)[None]
        if via_swap:
            return jnp.swapaxes(ref[tok, :].astype(F32).reshape(step, dil, LANE), 0, 1)
        stage_ref[...] = ref[...].astype(F32)
        return jnp.stack([stage_ref[pl.ds(r, n, stride=dil), :] for r in range(dil)], axis=0)

    lane3 = lax.broadcasted_iota(jnp.int32, (dil, step, LANE), 2)
    for res, tok in slabs:
        q3 = to_res(q_ref, tok) * (ATT_E ** -0.5)
        qm_ref[0, :, res, :] = jnp.where(lane3 < ATT_E, q3, 0.0).astype(BF16)
        qm_ref[1, :, res, :] = jnp.where(lane3 >= ATT_E, q3, 0.0).astype(BF16)
        ks_ref[:, res, :] = to_res(k_ref, tok).astype(BF16)
        vs_ref[:, res, 0:LANE] = to_res(v_ref, tok).astype(BF16)
    vs_ref[:, :, LANE:2 * LANE] = jnp.ones((dil, n, LANE), BF16)

    nk = ATT_SPAN if nb == 1 else 2 * ATT_SPAN
    a_idx = lax.broadcasted_iota(jnp.int32, (ATT_SPAN, nk), 0)
    c_idx = lax.broadcasted_iota(jnp.int32, (ATT_SPAN, nk), 1)
    for var in range(1 if nb == 1 else 2):
        j = a_idx - c_idx + var * ATT_SPAN
        valid = (j >= 0) & (j <= ATT_SPAN)
        for h in range(2):
            scale = jnp.where(hp == 0, -slopes[h] * dil, -slopes[2 + h] * dil)
            bias_ref[h, var] = jnp.where(valid, j.astype(F32) * scale, NEG_INF)

    lane = lax.broadcasted_iota(jnp.int32, (ATT_SPAN, LANE), 1)
    hm = [lane < ATT_E, lane >= ATT_E]

    def slices(t):
        r, qi = t // nb, t % nb
        qsl = pl.ds(pl.multiple_of(qi * ATT_SPAN, ATT_SPAN), ATT_SPAN)
        if nb > 1:
            return r, qsl, pl.ds(pl.multiple_of(jnp.maximum(qi - 1, 0) * ATT_SPAN, ATT_SPAN), nk), jnp.minimum(qi, 1)
        return r, qsl, qsl, 0

    def scores(t, carry):
        r, qsl, ksl, var = slices(t)
        kk = ks_ref[r, ksl, :]
        for h in range(2):
            s = _dot_nt(qm_ref[h, r, qsl, :], kk) + bias_ref[h, var]
            s_ref[t, h] = s
            m_ref[t, h] = jnp.max(s, axis=-1, keepdims=True)
        return carry

    def outputs(t, carry):
        r, qsl, ksl, _ = slices(t)
        vv = vs_ref[r, ksl, :]
        o_acc = jnp.zeros((ATT_SPAN, LANE), F32)
        l_acc = jnp.zeros((ATT_SPAN, LANE), F32)
        for h in range(2):
            m = m_ref[t, h]
            p = jnp.exp(s_ref[t, h] - m).astype(BF16)
            od = _dot(p, vv)
            den = od[:, LANE:]
            o_acc = jnp.where(hm[h], od[:, :LANE] / den, o_acc)
            l_acc = jnp.where(hm[h], m + jnp.log(den), l_acc)
        os_ref[r, qsl, :] = o_acc
        ls_ref[r, qsl, :] = l_acc
        return carry

    lax.fori_loop(0, dil * nb, scores, 0, unroll=ATT_UNROLL)
    lax.fori_loop(0, dil * nb, outputs, 0, unroll=ATT_UNROLL)
    for res_ref, out_ref in ((os_ref, o_ref), (ls_ref, l_ref)):
        if dil == 1:
            out_ref[...] = res_ref[0]
        elif via_swap:
            for res, tok in slabs:
                out_ref[tok, :] = jnp.swapaxes(res_ref[:, res, :], 0, 1).reshape(step * dil, LANE)
        else:
            for r in range(dil):
                out_ref[pl.ds(r, n, stride=dil), :] = res_ref[r]


def _attn_prompt(a3, gi):
    bsz, t_len, _ = a3.shape
    win, dil = ATT_GROUPS[gi]
    keep = min(win, t_len)
    kern = functools.partial(_attn_prompt_kernel, t_len=t_len, keep=keep, dil=dil,
                             slopes=[float(s) for s in _SLOPES[gi * ATT_HG:(gi + 1) * ATT_HG]])
    uspec = lambda off: pl.BlockSpec((None, t_len, LANE), lambda b, hp: (b, 0, (off + gi * ATT_GW) // LANE + hp))
    ospec = lambda rows: pl.BlockSpec((None, rows, LANE), lambda b, hp: (b, 0, hp))
    n = t_len // dil
    nk = ATT_SPAN if n == ATT_SPAN else 2 * ATT_SPAN
    nblk = t_len // ATT_SPAN
    scratch = [pltpu.VMEM((2, dil, n, LANE), BF16), pltpu.VMEM((dil, n, LANE), BF16),
               pltpu.VMEM((dil, n, 2 * LANE), BF16), pltpu.VMEM((dil, n, LANE), F32),
               pltpu.VMEM((dil, n, LANE), F32), pltpu.VMEM((2, 2, ATT_SPAN, nk), F32),
               pltpu.VMEM((nblk, 2, ATT_SPAN, nk), F32), pltpu.VMEM((nblk, 2, ATT_SPAN, 1), F32),
               pltpu.VMEM((t_len, LANE), F32)]
    o, l, kc, vc = pl.pallas_call(
        kern,
        grid=(bsz, ATT_GW // LANE),
        in_specs=[uspec(A_Q), uspec(A_K), uspec(A_V)],
        out_specs=[ospec(t_len), ospec(t_len), ospec(keep), ospec(keep)],
        out_shape=[
            jax.ShapeDtypeStruct((bsz, t_len, ATT_GW), F32),
            jax.ShapeDtypeStruct((bsz, t_len, ATT_GW), F32),
            jax.ShapeDtypeStruct((bsz, keep, ATT_GW), F32),
            jax.ShapeDtypeStruct((bsz, keep, ATT_GW), F32),
        ],
        scratch_shapes=scratch,
        compiler_params=_params(("parallel", "parallel")),
        name=f"attn_prompt_g{gi}",
    )(a3, a3, a3)
    m = bsz * t_len
    return (o.reshape(m, ATT_GW), l.reshape(m, ATT_GW),
            kc.reshape(bsz, keep, ATT_HG, ATT_E), vc.reshape(bsz, keep, ATT_HG, ATT_E))


_KX_PAD = LANE


def _head_masks(rows):
    lane = lax.broadcasted_iota(jnp.int32, (rows, ATT_GW), 1)
    return [(lane >= h * ATT_E) & (lane < (h + 1) * ATT_E) for h in range(ATT_HG)]


def _attn_sample_kernel(*refs, t_len, first, layer_slopes):
    ng = len(ATT_GROUPS)
    qkv = refs[0:3 * ng]
    caches = refs[3 * ng:5 * ng]
    outs = refs[5 * ng:] if first else refs[7 * ng:]
    o_refs, l_refs, cache_out = outs[0:ng], outs[ng:2 * ng], outs[2 * ng:4 * ng]
    kx_ref, vx_ref = outs[4 * ng:]

    def shift_in(c_ref, new_t):
        ext = jnp.concatenate([c_ref[...], new_t], axis=1)
        return ext, pltpu.roll(ext, ext.shape[1] - t_len, axis=1)

    def attend():
        hm = _head_masks(t_len)
        rows = ATT_HG * t_len
        zrows = jnp.zeros((_KX_PAD - t_len, ATT_GW), F32)
        for gi, (win, dil) in enumerate(ATT_GROUPS):
            q_ref, kn_ref, vn_ref = qkv[3 * gi:3 * gi + 3]
            wb = caches[2 * gi].shape[1]
            nk = wb + _KX_PAD
            for c_ref, n_ref, x_ref, co_ref in ((caches[2 * gi], kn_ref, kx_ref, cache_out[2 * gi]),
                                                (caches[2 * gi + 1], vn_ref, vx_ref, cache_out[2 * gi + 1])):
                new_t = jnp.concatenate([n_ref[...], zrows], axis=0).T
                ext, moved = shift_in(c_ref, new_t)
                x_ref[:, 0:nk] = ext.astype(BF16)
                co_ref[...] = moved[:, 0:wb] if first else moved[:, wb - LANE:wb]
            q = q_ref[...] * (ATT_E ** -0.5)
            qst = jnp.concatenate([jnp.where(hm[h], q, 0.0) for h in range(ATT_HG)], axis=0).astype(BF16)
            s = _dot(qst, kx_ref[:, 0:nk])
            ri = lax.broadcasted_iota(jnp.int32, (rows, nk), 0)
            ci = lax.broadcasted_iota(jnp.int32, (rows, nk), 1)
            dist = wb + (ri % t_len) - ci
            valid = (dist >= 0) & (dist <= win) & ((dist & (dil - 1)) == 0)
            hrow = lax.broadcasted_iota(jnp.int32, (rows, 1), 0) // t_len
            slope = jnp.zeros((rows, 1), F32)
            for h in range(ATT_HG):
                slope = jnp.where(hrow == h, float(layer_slopes[gi * ATT_HG + h]), slope)
            s = jnp.where(valid, s - slope * dist.astype(F32), NEG_INF)
            m = jnp.max(s, axis=-1, keepdims=True)
            p = jnp.exp(s - m)
            den = jnp.sum(p, axis=-1, keepdims=True)
            ost = _dot_nt(p.astype(BF16), vx_ref[:, 0:nk]) / den
            lst = m + jnp.log(den)
            o = jnp.zeros((t_len, ATT_GW), F32)
            l = jnp.zeros((t_len, ATT_GW), F32)
            for h in range(ATT_HG):
                o = jnp.where(hm[h], ost[h * t_len:(h + 1) * t_len], o)
                l = jnp.where(hm[h], lst[h * t_len:(h + 1) * t_len], l)
            o_refs[gi][...] = o
            l_refs[gi][...] = l

    if not first:
        attend()
        return
    pl.when(pl.program_id(1) == 0)(attend)

    @pl.when(pl.program_id(1) > 0)
    def _():
        for c_ref, co_ref in zip(caches, cache_out):
            _, moved = shift_in(c_ref, jnp.zeros((ATT_GW, _KX_PAD), F32))
            co_ref[...] = moved[:, 0:c_ref.shape[1]]


def _attn_sample(a3, caches_t, layer, cache_prev):
    bsz, t_len, _ = a3.shape
    ng = len(ATT_GROUPS)
    first = cache_prev is None
    kern = functools.partial(_attn_sample_kernel, t_len=t_len, first=first,
                             layer_slopes=[float(s) for s in _SLOPES])
    in_specs, args = [], []
    for gi in range(ng):
        for off in (A_Q, A_K, A_V):
            in_specs.append(pl.BlockSpec((None, t_len, ATT_GW), lambda b, *_, o=off // ATT_GW + gi: (b, 0, o)))
            args.append(a3)
    small = pl.BlockSpec((None, t_len, ATT_GW), lambda b, *_: (b, 0, 0))
    out_specs = [small] * (2 * ng)
    out_shape = [jax.ShapeDtypeStruct((bsz, t_len, ATT_GW), F32)] * (2 * ng)
    wbs = [c.shape[3] for c in caches_t]
    for c, wb in zip(caches_t, wbs):
        whole = (None, None, ATT_GW, wb)
        if first:
            in_specs.append(pl.BlockSpec(whole, lambda b, l: (l, b, 0, 0)))
            out_specs.append(pl.BlockSpec(whole, lambda b, l: (l, b, 0, 0)))
        else:
            in_specs.append(pl.BlockSpec(whole, lambda b: (layer, b, 0, 0)))
            out_specs.append(pl.BlockSpec((None, None, ATT_GW, LANE), lambda b, t=wb // LANE - 1: (layer, b, 0, t)))
        args.append(c)
        out_shape.append(jax.ShapeDtypeStruct(c.shape, F32))
    aliases = {}
    if not first:
        for i, c in enumerate(cache_prev):
            in_specs.append(pl.BlockSpec(memory_space=pl.ANY))
            args.append(c)
            aliases[5 * ng + i] = 2 * ng + i
    res = pl.pallas_call(
        kern,
        grid=(bsz, DEPTH) if first else (bsz,),
        in_specs=in_specs,
        out_specs=out_specs,
        out_shape=out_shape,
        scratch_shapes=[pltpu.VMEM((ATT_GW, max(wbs) + _KX_PAD), BF16)] * 2,
        input_output_aliases=aliases,
        compiler_params=_params(("parallel", "arbitrary") if first else ("parallel",)),
        name="attn_sample",
    )(*args)
    m = bsz * t_len
    o = [r.reshape(m, ATT_GW) for r in res[0:ng]]
    l = [r.reshape(m, ATT_GW) for r in res[ng:2 * ng]]
    return o, l, list(res[2 * ng:])


def _merge_kernel(*refs, tm, pool_t_len):
    it = iter(refs)
    x_ref, og_ref = next(it), next(it)
    if pool_t_len is None:
        p_ref = next(it)
    else:
        u_ref, halo_ref = next(it), next(it)
    o0_ref, o1_ref, o2_ref, l0_ref, l1_ref, l2_ref, g_ref, wg_ref, woa_ref, wob_ref, woc_ref, wout_ref = (
        next(it) for _ in range(12))
    if pool_t_len is not None:
        pw_ref, ps_ref = next(it), next(it)
    h_ref = next(it)
    x = x_ref[...]
    xn = _rms(x, g_ref[...]).astype(BF16)

    def gate(i):
        return _sigmoid(_dot(xn, wg_ref[:, i * D_MODEL:(i + 1) * D_MODEL]))

    merged = gate(0) * _dot(og_ref[...].astype(BF16), woa_ref[...])
    if pool_t_len is None:
        yb = _dot(p_ref[...].astype(BF16), wob_ref[...])
    else:
        t0 = (pl.program_id(0) * tm) % pool_t_len
        halo = jnp.where(t0 == 0, 0.0, halo_ref[...])
        ext = jnp.concatenate([halo, u_ref[...]], axis=0)
        t_abs = lax.broadcasted_iota(jnp.int32, (tm, 1), 0) + t0
        yb = jnp.zeros_like(x)
        for gi, mixed in enumerate(_pool_mix(ext, t_abs, pw_ref, ps_ref)):
            yb = yb + _dot(mixed.astype(BF16), wob_ref[gi * POOL_GROUP:(gi + 1) * POOL_GROUP, :])
    merged = merged + gate(1) * yb
    ls = [l0_ref[...], l1_ref[...], l2_ref[...]]
    m = jnp.maximum(jnp.maximum(ls[0], ls[1]), ls[2])
    es = [jnp.exp(l - m) for l in ls]
    inv = 1.0 / (es[0] + es[1] + es[2])
    yc = jnp.zeros_like(x)
    for gi, o_ref in enumerate((o0_ref, o1_ref, o2_ref)):
        oc = (o_ref[...] * (es[gi] * inv)).astype(BF16)
        yc = yc + _dot(oc, woc_ref[gi * ATT_GW:(gi + 1) * ATT_GW, :])
    merged = merged + gate(2) * yc
    h_ref[...] = x + _dot(merged.astype(BF16), wout_ref[...])


def _merge(x2, og, p, o, l, g, wg, woa, wob, woc, wout, tm, pool=None):
    m = x2.shape[0]
    row = lambda w: pl.BlockSpec((tm, w), lambda i: (i, 0))
    in_specs = [row(D_MODEL), row(GLA_VW), row(D_MODEL)]
    args = [x2, og, p]
    if pool is not None:
        hb = tm // POOL_HALO
        in_specs.append(pl.BlockSpec((POOL_HALO, D_MODEL), lambda i: (jnp.maximum(i * hb - 1, 0), 0)))
        args.append(p)
    weights = [g, wg, woa, wob, woc, wout] + ([] if pool is None else list(pool[:2]))
    return pl.pallas_call(
        functools.partial(_merge_kernel, tm=tm, pool_t_len=None if pool is None else pool[2]),
        grid=(m // tm,),
        in_specs=in_specs + [row(ATT_GW)] * 6 + [_resident(a.shape) for a in weights],
        out_specs=row(D_MODEL),
        out_shape=jax.ShapeDtypeStruct((m, D_MODEL), F32),
        compiler_params=_params(("parallel",)),
        name="merge",
    )(*args, *o, *l, *weights)


FFN_TF = 256
FFN_HALO = 16


def _ffn_kernel(*refs, t_len, tm, use_halo, has_prev, final):
    it = iter(refs)
    h_ref = next(it)
    halo_ref = next(it) if use_halo else None
    g_ref, wup_ref, cw_ref, cb_ref, wd_ref, fg_ref = (next(it) for _ in range(6))
    c1_ref, c2_ref = (next(it), next(it)) if has_prev else (None, None)
    y_ref, cv_ref = next(it), next(it)
    hn_ref, hh_ref = next(it), next(it)
    a_scr = None if use_halo else next(it)
    ho = FFN_HALO if use_halo else 0
    if use_halo:
        hn_ref[0:ho, :] = _rms(halo_ref[...], g_ref[...]).astype(BF16)
    hn_ref[ho:ho + tm, :] = _rms(h_ref[...], g_ref[...]).astype(BF16)
    t_loc = (lax.broadcasted_iota(jnp.int32, (tm, 1), 0) + pl.program_id(0) * tm) % t_len
    for jt in range(D_FF // FFN_TF):
        cols = slice(jt * FFN_TF, (jt + 1) * FFN_TF)
        a_ext = _dot(hn_ref[...], wup_ref[:, cols])
        a = a_ext[ho:]
        bg = _dot(hn_ref[ho:ho + tm, :], wup_ref[:, D_FF + jt * FFN_TF:D_FF + (jt + 1) * FFN_TF])
        if use_halo:
            a1, a2 = a_ext[ho - 1:ho - 1 + tm], a_ext[ho - 2:ho - 2 + tm]
        else:
            a1, a2 = pltpu.roll(a, 1, axis=0), pltpu.roll(a, 2, axis=0)
        a1 = jnp.where(t_loc >= 1, a1, c1_ref[:, cols] if has_prev else 0.0)
        a2 = jnp.where(t_loc >= 2, a2, c2_ref[:, cols] if has_prev else 0.0)
        y = cb_ref[:, cols] + cw_ref[0:1, cols] * a2 + cw_ref[1:2, cols] * a1 + cw_ref[2:3, cols] * a
        hh_ref[:, cols] = ((y * _sigmoid(y)) * bg).astype(BF16)
        if use_halo:
            cv_ref[:, cols] = a[tm - 2:tm]
        else:
            nseq = tm // t_len
            for kk in range(FFN_TF // LANE):
                lc = slice(kk * LANE, (kk + 1) * LANE)
                oc = slice(jt * FFN_TF + kk * LANE, jt * FFN_TF + (kk + 1) * LANE)
                a_scr[kk] = a[:, lc]
                cv_ref[0, :, oc] = a_scr[kk, pl.ds(t_len - 2, nseq, stride=t_len), :]
                cv_ref[1, :, oc] = a_scr[kk, pl.ds(t_len - 1, nseq, stride=t_len), :]
    out = h_ref[...] + _dot(hh_ref[...], wd_ref[...])
    if final:
        out = _rms(out, fg_ref[...])
    y_ref[...] = out


def _ffn(h2, g, wup, cw, cb, wd, fg, prev, t_len, tm, final):
    m = h2.shape[0]
    bsz = m // t_len
    use_halo = tm % t_len != 0
    has_prev = prev is not None
    kern = functools.partial(_ffn_kernel, t_len=t_len, tm=tm, use_halo=use_halo, has_prev=has_prev, final=final)
    in_specs = [pl.BlockSpec((tm, D_MODEL), lambda i: (i, 0))]
    args = [h2]
    if use_halo:
        hb = tm // FFN_HALO
        in_specs.append(pl.BlockSpec((FFN_HALO, D_MODEL), lambda i: (jnp.maximum(i * hb - 1, 0), 0)))
        args.append(h2)
    in_specs += [_resident(a.shape) for a in (g, wup, cw, cb, wd, fg)]
    args += [g, wup, cw, cb, wd, fg]
    if has_prev:
        z = jnp.zeros((bsz, t_len - 2, D_FF), F32)
        c1 = jnp.concatenate([prev[:, 1:2], jnp.zeros((bsz, 1, D_FF), F32), z], axis=1).reshape(m, D_FF)
        c2 = jnp.concatenate([prev, z], axis=1).reshape(m, D_FF)
        in_specs += [pl.BlockSpec((tm, D_FF), lambda i: (i, 0))] * 2
        args += [c1, c2]
    scratch = [pltpu.VMEM(((FFN_HALO if use_halo else 0) + tm, D_MODEL), BF16), pltpu.VMEM((tm, D_FF), BF16)]
    if use_halo:
        cv_spec = pl.BlockSpec((None, 2, D_FF), lambda i: (i, 0, 0))
        cv_shape = jax.ShapeDtypeStruct((m // tm, 2, D_FF), F32)
    else:
        nseq = tm // t_len
        cv_spec = pl.BlockSpec((2, nseq, D_FF), lambda i: (0, i, 0))
        cv_shape = jax.ShapeDtypeStruct((2, bsz, D_FF), F32)
        scratch.append(pltpu.VMEM((FFN_TF // LANE, tm, LANE), F32))
    y, cv = pl.pallas_call(
        kern,
        grid=(m // tm,),
        in_specs=in_specs,
        out_specs=[pl.BlockSpec((tm, D_MODEL), lambda i: (i, 0)), cv_spec],
        out_shape=[jax.ShapeDtypeStruct((m, D_MODEL), F32), cv_shape],
        scratch_shapes=scratch,
        compiler_params=_params(("parallel",)),
        name="ffn",
    )(*args)
    if use_halo:
        per = t_len // tm
        cv = cv[per - 1::per]
    else:
        cv = jnp.swapaxes(cv, 0, 1)
    return y, cv


def _prep_weights(w_in, gla_wa2, pool_w, w_oa, w_ob, w_oc, w_out, ffn_w_up, ffn_w_down):
    o = _IN_OFF
    cast = lambda a: a.astype(BF16)
    w1 = cast(w_in[:, :, o[0]:o[4]])
    wal = cast(jnp.pad(w_in[:, :, o[4]:o[5]], ((0, 0), (0, 0), (0, LANE - GLA_GATE_RANK))))
    wu = cast(w_in[:, :, o[5]:o[6]])
    wa = cast(w_in[:, :, o[6]:o[9]])
    wg = cast(w_in[:, :, o[9]:o[12]])
    wa2p = cast(jnp.pad(gla_wa2, ((0, 0), (0, LANE - GLA_GATE_RANK), (0, 0))))
    return (w1, wal, wu, wa, wg, wa2p, cast(pool_w), cast(w_oa), cast(w_ob), cast(w_oc), cast(w_out),
            cast(ffn_w_up), cast(ffn_w_down))


def _cache_channel_major(c):
    d, b, w = c.shape[:3]
    return jnp.transpose(c, (0, 1, 3, 4, 2)).reshape(d, b, ATT_GW, w)


def _cache_token_major(ct):
    d, b, _, w = ct.shape
    return jnp.transpose(ct.reshape(d, b, ATT_HG, ATT_E, w), (0, 1, 4, 2, 3))


def _run_group(x, states, weights, small, tiles):
    bsz, t_len, _ = x.shape
    m = bsz * t_len
    prompt = states is None
    act = _act_dtype(t_len)
    w1, wal, wu, wa, wg, wa2p, pw, woa, wob, woc, wout, wup, wdn = weights
    norm1_g, norm2_g, gla_ba, gla_norm_g, pool_scale, conv_w, conv_b, final_g = small
    x2 = x.reshape(m, D_MODEL)
    new_gla, new_pool, new_kv, new_conv = [], [], [], []
    kv_t = None if prompt else [_cache_channel_major(c) for c in states["kv"]]
    kv_run = None
    for l in range(DEPTH):
        al, zg, u, a = _in_proj(x2, norm1_g[l][None], wal[l],
                                [(w1[l], ZG_TN, act), (wu[l], U_TN, F32), (wa[l], A_TN, act)], tiles["in_tm"], act)
        zg3 = zg.reshape(bsz, t_len, ZG_W)
        al3 = al.reshape(bsz, t_len, LANE)
        u3 = u.reshape(bsz, t_len, D_MODEL)
        a3 = a.reshape(bsz, t_len, A_W)
        if prompt:
            s0, s0_layer = jnp.zeros((1, bsz, GLA_HEADS, GLA_DK, GLA_DV), F32), 0
            prev = jnp.zeros((bsz, POOL_HALO, D_MODEL), F32)
            n_prev = 0
        else:
            s0, s0_layer = states["gla"], l
            prev = jnp.pad(states["pool"][l], ((0, 0), (POOL_HALO - POOL_STATE, 0), (0, 0)))
            n_prev = POOL_STATE
        og, g_new = _gla(zg3, al3, wa2p[l], gla_ba[l][None], gla_norm_g[l][None], s0, s0_layer, t_len,
                         tiles["gla_hps"])
        if prompt:
            pb, p_new, fused_pool = u, u3[:, t_len - POOL_STATE:], (pw[l], pool_scale[l][None], t_len)
        else:
            pb, p_new = _pool(u3, prev, pw[l], pool_scale[l][None], t_len, n_prev)
            fused_pool = None
        if prompt:
            o, lse, kv = [], [], []
            for gi in range(len(ATT_GROUPS)):
                og_i, l_i, kc, vc = _attn_prompt(a3, gi)
                o.append(og_i)
                lse.append(l_i)
                kv += [kc, vc]
        else:
            o, lse, kv_run = _attn_sample(a3, kv_t, l, kv_run)
            kv = None
        h2 = _merge(x2, og.reshape(m, GLA_VW), pb.reshape(m, D_MODEL), o, lse, norm1_g[l][None], wg[l],
                    woa[l], wob[l], woc[l], wout[l], tiles["merge_tm"], fused_pool)
        x2, c_new = _ffn(h2, norm2_g[l][None], wup[l], conv_w[l], conv_b[l][None], wdn[l], final_g[None],
                         None if prompt else states["conv"][l], t_len, tiles["ffn_tm"], l == DEPTH - 1)
        new_gla.append(g_new)
        new_pool.append(p_new)
        new_kv.append(kv)
        new_conv.append(c_new)
    if prompt:
        kv_out = [jnp.stack([kvl[i] for kvl in new_kv], axis=0) for i in range(2 * len(ATT_GROUPS))]
    else:
        kv_out = [_cache_token_major(c) for c in kv_run]
    return (x2.reshape(bsz, t_len, D_MODEL), jnp.stack(new_gla, 0), jnp.stack(new_pool, 0), kv_out,
            jnp.stack(new_conv, 0))


def kernel(x_prompt, x_sample, state_gla, state_pool, cache_k_w128, cache_v_w128, cache_k_w512, cache_v_w512,
           cache_k_w2048, cache_v_w2048, state_ffn_conv, norm1_g, norm2_g, w_in, gla_wa2, gla_ba, gla_norm_g,
           pool_w, pool_scale, w_oa, w_ob, w_oc, w_out, ffn_w_up, ffn_conv_w, ffn_conv_b, ffn_w_down,
           final_norm_g):
    weights = _prep_weights(w_in, gla_wa2, pool_w, w_oa, w_ob, w_oc, w_out, ffn_w_up, ffn_w_down)
    small = (norm1_g, norm2_g, gla_ba, gla_norm_g, pool_scale, ffn_conv_w, ffn_conv_b, final_norm_g)
    y_p, gla_p, pool_p, kv_p, conv_p = _run_group(
        x_prompt, None, weights, small, dict(in_tm=512, merge_tm=512, ffn_tm=1024, gla_hps=1))
    states = dict(gla=state_gla, pool=state_pool, conv=state_ffn_conv,
                  kv=[cache_k_w128, cache_v_w128, cache_k_w512, cache_v_w512, cache_k_w2048, cache_v_w2048])
    m_s = x_sample.shape[0] * x_sample.shape[1]
    y_s, gla_s, pool_s, kv_s, conv_s = _run_group(
        x_sample, states, weights, small, dict(in_tm=m_s, merge_tm=m_s, ffn_tm=m_s, gla_hps=GLA_HEADS))
    k128_p, v128_p, k512_p, v512_p, k2048_p, v2048_p = kv_p
    k128_s, v128_s, k512_s, v512_s, k2048_s, v2048_s = kv_s
    return (y_p, y_s, gla_p, gla_s, pool_p, pool_s, k128_p, k128_s, v128_p, v128_s,
            k512_p, k512_s, v512_p, v512_s, k2048_p, k2048_s, v2048_p, v2048_s, conv_p, conv_s)
```

```python
import functools

import jax
import jax.numpy as jnp
import numpy as np
from jax import lax
from jax.experimental import pallas as pl
from jax.experimental.pallas import tpu as pltpu

F32 = jnp.float32
BF16 = jnp.bfloat16

D_MODEL = 1024
DEPTH = 2
GLA_HEADS = 4
GLA_DK = 128
GLA_DV = 256
GLA_QK = GLA_HEADS * GLA_DK
GLA_VW = GLA_HEADS * GLA_DV
GLA_GATE_RANK = 16
GLA_GATE_NORM = 16.0
GLA_CHUNK = 64
GLA_SCAN_ROWS = 256
GLA_UNROLL = 32
POOL_WINDOWS = (2, 4, 8, 16)
POOL_GROUP = 256
POOL_STATE = 15
POOL_HALO = 16
ATT_GROUPS = ((128, 1), (512, 4), (2048, 16))
ATT_HG = 4
ATT_E = 64
ATT_GW = ATT_HG * ATT_E
ATT_SPAN = 128
ATT_UNROLL = 16
ATT_HEADS = 12
ATT_WIDTH = ATT_HEADS * ATT_E
D_FF = 2816
NORM_EPS = 1e-6
NEG_INF = -1e30

LANE = 128
SUBLANE = 8
BF16_SUBLANE = 16
VMEM_LIMIT = 56 * 1024 * 1024

_IN_SPLITS = (GLA_QK, GLA_QK, GLA_VW, GLA_VW, GLA_GATE_RANK, D_MODEL, ATT_WIDTH, ATT_WIDTH, ATT_WIDTH,
              D_MODEL, D_MODEL, D_MODEL)
_IN_OFF = [0] + [int(v) for v in np.cumsum(_IN_SPLITS)]
ZG_W = 2 * GLA_QK + 2 * GLA_VW
ZG_K, ZG_V, ZG_R = GLA_QK, 2 * GLA_QK, 2 * GLA_QK + GLA_VW
A_W = 3 * ATT_WIDTH
A_Q, A_K, A_V = 0, ATT_WIDTH, 2 * ATT_WIDTH
ZG_TN, U_TN, A_TN = 1024, 512, 768

_SLOPES = (2.0 ** (-8.0 * np.arange(1, ATT_HEADS + 1) / ATT_HEADS)).astype(np.float32)


def _params(sem):
    return pltpu.CompilerParams(dimension_semantics=sem, vmem_limit_bytes=VMEM_LIMIT)


def _resident(shape):
    return pl.BlockSpec(shape, lambda *_: (0,) * len(shape), pipeline_mode=pl.Buffered(1))


def _act_dtype(t_len):
    return BF16 if t_len % BF16_SUBLANE == 0 else F32


def _dot(a, b):
    return jnp.dot(a, b, preferred_element_type=F32)


def _dot_nt(a, b):
    return lax.dot_general(a, b, (((1,), (1,)), ((), ())), preferred_element_type=F32)


def _dot_tn(a, b):
    return lax.dot_general(a, b, (((0,), (0,)), ((), ())), preferred_element_type=F32)


def _rms(x, g):
    return x * lax.rsqrt(jnp.mean(x * x, axis=-1, keepdims=True) + NORM_EPS) * g


def _sigmoid(x):
    return 0.5 * jnp.tanh(0.5 * x) + 0.5


def _in_proj_kernel(x_ref, g_ref, wal_ref, *refs, tiles):
    nseg = len(tiles)
    w_refs, al_ref, out_refs, xn_ref = refs[:nseg], refs[nseg], refs[nseg + 1:2 * nseg + 1], refs[-1]
    xn_ref[...] = _rms(x_ref[...], g_ref[...]).astype(BF16)
    al_ref[...] = _dot(xn_ref[...], wal_ref[...]).astype(al_ref.dtype)
    for w_ref, o_ref, tn in zip(w_refs, out_refs, tiles):
        for c in range(w_ref.shape[1] // tn):
            cols = slice(c * tn, (c + 1) * tn)
            o_ref[:, cols] = _dot(xn_ref[...], w_ref[:, cols]).astype(o_ref.dtype)


def _in_proj(x2, g, wal, segs, tm, act):
    m = x2.shape[0]
    row = lambda w: pl.BlockSpec((tm, w), lambda i: (i, 0))
    return pl.pallas_call(
        functools.partial(_in_proj_kernel, tiles=tuple(tn for _, tn, _ in segs)),
        grid=(m // tm,),
        in_specs=[row(D_MODEL), _resident(g.shape), _resident(wal.shape)] + [_resident(w.shape) for w, _, _ in segs],
        out_specs=[row(LANE)] + [row(w.shape[1]) for w, _, _ in segs],
        out_shape=[jax.ShapeDtypeStruct((m, LANE), act)]
                  + [jax.ShapeDtypeStruct((m, w.shape[1]), dt) for w, _, dt in segs],
        scratch_shapes=[pltpu.VMEM((tm, D_MODEL), BF16)],
        compiler_params=_params(("parallel",)),
        name="in_proj",
    )(x2, g, wal, *[w for w, _, _ in segs])


def _gla_kernel(q_ref, k_ref, v_ref, r_ref, a_ref, wa_ref, ba_ref, gn_ref, s0_ref, o_ref, so_ref, *scratch,
                c, nc, hps):
    for hh in range(hps):
        dk = slice(hh * GLA_DK, (hh + 1) * GLA_DK)
        dv = slice(hh * GLA_DV, (hh + 1) * GLA_DV)
        _gla_head(q_ref.at[:, dk], k_ref.at[:, dk], v_ref.at[:, dv], r_ref.at[:, dv], a_ref, wa_ref.at[:, dk],
                  ba_ref.at[:, dk], gn_ref, s0_ref.at[hh], o_ref.at[:, dv], so_ref.at[hh], *scratch, c=c, nc=nc)


def _gla_head(q_ref, k_ref, v_ref, r_ref, a_ref, wa_ref, ba_ref, gn_ref, s0_ref, o_ref, so_ref,
              qe_ref, ke_ref, qi_ref, ks_ref, vb_ref, att_ref, dec_ref, oacc_ref, kv_ref, st_ref, *, c, nc):
    ce = max(c, BF16_SUBLANE)
    tp = nc * ce
    mid = (c - 1) // 2

    def load(ref):
        x = ref[...].astype(F32)
        if ce != c:
            x = jnp.concatenate([x, jnp.zeros((ce - c, x.shape[1]), F32)], axis=0)
        return x

    z = _dot(load(a_ref).astype(BF16), wa_ref[...]) + ba_ref[...]
    la = (jnp.minimum(z, 0.0) - jnp.log(1.0 + jnp.exp(-jnp.abs(z)))) * (1.0 / GLA_GATE_NORM)
    if ce != c:
        la = jnp.where(lax.broadcasted_iota(jnp.int32, (tp, GLA_DK), 0) < c, la, 0.0)
    slab = min(tp, GLA_SCAN_ROWS)
    ri = lax.broadcasted_iota(jnp.int32, (slab, slab), 0)
    ci = lax.broadcasted_iota(jnp.int32, (slab, slab), 1)
    tri = jnp.where((ri >= ci) & ((ri & -ce) == (ci & -ce)), 1.0, 0.0).astype(BF16)
    la_hi = la.astype(BF16)
    la_lo = (la - la_hi.astype(F32)).astype(BF16)
    b = jnp.concatenate(
        [_dot(tri, la_hi[i * slab:(i + 1) * slab]) + _dot(tri, la_lo[i * slab:(i + 1) * slab])
         for i in range(tp // slab)], axis=0)
    b3 = b.reshape(nc, ce, GLA_DK)
    b_mid = b3[:, mid:mid + 1, :]
    b_last = b3[:, c - 1:c, :]
    q3 = (load(q_ref) * (GLA_DK ** -0.5)).reshape(nc, ce, GLA_DK)
    k3 = load(k_ref).reshape(nc, ce, GLA_DK)
    flat = lambda x: x.reshape(tp, GLA_DK).astype(BF16)
    qe = q3 * jnp.exp(b3 - b_mid)
    ke = k3 * jnp.exp(b_mid - b3)
    qe_ref[...] = flat(qe)
    ke_ref[...] = flat(ke)
    qi_ref[...] = flat(qe * jnp.exp(b_mid))
    ks_ref[...] = flat(ke * jnp.exp(b_last - b_mid))
    dec_ref[...] = jnp.exp(b_last)
    vb_ref[...] = load(v_ref).astype(BF16)
    causal = (lax.broadcasted_iota(jnp.int32, (ce, ce), 0) >= lax.broadcasted_iota(jnp.int32, (ce, ce), 1))
    rows = lambda n: pl.ds(pl.multiple_of(n * ce, ce), ce)

    def scores(n, carry):
        sl = rows(n)
        att_ref[sl, :] = jnp.where(causal, _dot_nt(qe_ref[sl, :], ke_ref[sl, :]), 0.0).astype(BF16)
        return carry

    def within(n, carry):
        sl = rows(n)
        oacc_ref[sl, :] = _dot(att_ref[sl, :], vb_ref[sl, :])
        return carry

    def increments(n, carry):
        sl = rows(n)
        kv_ref[n] = _dot_tn(vb_ref[sl, :], ks_ref[sl, :])
        return carry

    def across(n, carry):
        sl = rows(n)
        st = st_ref[...]
        oacc_ref[sl, :] += _dot_nt(qi_ref[sl, :], st.astype(BF16))
        st_ref[...] = st * dec_ref[n] + kv_ref[n]
        return carry

    unroll = min(nc, GLA_UNROLL)
    lax.fori_loop(0, nc, scores, 0, unroll=unroll)
    lax.fori_loop(0, nc, within, 0, unroll=unroll)
    lax.fori_loop(0, nc, increments, 0, unroll=unroll)
    st_ref[...] = s0_ref[...].T
    lax.fori_loop(0, nc, across, 0, unroll=unroll)
    r = load(r_ref)
    res = _rms(oacc_ref[...], gn_ref[...]) * (r * _sigmoid(r))
    o_ref[...] = res[:nc * c if ce == c else c].astype(o_ref.dtype)
    so_ref[...] = st_ref[...].T


def _gla(zg3, al3, wa2p, ba, gn, s0, layer, t_len, hps):
    bsz = zg3.shape[0]
    c = GLA_CHUNK if t_len % GLA_CHUNK == 0 else t_len
    nc = t_len // c
    ce = max(c, BF16_SUBLANE)
    assert ce == c or nc == 1
    tp = nc * ce
    kern = functools.partial(_gla_kernel, c=c, nc=nc, hps=hps)
    zspec = lambda w, off: pl.BlockSpec((None, t_len, hps * w), lambda b, h: (b, 0, off // (hps * w) + h))
    return pl.pallas_call(
        kern,
        grid=(bsz, GLA_HEADS // hps),
        in_specs=[
            zspec(GLA_DK, 0), zspec(GLA_DK, ZG_K), zspec(GLA_DV, ZG_V), zspec(GLA_DV, ZG_R),
            pl.BlockSpec((None, t_len, LANE), lambda b, h: (b, 0, 0)),
            pl.BlockSpec((LANE, hps * GLA_DK), lambda b, h: (0, h)),
            pl.BlockSpec((1, hps * GLA_DK), lambda b, h: (0, h)),
            pl.BlockSpec((1, GLA_DV), lambda b, h: (0, 0)),
            pl.BlockSpec((None, None, hps, GLA_DK, GLA_DV), lambda b, h: (layer, b, h, 0, 0)),
        ],
        out_specs=[
            pl.BlockSpec((None, t_len, hps * GLA_DV), lambda b, h: (b, 0, h)),
            pl.BlockSpec((None, hps, GLA_DK, GLA_DV), lambda b, h: (b, h, 0, 0)),
        ],
        out_shape=[
            jax.ShapeDtypeStruct((bsz, t_len, GLA_VW), zg3.dtype),
            jax.ShapeDtypeStruct((bsz, GLA_HEADS, GLA_DK, GLA_DV), F32),
        ],
        scratch_shapes=[pltpu.VMEM((tp, GLA_DK), BF16)] * 4 + [
            pltpu.VMEM((tp, GLA_DV), BF16),
            pltpu.VMEM((tp, ce), BF16),
            pltpu.VMEM((nc, 1, GLA_DK), F32),
            pltpu.VMEM((tp, GLA_DV), F32),
            pltpu.VMEM((nc, GLA_DV, GLA_DK), F32),
            pltpu.VMEM((GLA_DV, GLA_DK), F32),
        ],
        compiler_params=_params(("parallel", "arbitrary")),
        name="gla",
    )(zg3, zg3, zg3, zg3, al3, wa2p, ba, gn, s0)


def _pool_mix(ext, t_abs, pw_ref, ps_ref):
    outs = []
    for gi, w in enumerate(POOL_WINDOWS):
        cols = slice(gi * POOL_GROUP, (gi + 1) * POOL_GROUP)
        x = ext[:, cols]
        acc = x
        s = 1
        while s < w:
            acc = acc + pltpu.roll(acc, s, axis=0)
            s *= 2
        cnt = jnp.minimum(t_abs + 1, w).astype(F32)
        pooled = acc[POOL_HALO:] / cnt - x[POOL_HALO:]
        outs.append(_dot(pooled.astype(BF16), pw_ref[gi]) * ps_ref[:, cols])
    return outs


def _pool_kernel(u_ref, prev_ref, pw_ref, ps_ref, p_ref, pn_ref, *, t_len, n_prev):
    ext = jnp.concatenate([prev_ref[...], u_ref[...]], axis=0)
    t_abs = lax.broadcasted_iota(jnp.int32, (t_len, 1), 0) + n_prev
    for gi, mixed in enumerate(_pool_mix(ext, t_abs, pw_ref, ps_ref)):
        p_ref[:, gi * POOL_GROUP:(gi + 1) * POOL_GROUP] = mixed.astype(p_ref.dtype)
    keep = max(0, POOL_STATE - t_len)
    if keep:
        pn_ref[0:keep, :] = prev_ref[POOL_HALO - keep:POOL_HALO, :]
    pn_ref[keep:POOL_STATE, :] = u_ref[t_len - (POOL_STATE - keep):t_len, :]


def _pool(u3, prev, pw, ps, t_len, n_prev):
    bsz = u3.shape[0]
    kern = functools.partial(_pool_kernel, t_len=t_len, n_prev=n_prev)
    return pl.pallas_call(
        kern,
        grid=(bsz,),
        in_specs=[
            pl.BlockSpec((None, t_len, D_MODEL), lambda b: (b, 0, 0)),
            pl.BlockSpec((None, POOL_HALO, D_MODEL), lambda b: (b, 0, 0)),
            pl.BlockSpec((4, POOL_GROUP, POOL_GROUP), lambda b: (0, 0, 0)),
            pl.BlockSpec((1, D_MODEL), lambda b: (0, 0)),
        ],
        out_specs=[
            pl.BlockSpec((None, t_len, D_MODEL), lambda b: (b, 0, 0)),
            pl.BlockSpec((None, POOL_STATE, D_MODEL), lambda b: (b, 0, 0)),
        ],
        out_shape=[
            jax.ShapeDtypeStruct((bsz, t_len, D_MODEL), _act_dtype(t_len)),
            jax.ShapeDtypeStruct((bsz, POOL_STATE, D_MODEL), F32),
        ],
        compiler_params=_params(("parallel",)),
        name="pool",
    )(u3, prev, pw, ps)


def _attn_prompt_kernel(q_ref, k_ref, v_ref, o_ref, mo_ref, do_ref, kc_ref, vc_ref,
                        qm_ref, ks_ref, vs_ref, os_ref, ms_ref, ds_ref, bias_ref, s_ref, m_ref, stage_ref,
                        *, t_len, keep, dil, slopes):
    n = t_len // dil
    nb = n // ATT_SPAN
    hp = pl.program_id(1)
    kc_ref[...] = k_ref[t_len - keep:t_len, :].astype(F32)
    vc_ref[...] = v_ref[t_len - keep:t_len, :].astype(F32)
    via_swap = dil % SUBLANE == 0

    step = BF16_SUBLANE if via_swap else n
    slabs = [(slice(i * step, (i + 1) * step), slice(i * step * dil, (i + 1) * step * dil))
             for i in range(n // step)]

    def to_res(ref, tok):
        if dil == 1:
            return ref[...].astype(F32)[None]
        if via_swap:
            return jnp.swapaxes(ref[tok, :].astype(F32).reshape(step, dil, LANE), 0, 1)
        stage_ref[...] = ref[...].astype(F32)
        return jnp.stack([stage_ref[pl.ds(r, n, stride=dil), :] for r in range(dil)], axis=0)

    lane3 = lax.broadcasted_iota(jnp.int32, (dil, step, LANE), 2)
    for res, tok in slabs:
        q3 = to_res(q_ref, tok) * (ATT_E ** -0.5)
        qm_ref[0, :, res, :] = jnp.where(lane3 < ATT_E, q3, 0.0).astype(BF16)
        qm_ref[1, :, res, :] = jnp.where(lane3 >= ATT_E, q3, 0.0).astype(BF16)
        ks_ref[:, res, :] = to_res(k_ref, tok).astype(BF16)
        vs_ref[:, res, 0:LANE] = to_res(v_ref, tok).astype(BF16)
    vs_ref[:, :, LANE:2 * LANE] = jnp.ones((dil, n, LANE), BF16)

    nk = ATT_SPAN if nb == 1 else 2 * ATT_SPAN
    a_idx = lax.broadcasted_iota(jnp.int32, (ATT_SPAN, nk), 0)
    c_idx = lax.broadcasted_iota(jnp.int32, (ATT_SPAN, nk), 1)
    for var in range(1 if nb == 1 else 2):
        j = a_idx - c_idx + var * ATT_SPAN
        valid = (j >= 0) & (j <= ATT_SPAN)
        for h in range(2):
            scale = jnp.where(hp == 0, -slopes[h] * dil, -slopes[2 + h] * dil)
            bias_ref[h, var] = jnp.where(valid, j.astype(F32) * scale, NEG_INF)

    lane = lax.broadcasted_iota(jnp.int32, (ATT_SPAN, LANE), 1)
    hm = [lane < ATT_E, lane >= ATT_E]

    def slices(t):
        r, qi = t // nb, t % nb
        qsl = pl.ds(pl.multiple_of(qi * ATT_SPAN, ATT_SPAN), ATT_SPAN)
        if nb > 1:
            return r, qsl, pl.ds(pl.multiple_of(jnp.maximum(qi - 1, 0) * ATT_SPAN, ATT_SPAN), nk), jnp.minimum(qi, 1)
        return r, qsl, qsl, 0

    def scores(t, carry):
        r, qsl, ksl, var = slices(t)
        kk = ks_ref[r, ksl, :]
        for h in range(2):
            s = _dot_nt(qm_ref[h, r, qsl, :], kk) + bias_ref[h, var]
            s_ref[t, h] = s
            m_ref[t, h] = jnp.max(s, axis=-1, keepdims=True)
        return carry

    def outputs(t, carry):
        r, qsl, ksl, _ = slices(t)
        vv = vs_ref[r, ksl, :]
        o_acc = jnp.zeros((ATT_SPAN, LANE), F32)
        m_acc = jnp.zeros((ATT_SPAN, LANE), F32)
        d_acc = jnp.zeros((ATT_SPAN, LANE), F32)
        for h in range(2):
            m = m_ref[t, h]
            od = _dot(jnp.exp(s_ref[t, h] - m).astype(BF16), vv)
            o_acc = jnp.where(hm[h], od[:, :LANE], o_acc)
            d_acc = jnp.where(hm[h], od[:, LANE:], d_acc)
            m_acc = jnp.where(hm[h], m, m_acc)
        os_ref[r, qsl, :] = o_acc
        ms_ref[r, qsl, :] = m_acc
        ds_ref[r, qsl, :] = d_acc
        return carry

    lax.fori_loop(0, dil * nb, scores, 0, unroll=ATT_UNROLL)
    lax.fori_loop(0, dil * nb, outputs, 0, unroll=ATT_UNROLL)
    for res_ref, out_ref in ((os_ref, o_ref), (ms_ref, mo_ref), (ds_ref, do_ref)):
        if dil == 1:
            out_ref[...] = res_ref[0]
        elif via_swap:
            for res, tok in slabs:
                out_ref[tok, :] = jnp.swapaxes(res_ref[:, res, :], 0, 1).reshape(step * dil, LANE)
        else:
            for r in range(dil):
                out_ref[pl.ds(r, n, stride=dil), :] = res_ref[r]


def _attn_prompt(a3, gi):
    bsz, t_len, _ = a3.shape
    win, dil = ATT_GROUPS[gi]
    keep = min(win, t_len)
    kern = functools.partial(_attn_prompt_kernel, t_len=t_len, keep=keep, dil=dil,
                             slopes=[float(s) for s in _SLOPES[gi * ATT_HG:(gi + 1) * ATT_HG]])
    uspec = lambda off: pl.BlockSpec((None, t_len, LANE), lambda b, hp: (b, 0, (off + gi * ATT_GW) // LANE + hp))
    ospec = lambda rows: pl.BlockSpec((None, rows, LANE), lambda b, hp: (b, 0, hp))
    n = t_len // dil
    nk = ATT_SPAN if n == ATT_SPAN else 2 * ATT_SPAN
    nblk = t_len // ATT_SPAN
    scratch = [pltpu.VMEM((2, dil, n, LANE), BF16), pltpu.VMEM((dil, n, LANE), BF16),
               pltpu.VMEM((dil, n, 2 * LANE), BF16), pltpu.VMEM((dil, n, LANE), F32),
               pltpu.VMEM((dil, n, LANE), F32), pltpu.VMEM((dil, n, LANE), F32),
               pltpu.VMEM((2, 2, ATT_SPAN, nk), F32),
               pltpu.VMEM((nblk, 2, ATT_SPAN, nk), F32), pltpu.VMEM((nblk, 2, ATT_SPAN, 1), F32),
               pltpu.VMEM((t_len, LANE), F32)]
    o, mx, dn, kc, vc = pl.pallas_call(
        kern,
        grid=(bsz, ATT_GW // LANE),
        in_specs=[uspec(A_Q), uspec(A_K), uspec(A_V)],
        out_specs=[ospec(t_len)] * 3 + [ospec(keep)] * 2,
        out_shape=[jax.ShapeDtypeStruct((bsz, t_len, ATT_GW), F32)] * 3
                  + [jax.ShapeDtypeStruct((bsz, keep, ATT_GW), F32)] * 2,
        scratch_shapes=scratch,
        compiler_params=_params(("parallel", "parallel")),
        name=f"attn_prompt_g{gi}",
    )(a3, a3, a3)
    m = bsz * t_len
    return ([o.reshape(m, ATT_GW), mx.reshape(m, ATT_GW), dn.reshape(m, ATT_GW)],
            kc.reshape(bsz, keep, ATT_HG, ATT_E), vc.reshape(bsz, keep, ATT_HG, ATT_E))


_KX_PAD = LANE


def _head_masks(rows):
    lane = lax.broadcasted_iota(jnp.int32, (rows, ATT_GW), 1)
    return [(lane >= h * ATT_E) & (lane < (h + 1) * ATT_E) for h in range(ATT_HG)]


def _attn_sample_kernel(*refs, t_len, first, layer_slopes):
    ng = len(ATT_GROUPS)
    qkv = refs[0:3 * ng]
    caches = refs[3 * ng:5 * ng]
    outs = refs[5 * ng:] if first else refs[7 * ng:]
    o_refs, m_refs, d_refs, cache_out = outs[0:ng], outs[ng:2 * ng], outs[2 * ng:3 * ng], outs[3 * ng:5 * ng]
    kx_ref, vx_ref = outs[5 * ng:]

    def shift_in(c_ref, new_t):
        ext = jnp.concatenate([c_ref[...], new_t], axis=1)
        return ext, pltpu.roll(ext, ext.shape[1] - t_len, axis=1)

    def attend():
        hm = _head_masks(t_len)
        rows = ATT_HG * t_len
        zrows = jnp.zeros((_KX_PAD - t_len, ATT_GW), F32)
        for gi, (win, dil) in enumerate(ATT_GROUPS):
            q_ref, kn_ref, vn_ref = qkv[3 * gi:3 * gi + 3]
            wb = caches[2 * gi].shape[1]
            nk = wb + _KX_PAD
            for c_ref, n_ref, x_ref, co_ref in ((caches[2 * gi], kn_ref, kx_ref, cache_out[2 * gi]),
                                                (caches[2 * gi + 1], vn_ref, vx_ref, cache_out[2 * gi + 1])):
                new_t = jnp.concatenate([n_ref[...], zrows], axis=0).T
                ext, moved = shift_in(c_ref, new_t)
                x_ref[:, 0:nk] = ext.astype(BF16)
                co_ref[...] = moved[:, 0:wb] if first else moved[:, wb - LANE:wb]
            q = q_ref[...] * (ATT_E ** -0.5)
            qst = jnp.concatenate([jnp.where(hm[h], q, 0.0) for h in range(ATT_HG)], axis=0).astype(BF16)
            s = _dot(qst, kx_ref[:, 0:nk])
            ri = lax.broadcasted_iota(jnp.int32, (rows, nk), 0)
            ci = lax.broadcasted_iota(jnp.int32, (rows, nk), 1)
            dist = wb + (ri % t_len) - ci
            valid = (dist >= 0) & (dist <= win) & ((dist & (dil - 1)) == 0)
            hrow = lax.broadcasted_iota(jnp.int32, (rows, 1), 0) // t_len
            slope = jnp.zeros((rows, 1), F32)
            for h in range(ATT_HG):
                slope = jnp.where(hrow == h, float(layer_slopes[gi * ATT_HG + h]), slope)
            s = jnp.where(valid, s - slope * dist.astype(F32), NEG_INF)
            m = jnp.max(s, axis=-1, keepdims=True)
            p = jnp.exp(s - m)
            den = jnp.sum(p, axis=-1, keepdims=True)
            ost = _dot_nt(p.astype(BF16), vx_ref[:, 0:nk])
            o = jnp.zeros((t_len, ATT_GW), F32)
            mx = jnp.zeros((t_len, ATT_GW), F32)
            dn = jnp.zeros((t_len, ATT_GW), F32)
            for h in range(ATT_HG):
                hrows = slice(h * t_len, (h + 1) * t_len)
                o = jnp.where(hm[h], ost[hrows], o)
                mx = jnp.where(hm[h], m[hrows], mx)
                dn = jnp.where(hm[h], den[hrows], dn)
            o_refs[gi][...] = o
            m_refs[gi][...] = mx
            d_refs[gi][...] = dn

    if not first:
        attend()
        return
    pl.when(pl.program_id(1) == 0)(attend)

    @pl.when(pl.program_id(1) > 0)
    def _():
        for c_ref, co_ref in zip(caches, cache_out):
            _, moved = shift_in(c_ref, jnp.zeros((ATT_GW, _KX_PAD), F32))
            co_ref[...] = moved[:, 0:c_ref.shape[1]]


def _attn_sample(a3, caches_t, layer, cache_prev):
    bsz, t_len, _ = a3.shape
    ng = len(ATT_GROUPS)
    first = cache_prev is None
    kern = functools.partial(_attn_sample_kernel, t_len=t_len, first=first,
                             layer_slopes=[float(s) for s in _SLOPES])
    in_specs, args = [], []
    for gi in range(ng):
        for off in (A_Q, A_K, A_V):
            in_specs.append(pl.BlockSpec((None, t_len, ATT_GW), lambda b, *_, o=off // ATT_GW + gi: (b, 0, o)))
            args.append(a3)
    small = pl.BlockSpec((None, t_len, ATT_GW), lambda b, *_: (b, 0, 0))
    out_specs = [small] * (3 * ng)
    out_shape = [jax.ShapeDtypeStruct((bsz, t_len, ATT_GW), F32)] * (3 * ng)
    wbs = [c.shape[3] for c in caches_t]
    for c, wb in zip(caches_t, wbs):
        whole = (None, None, ATT_GW, wb)
        if first:
            in_specs.append(pl.BlockSpec(whole, lambda b, l: (l, b, 0, 0)))
            out_specs.append(pl.BlockSpec(whole, lambda b, l: (l, b, 0, 0)))
        else:
            in_specs.append(pl.BlockSpec(whole, lambda b: (layer, b, 0, 0)))
            out_specs.append(pl.BlockSpec((None, None, ATT_GW, LANE), lambda b, t=wb // LANE - 1: (layer, b, 0, t)))
        args.append(c)
        out_shape.append(jax.ShapeDtypeStruct(c.shape, F32))
    aliases = {}
    if not first:
        for i, c in enumerate(cache_prev):
            in_specs.append(pl.BlockSpec(memory_space=pl.ANY))
            args.append(c)
            aliases[5 * ng + i] = 3 * ng + i
    res = pl.pallas_call(
        kern,
        grid=(bsz, DEPTH) if first else (bsz,),
        in_specs=in_specs,
        out_specs=out_specs,
        out_shape=out_shape,
        scratch_shapes=[pltpu.VMEM((ATT_GW, max(wbs) + _KX_PAD), BF16)] * 2,
        input_output_aliases=aliases,
        compiler_params=_params(("parallel", "arbitrary") if first else ("parallel",)),
        name="attn_sample",
    )(*args)
    m = bsz * t_len
    parts = [[res[k * ng + gi].reshape(m, ATT_GW) for k in range(3)] for gi in range(ng)]
    return parts, list(res[3 * ng:])


def _merge_kernel(*refs, tm, pool_t_len):
    it = iter(refs)
    x_ref, og_ref = next(it), next(it)
    if pool_t_len is None:
        p_ref = next(it)
    else:
        u_ref, halo_ref = next(it), next(it)
    att = [(next(it), next(it), next(it)) for _ in ATT_GROUPS]
    g_ref, wg_ref, woa_ref, wob_ref, woc_ref, wout_ref = (next(it) for _ in range(6))
    if pool_t_len is not None:
        pw_ref, ps_ref = next(it), next(it)
    h_ref = next(it)
    x = x_ref[...]
    xn = _rms(x, g_ref[...]).astype(BF16)

    def gate(i):
        return _sigmoid(_dot(xn, wg_ref[:, i * D_MODEL:(i + 1) * D_MODEL]))

    merged = gate(0) * _dot(og_ref[...].astype(BF16), woa_ref[...])
    if pool_t_len is None:
        yb = _dot(p_ref[...].astype(BF16), wob_ref[...])
    else:
        t0 = (pl.program_id(0) * tm) % pool_t_len
        halo = jnp.where(t0 == 0, 0.0, halo_ref[...])
        ext = jnp.concatenate([halo, u_ref[...]], axis=0)
        t_abs = lax.broadcasted_iota(jnp.int32, (tm, 1), 0) + t0
        yb = jnp.zeros_like(x)
        for gi, mixed in enumerate(_pool_mix(ext, t_abs, pw_ref, ps_ref)):
            yb = yb + _dot(mixed.astype(BF16), wob_ref[gi * POOL_GROUP:(gi + 1) * POOL_GROUP, :])
    merged = merged + gate(1) * yb
    ms = [m_ref[...] for _, m_ref, _ in att]
    m = jnp.maximum(jnp.maximum(ms[0], ms[1]), ms[2])
    ws = [jnp.exp(mg - m) for mg in ms]
    inv = 1.0 / sum(d_ref[...] * w for (_, _, d_ref), w in zip(att, ws))
    yc = jnp.zeros_like(x)
    for gi, (o_ref, _, _) in enumerate(att):
        oc = (o_ref[...] * (ws[gi] * inv)).astype(BF16)
        yc = yc + _dot(oc, woc_ref[gi * ATT_GW:(gi + 1) * ATT_GW, :])
    merged = merged + gate(2) * yc
    h_ref[...] = x + _dot(merged.astype(BF16), wout_ref[...])


def _merge(x2, og, p, att, g, wg, woa, wob, woc, wout, tm, pool=None):
    m = x2.shape[0]
    row = lambda w: pl.BlockSpec((tm, w), lambda i: (i, 0))
    in_specs = [row(D_MODEL), row(GLA_VW), row(D_MODEL)]
    args = [x2, og, p]
    if pool is not None:
        hb = tm // POOL_HALO
        in_specs.append(pl.BlockSpec((POOL_HALO, D_MODEL), lambda i: (jnp.maximum(i * hb - 1, 0), 0)))
        args.append(p)
    weights = [g, wg, woa, wob, woc, wout] + ([] if pool is None else list(pool[:2]))
    return pl.pallas_call(
        functools.partial(_merge_kernel, tm=tm, pool_t_len=None if pool is None else pool[2]),
        grid=(m // tm,),
        in_specs=in_specs + [row(ATT_GW)] * (3 * len(att)) + [_resident(a.shape) for a in weights],
        out_specs=row(D_MODEL),
        out_shape=jax.ShapeDtypeStruct((m, D_MODEL), F32),
        compiler_params=_params(("parallel",)),
        name="merge",
    )(*args, *[a for parts in att for a in parts], *weights)


FFN_TF = 256
FFN_HALO = 16


def _ffn_kernel(*refs, t_len, tm, use_halo, has_prev, final):
    it = iter(refs)
    h_ref = next(it)
    halo_ref = next(it) if use_halo else None
    g_ref, wup_ref, cw_ref, cb_ref, wd_ref, fg_ref = (next(it) for _ in range(6))
    c1_ref, c2_ref = (next(it), next(it)) if has_prev else (None, None)
    y_ref, cv_ref = next(it), next(it)
    hn_ref, hh_ref = next(it), next(it)
    a_scr = None if use_halo else next(it)
    ho = FFN_HALO if use_halo else 0
    if use_halo:
        hn_ref[0:ho, :] = _rms(halo_ref[...], g_ref[...]).astype(BF16)
    hn_ref[ho:ho + tm, :] = _rms(h_ref[...], g_ref[...]).astype(BF16)
    t_loc = (lax.broadcasted_iota(jnp.int32, (tm, 1), 0) + pl.program_id(0) * tm) % t_len
    for jt in range(D_FF // FFN_TF):
        cols = slice(jt * FFN_TF, (jt + 1) * FFN_TF)
        a_ext = _dot(hn_ref[...], wup_ref[:, cols])
        a = a_ext[ho:]
        bg = _dot(hn_ref[ho:ho + tm, :], wup_ref[:, D_FF + jt * FFN_TF:D_FF + (jt + 1) * FFN_TF])
        if use_halo:
            a1, a2 = a_ext[ho - 1:ho - 1 + tm], a_ext[ho - 2:ho - 2 + tm]
        else:
            a1, a2 = pltpu.roll(a, 1, axis=0), pltpu.roll(a, 2, axis=0)
        a1 = jnp.where(t_loc >= 1, a1, c1_ref[:, cols] if has_prev else 0.0)
        a2 = jnp.where(t_loc >= 2, a2, c2_ref[:, cols] if has_prev else 0.0)
        y = cb_ref[:, cols] + cw_ref[0:1, cols] * a2 + cw_ref[1:2, cols] * a1 + cw_ref[2:3, cols] * a
        hh_ref[:, cols] = ((y * _sigmoid(y)) * bg).astype(BF16)
        if use_halo:
            cv_ref[:, cols] = a[tm - 2:tm]
        else:
            nseq = tm // t_len
            for kk in range(FFN_TF // LANE):
                lc = slice(kk * LANE, (kk + 1) * LANE)
                oc = slice(jt * FFN_TF + kk * LANE, jt * FFN_TF + (kk + 1) * LANE)
                a_scr[kk] = a[:, lc]
                cv_ref[0, :, oc] = a_scr[kk, pl.ds(t_len - 2, nseq, stride=t_len), :]
                cv_ref[1, :, oc] = a_scr[kk, pl.ds(t_len - 1, nseq, stride=t_len), :]
    out = h_ref[...] + _dot(hh_ref[...], wd_ref[...])
    if final:
        out = _rms(out, fg_ref[...])
    y_ref[...] = out


def _ffn(h2, g, wup, cw, cb, wd, fg, prev, t_len, tm, final):
    m = h2.shape[0]
    bsz = m // t_len
    use_halo = tm % t_len != 0
    has_prev = prev is not None
    kern = functools.partial(_ffn_kernel, t_len=t_len, tm=tm, use_halo=use_halo, has_prev=has_prev, final=final)
    in_specs = [pl.BlockSpec((tm, D_MODEL), lambda i: (i, 0))]
    args = [h2]
    if use_halo:
        hb = tm // FFN_HALO
        in_specs.append(pl.BlockSpec((FFN_HALO, D_MODEL), lambda i: (jnp.maximum(i * hb - 1, 0), 0)))
        args.append(h2)
    in_specs += [_resident(a.shape) for a in (g, wup, cw, cb, wd, fg)]
    args += [g, wup, cw, cb, wd, fg]
    if has_prev:
        z = jnp.zeros((bsz, t_len - 2, D_FF), F32)
        c1 = jnp.concatenate([prev[:, 1:2], jnp.zeros((bsz, 1, D_FF), F32), z], axis=1).reshape(m, D_FF)
        c2 = jnp.concatenate([prev, z], axis=1).reshape(m, D_FF)
        in_specs += [pl.BlockSpec((tm, D_FF), lambda i: (i, 0))] * 2
        args += [c1, c2]
    scratch = [pltpu.VMEM(((FFN_HALO if use_halo else 0) + tm, D_MODEL), BF16), pltpu.VMEM((tm, D_FF), BF16)]
    if use_halo:
        cv_spec = pl.BlockSpec((None, 2, D_FF), lambda i: (i, 0, 0))
        cv_shape = jax.ShapeDtypeStruct((m // tm, 2, D_FF), F32)
    else:
        nseq = tm // t_len
        cv_spec = pl.BlockSpec((2, nseq, D_FF), lambda i: (0, i, 0))
        cv_shape = jax.ShapeDtypeStruct((2, bsz, D_FF), F32)
        scratch.append(pltpu.VMEM((FFN_TF // LANE, tm, LANE), F32))
    y, cv = pl.pallas_call(
        kern,
        grid=(m // tm,),
        in_specs=in_specs,
        out_specs=[pl.BlockSpec((tm, D_MODEL), lambda i: (i, 0)), cv_spec],
        out_shape=[jax.ShapeDtypeStruct((m, D_MODEL), F32), cv_shape],
        scratch_shapes=scratch,
        compiler_params=_params(("parallel",)),
        name="ffn",
    )(*args)
    if use_halo:
        per = t_len // tm
        cv = cv[per - 1::per]
    else:
        cv = jnp.swapaxes(cv, 0, 1)
    return y, cv


def _prep_weights(w_in, gla_wa2, pool_w, w_oa, w_ob, w_oc, w_out, ffn_w_up, ffn_w_down):
    o = _IN_OFF
    cast = lambda a: a.astype(BF16)
    w1 = cast(w_in[:, :, o[0]:o[4]])
    wal = cast(jnp.pad(w_in[:, :, o[4]:o[5]], ((0, 0), (0, 0), (0, LANE - GLA_GATE_RANK))))
    wu = cast(w_in[:, :, o[5]:o[6]])
    wa = cast(w_in[:, :, o[6]:o[9]])
    wg = cast(w_in[:, :, o[9]:o[12]])
    wa2p = cast(jnp.pad(gla_wa2, ((0, 0), (0, LANE - GLA_GATE_RANK), (0, 0))))
    return (w1, wal, wu, wa, wg, wa2p, cast(pool_w), cast(w_oa), cast(w_ob), cast(w_oc), cast(w_out),
            cast(ffn_w_up), cast(ffn_w_down))


def _cache_channel_major(c):
    d, b, w = c.shape[:3]
    return jnp.transpose(c, (0, 1, 3, 4, 2)).reshape(d, b, ATT_GW, w)


def _cache_token_major(ct):
    d, b, _, w = ct.shape
    return jnp.transpose(ct.reshape(d, b, ATT_HG, ATT_E, w), (0, 1, 4, 2, 3))


def _run_group(x, states, weights, small, tiles):
    bsz, t_len, _ = x.shape
    m = bsz * t_len
    prompt = states is None
    act = _act_dtype(t_len)
    w1, wal, wu, wa, wg, wa2p, pw, woa, wob, woc, wout, wup, wdn = weights
    norm1_g, norm2_g, gla_ba, gla_norm_g, pool_scale, conv_w, conv_b, final_g = small
    x2 = x.reshape(m, D_MODEL)
    new_gla, new_pool, new_kv, new_conv = [], [], [], []
    kv_t = None if prompt else [_cache_channel_major(c) for c in states["kv"]]
    kv_run = None
    for l in range(DEPTH):
        al, zg, u, a = _in_proj(x2, norm1_g[l][None], wal[l],
                                [(w1[l], ZG_TN, act), (wu[l], U_TN, F32), (wa[l], A_TN, act)], tiles["in_tm"], act)
        zg3 = zg.reshape(bsz, t_len, ZG_W)
        al3 = al.reshape(bsz, t_len, LANE)
        u3 = u.reshape(bsz, t_len, D_MODEL)
        a3 = a.reshape(bsz, t_len, A_W)
        if prompt:
            s0, s0_layer = jnp.zeros((1, bsz, GLA_HEADS, GLA_DK, GLA_DV), F32), 0
            prev = jnp.zeros((bsz, POOL_HALO, D_MODEL), F32)
            n_prev = 0
        else:
            s0, s0_layer = states["gla"], l
            prev = jnp.pad(states["pool"][l], ((0, 0), (POOL_HALO - POOL_STATE, 0), (0, 0)))
            n_prev = POOL_STATE
        og, g_new = _gla(zg3, al3, wa2p[l], gla_ba[l][None], gla_norm_g[l][None], s0, s0_layer, t_len,
                         tiles["gla_hps"])
        if prompt:
            pb, p_new, fused_pool = u, u3[:, t_len - POOL_STATE:], (pw[l], pool_scale[l][None], t_len)
        else:
            pb, p_new = _pool(u3, prev, pw[l], pool_scale[l][None], t_len, n_prev)
            fused_pool = None
        if prompt:
            att, kv = [], []
            for gi in range(len(ATT_GROUPS)):
                parts, kc, vc = _attn_prompt(a3, gi)
                att.append(parts)
                kv += [kc, vc]
        else:
            att, kv_run = _attn_sample(a3, kv_t, l, kv_run)
            kv = None
        h2 = _merge(x2, og.reshape(m, GLA_VW), pb.reshape(m, D_MODEL), att, norm1_g[l][None], wg[l],
                    woa[l], wob[l], woc[l], wout[l], tiles["merge_tm"], fused_pool)
        x2, c_new = _ffn(h2, norm2_g[l][None], wup[l], conv_w[l], conv_b[l][None], wdn[l], final_g[None],
                         None if prompt else states["conv"][l], t_len, tiles["ffn_tm"], l == DEPTH - 1)
        new_gla.append(g_new)
        new_pool.append(p_new)
        new_kv.append(kv)
        new_conv.append(c_new)
    if prompt:
        kv_out = [jnp.stack([kvl[i] for kvl in new_kv], axis=0) for i in range(2 * len(ATT_GROUPS))]
    else:
        kv_out = [_cache_token_major(c) for c in kv_run]
    return (x2.reshape(bsz, t_len, D_MODEL), jnp.stack(new_gla, 0), jnp.stack(new_pool, 0), kv_out,
            jnp.stack(new_conv, 0))


def kernel(x_prompt, x_sample, state_gla, state_pool, cache_k_w128, cache_v_w128, cache_k_w512, cache_v_w512,
           cache_k_w2048, cache_v_w2048, state_ffn_conv, norm1_g, norm2_g, w_in, gla_wa2, gla_ba, gla_norm_g,
           pool_w, pool_scale, w_oa, w_ob, w_oc, w_out, ffn_w_up, ffn_conv_w, ffn_conv_b, ffn_w_down,
           final_norm_g):
    weights = _prep_weights(w_in, gla_wa2, pool_w, w_oa, w_ob, w_oc, w_out, ffn_w_up, ffn_w_down)
    small = (norm1_g, norm2_g, gla_ba, gla_norm_g, pool_scale, ffn_conv_w, ffn_conv_b, final_norm_g)
    y_p, gla_p, pool_p, kv_p, conv_p = _run_group(
        x_prompt, None, weights, small, dict(in_tm=512, merge_tm=512, ffn_tm=1024, gla_hps=1))
    states = dict(gla=state_gla, pool=state_pool, conv=state_ffn_conv,
                  kv=[cache_k_w128, cache_v_w128, cache_k_w512, cache_v_w512, cache_k_w2048, cache_v_w2048])
    m_s = x_sample.shape[0] * x_sample.shape[1]
    y_s, gla_s, pool_s, kv_s, conv_s = _run_group(
        x_sample, states, weights, small, dict(in_tm=m_s, merge_tm=m_s, ffn_tm=m_s, gla_hps=GLA_HEADS))
    k128_p, v128_p, k512_p, v512_p, k2048_p, v2048_p = kv_p
    k128_s, v128_s, k512_s, v512_s, k2048_s, v2048_s = kv_s
    return (y_p, y_s, gla_p, gla_s, pool_p, pool_s, k128_p, k128_s, v128_p, v128_s,
            k512_p, k512_s, v512_p, v512_s, k2048_p, k2048_s, v2048_p, v2048_s, conv_p, conv_s)
```

```python
import functools

import jax
import jax.numpy as jnp
import numpy as np
from jax import lax
from jax.experimental import pallas as pl
from jax.experimental.pallas import tpu as pltpu

F32 = jnp.float32
BF16 = jnp.bfloat16

D_MODEL = 1024
DEPTH = 2
GLA_HEADS = 4
GLA_DK = 128
GLA_DV = 256
GLA_QK = GLA_HEADS * GLA_DK
GLA_VW = GLA_HEADS * GLA_DV
GLA_GATE_RANK = 16
GLA_GATE_NORM = 16.0
GLA_CHUNK = 64
GLA_SCAN_ROWS = 256
GLA_UNROLL = 32
POOL_WINDOWS = (2, 4, 8, 16)
POOL_GROUP = 256
POOL_STATE = 15
POOL_HALO = 16
ATT_GROUPS = ((128, 1), (512, 4), (2048, 16))
ATT_HG = 4
ATT_E = 64
ATT_GW = ATT_HG * ATT_E
ATT_SPAN = 128
ATT_UNROLL = 16
ATT_HEADS = 12
ATT_WIDTH = ATT_HEADS * ATT_E
D_FF = 2816
NORM_EPS = 1e-6
NEG_INF = -1e30

LANE = 128
SUBLANE = 8
BF16_SUBLANE = 16
VMEM_LIMIT = 56 * 1024 * 1024

_IN_SPLITS = (GLA_QK, GLA_QK, GLA_VW, GLA_VW, GLA_GATE_RANK, D_MODEL, ATT_WIDTH, ATT_WIDTH, ATT_WIDTH,
              D_MODEL, D_MODEL, D_MODEL)
_IN_OFF = [0] + [int(v) for v in np.cumsum(_IN_SPLITS)]
ZG_W = 2 * GLA_QK + 2 * GLA_VW
ZG_K, ZG_V, ZG_R = GLA_QK, 2 * GLA_QK, 2 * GLA_QK + GLA_VW
A_W = 3 * ATT_WIDTH
A_Q, A_K, A_V = 0, ATT_WIDTH, 2 * ATT_WIDTH
ZG_TN, U_TN, A_TN = 1024, 512, 768

_SLOPES = (2.0 ** (-8.0 * np.arange(1, ATT_HEADS + 1) / ATT_HEADS)).astype(np.float32)


def _params(sem):
    return pltpu.CompilerParams(dimension_semantics=sem, vmem_limit_bytes=VMEM_LIMIT)


def _resident(shape):
    return pl.BlockSpec(shape, lambda *_: (0,) * len(shape), pipeline_mode=pl.Buffered(1))


def _act_dtype(t_len):
    return BF16 if t_len % BF16_SUBLANE == 0 else F32


def _dot(a, b):
    return jnp.dot(a, b, preferred_element_type=F32)


def _dot_nt(a, b):
    return lax.dot_general(a, b, (((1,), (1,)), ((), ())), preferred_element_type=F32)


def _dot_tn(a, b):
    return lax.dot_general(a, b, (((0,), (0,)), ((), ())), preferred_element_type=F32)


def _rms(x, g):
    return x * lax.rsqrt(jnp.mean(x * x, axis=-1, keepdims=True) + NORM_EPS) * g


def _sigmoid(x):
    return 0.5 * jnp.tanh(0.5 * x) + 0.5


def _in_proj_kernel(x_ref, g_ref, wal_ref, *refs, tiles):
    nseg = len(tiles)
    w_refs, al_ref, out_refs, xn_ref = refs[:nseg], refs[nseg], refs[nseg + 1:2 * nseg + 1], refs[-1]
    xn_ref[...] = _rms(x_ref[...], g_ref[...]).astype(BF16)
    al_ref[...] = _dot(xn_ref[...], wal_ref[...]).astype(al_ref.dtype)
    for w_ref, o_ref, tn in zip(w_refs, out_refs, tiles):
        for c in range(w_ref.shape[1] // tn):
            cols = slice(c * tn, (c + 1) * tn)
            o_ref[:, cols] = _dot(xn_ref[...], w_ref[:, cols]).astype(o_ref.dtype)


def _in_proj(x2, g, wal, segs, tm, act):
    m = x2.shape[0]
    row = lambda w: pl.BlockSpec((tm, w), lambda i: (i, 0))
    return pl.pallas_call(
        functools.partial(_in_proj_kernel, tiles=tuple(tn for _, tn, _ in segs)),
        grid=(m // tm,),
        in_specs=[row(D_MODEL), _resident(g.shape), _resident(wal.shape)] + [_resident(w.shape) for w, _, _ in segs],
        out_specs=[row(LANE)] + [row(w.shape[1]) for w, _, _ in segs],
        out_shape=[jax.ShapeDtypeStruct((m, LANE), act)]
                  + [jax.ShapeDtypeStruct((m, w.shape[1]), dt) for w, _, dt in segs],
        scratch_shapes=[pltpu.VMEM((tm, D_MODEL), BF16)],
        compiler_params=_params(("parallel",)),
        name="in_proj",
    )(x2, g, wal, *[w for w, _, _ in segs])


def _gla_kernel(q_ref, k_ref, v_ref, r_ref, a_ref, wa_ref, ba_ref, gn_ref, s0_ref, o_ref, so_ref, *scratch,
                c, nc, hps):
    for hh in range(hps):
        dk = slice(hh * GLA_DK, (hh + 1) * GLA_DK)
        dv = slice(hh * GLA_DV, (hh + 1) * GLA_DV)
        _gla_head(q_ref.at[:, dk], k_ref.at[:, dk], v_ref.at[:, dv], r_ref.at[:, dv], a_ref, wa_ref.at[:, dk],
                  ba_ref.at[:, dk], gn_ref, s0_ref.at[hh], o_ref.at[:, dv], so_ref.at[hh], *scratch, c=c, nc=nc)


def _gla_head(q_ref, k_ref, v_ref, r_ref, a_ref, wa_ref, ba_ref, gn_ref, s0_ref, o_ref, so_ref,
              qe_ref, ke_ref, qi_ref, ks_ref, vb_ref, att_ref, dec_ref, oacc_ref, kv_ref, st_ref, *, c, nc):
    ce = max(c, BF16_SUBLANE)
    tp = nc * ce
    mid = (c - 1) // 2

    def load(ref):
        x = ref[...].astype(F32)
        if ce != c:
            x = jnp.concatenate([x, jnp.zeros((ce - c, x.shape[1]), F32)], axis=0)
        return x

    z = _dot(load(a_ref).astype(BF16), wa_ref[...]) + ba_ref[...]
    la = (jnp.minimum(z, 0.0) - jnp.log(1.0 + jnp.exp(-jnp.abs(z)))) * (1.0 / GLA_GATE_NORM)
    if ce != c:
        la = jnp.where(lax.broadcasted_iota(jnp.int32, (tp, GLA_DK), 0) < c, la, 0.0)
    slab = min(tp, GLA_SCAN_ROWS)
    ri = lax.broadcasted_iota(jnp.int32, (slab, slab), 0)
    ci = lax.broadcasted_iota(jnp.int32, (slab, slab), 1)
    tri = jnp.where((ri >= ci) & ((ri & -ce) == (ci & -ce)), 1.0, 0.0).astype(BF16)
    la_hi = la.astype(BF16)
    la_lo = (la - la_hi.astype(F32)).astype(BF16)
    b = jnp.concatenate(
        [_dot(tri, la_hi[i * slab:(i + 1) * slab]) + _dot(tri, la_lo[i * slab:(i + 1) * slab])
         for i in range(tp // slab)], axis=0)
    b3 = b.reshape(nc, ce, GLA_DK)
    b_mid = b3[:, mid:mid + 1, :]
    b_last = b3[:, c - 1:c, :]
    q3 = (load(q_ref) * (GLA_DK ** -0.5)).reshape(nc, ce, GLA_DK)
    k3 = load(k_ref).reshape(nc, ce, GLA_DK)
    flat = lambda x: x.reshape(tp, GLA_DK).astype(BF16)
    qe = q3 * jnp.exp(b3 - b_mid)
    ke = k3 * jnp.exp(b_mid - b3)
    qe_ref[...] = flat(qe)
    ke_ref[...] = flat(ke)
    qi_ref[...] = flat(qe * jnp.exp(b_mid))
    ks_ref[...] = flat(ke * jnp.exp(b_last - b_mid))
    dec_ref[...] = jnp.exp(b_last)
    vb_ref[...] = load(v_ref).astype(BF16)
    causal = (lax.broadcasted_iota(jnp.int32, (ce, ce), 0) >= lax.broadcasted_iota(jnp.int32, (ce, ce), 1))
    rows = lambda n: pl.ds(pl.multiple_of(n * ce, ce), ce)

    def scores(n, carry):
        sl = rows(n)
        att_ref[sl, :] = jnp.where(causal, _dot_nt(qe_ref[sl, :], ke_ref[sl, :]), 0.0).astype(BF16)
        return carry

    def within(n, carry):
        sl = rows(n)
        oacc_ref[sl, :] = _dot(att_ref[sl, :], vb_ref[sl, :])
        return carry

    def increments(n, carry):
        sl = rows(n)
        kv_ref[n] = _dot_tn(vb_ref[sl, :], ks_ref[sl, :])
        return carry

    def across(n, carry):
        sl = rows(n)
        st = st_ref[...]
        oacc_ref[sl, :] += _dot_nt(qi_ref[sl, :], st.astype(BF16))
        st_ref[...] = st * dec_ref[n] + kv_ref[n]
        return carry

    unroll = min(nc, GLA_UNROLL)
    lax.fori_loop(0, nc, scores, 0, unroll=unroll)
    lax.fori_loop(0, nc, within, 0, unroll=unroll)
    lax.fori_loop(0, nc, increments, 0, unroll=unroll)
    st_ref[...] = s0_ref[...].T
    lax.fori_loop(0, nc, across, 0, unroll=unroll)
    r = load(r_ref)
    res = _rms(oacc_ref[...], gn_ref[...]) * (r * _sigmoid(r))
    o_ref[...] = res[:nc * c if ce == c else c].astype(o_ref.dtype)
    so_ref[...] = st_ref[...].T


def _gla(zg3, al3, wa2p, ba, gn, s0, layer, t_len, hps):
    bsz = zg3.shape[0]
    c = GLA_CHUNK if t_len % GLA_CHUNK == 0 else t_len
    nc = t_len // c
    ce = max(c, BF16_SUBLANE)
    assert ce == c or nc == 1
    tp = nc * ce
    kern = functools.partial(_gla_kernel, c=c, nc=nc, hps=hps)
    zspec = lambda w, off: pl.BlockSpec((None, t_len, hps * w), lambda b, h: (b, 0, off // (hps * w) + h))
    return pl.pallas_call(
        kern,
        grid=(bsz, GLA_HEADS // hps),
        in_specs=[
            zspec(GLA_DK, 0), zspec(GLA_DK, ZG_K), zspec(GLA_DV, ZG_V), zspec(GLA_DV, ZG_R),
            pl.BlockSpec((None, t_len, LANE), lambda b, h: (b, 0, 0)),
            pl.BlockSpec((LANE, hps * GLA_DK), lambda b, h: (0, h)),
            pl.BlockSpec((1, hps * GLA_DK), lambda b, h: (0, h)),
            pl.BlockSpec((1, GLA_DV), lambda b, h: (0, 0)),
            pl.BlockSpec((None, None, hps, GLA_DK, GLA_DV), lambda b, h: (layer, b, h, 0, 0)),
        ],
        out_specs=[
            pl.BlockSpec((None, t_len, hps * GLA_DV), lambda b, h: (b, 0, h)),
            pl.BlockSpec((None, hps, GLA_DK, GLA_DV), lambda b, h: (b, h, 0, 0)),
        ],
        out_shape=[
            jax.ShapeDtypeStruct((bsz, t_len, GLA_VW), zg3.dtype),
            jax.ShapeDtypeStruct((bsz, GLA_HEADS, GLA_DK, GLA_DV), F32),
        ],
        scratch_shapes=[pltpu.VMEM((tp, GLA_DK), BF16)] * 4 + [
            pltpu.VMEM((tp, GLA_DV), BF16),
            pltpu.VMEM((tp, ce), BF16),
            pltpu.VMEM((nc, 1, GLA_DK), F32),
            pltpu.VMEM((tp, GLA_DV), F32),
            pltpu.VMEM((nc, GLA_DV, GLA_DK), F32),
            pltpu.VMEM((GLA_DV, GLA_DK), F32),
        ],
        compiler_params=_params(("parallel", "arbitrary")),
        name="gla",
    )(zg3, zg3, zg3, zg3, al3, wa2p, ba, gn, s0)


def _pool_mix(ext, t_abs, pw_ref, ps_ref):
    outs = []
    for gi, w in enumerate(POOL_WINDOWS):
        cols = slice(gi * POOL_GROUP, (gi + 1) * POOL_GROUP)
        x = ext[:, cols]
        acc = x
        s = 1
        while s < w:
            acc = acc + pltpu.roll(acc, s, axis=0)
            s *= 2
        cnt = jnp.minimum(t_abs + 1, w).astype(F32)
        pooled = acc[POOL_HALO:] / cnt - x[POOL_HALO:]
        outs.append(_dot(pooled.astype(BF16), pw_ref[gi]) * ps_ref[:, cols])
    return outs


def _pool_kernel(u_ref, prev_ref, pw_ref, ps_ref, p_ref, pn_ref, *, t_len, n_prev):
    ext = jnp.concatenate([prev_ref[...], u_ref[...]], axis=0)
    t_abs = lax.broadcasted_iota(jnp.int32, (t_len, 1), 0) + n_prev
    for gi, mixed in enumerate(_pool_mix(ext, t_abs, pw_ref, ps_ref)):
        p_ref[:, gi * POOL_GROUP:(gi + 1) * POOL_GROUP] = mixed.astype(p_ref.dtype)
    keep = max(0, POOL_STATE - t_len)
    if keep:
        pn_ref[0:keep, :] = prev_ref[POOL_HALO - keep:POOL_HALO, :]
    pn_ref[keep:POOL_STATE, :] = u_ref[t_len - (POOL_STATE - keep):t_len, :]


def _pool(u3, prev, pw, ps, t_len, n_prev):
    bsz = u3.shape[0]
    kern = functools.partial(_pool_kernel, t_len=t_len, n_prev=n_prev)
    return pl.pallas_call(
        kern,
        grid=(bsz,),
        in_specs=[
            pl.BlockSpec((None, t_len, D_MODEL), lambda b: (b, 0, 0)),
            pl.BlockSpec((None, POOL_HALO, D_MODEL), lambda b: (b, 0, 0)),
            pl.BlockSpec((4, POOL_GROUP, POOL_GROUP), lambda b: (0, 0, 0)),
            pl.BlockSpec((1, D_MODEL), lambda b: (0, 0)),
        ],
        out_specs=[
            pl.BlockSpec((None, t_len, D_MODEL), lambda b: (b, 0, 0)),
            pl.BlockSpec((None, POOL_STATE, D_MODEL), lambda b: (b, 0, 0)),
        ],
        out_shape=[
            jax.ShapeDtypeStruct((bsz, t_len, D_MODEL), _act_dtype(t_len)),
            jax.ShapeDtypeStruct((bsz, POOL_STATE, D_MODEL), F32),
        ],
        compiler_params=_params(("parallel",)),
        name="pool",
    )(u3, prev, pw, ps)


def _attn_prompt_kernel(q_ref, k_ref, v_ref, o_ref, mo_ref, do_ref, kc_ref, vc_ref,
                        qm_ref, ks_ref, vs_ref, os_ref, ms_ref, ds_ref, bias_ref, s_ref, m_ref, stage_ref,
                        *, t_len, keep, dil, slopes):
    n = t_len // dil
    nb = n // ATT_SPAN
    hp = pl.program_id(1)
    kc_ref[...] = k_ref[t_len - keep:t_len, :].astype(F32)
    vc_ref[...] = v_ref[t_len - keep:t_len, :].astype(F32)
    via_swap = dil % SUBLANE == 0

    step = BF16_SUBLANE if via_swap else n
    slabs = [(slice(i * step, (i + 1) * step), slice(i * step * dil, (i + 1) * step * dil))
             for i in range(n // step)]

    def to_res(ref, tok):
        if dil == 1:
            return ref[...].astype(F32)[None]
        if via_swap:
            return jnp.swapaxes(ref[tok, :].astype(F32).reshape(step, dil, LANE), 0, 1)
        stage_ref[...] = ref[...].astype(F32)
        return jnp.stack([stage_ref[pl.ds(r, n, stride=dil), :] for r in range(dil)], axis=0)

    lane3 = lax.broadcasted_iota(jnp.int32, (dil, step, LANE), 2)
    for res, tok in slabs:
        q3 = to_res(q_ref, tok) * (ATT_E ** -0.5)
        qm_ref[0, :, res, :] = jnp.where(lane3 < ATT_E, q3, 0.0).astype(BF16)
        qm_ref[1, :, res, :] = jnp.where(lane3 >= ATT_E, q3, 0.0).astype(BF16)
        ks_ref[:, res, :] = to_res(k_ref, tok).astype(BF16)
        vs_ref[:, res, 0:LANE] = to_res(v_ref, tok).astype(BF16)
    vs_ref[:, :, LANE:2 * LANE] = jnp.ones((dil, n, LANE), BF16)

    nk = ATT_SPAN if nb == 1 else 2 * ATT_SPAN
    a_idx = lax.broadcasted_iota(jnp.int32, (ATT_SPAN, nk), 0)
    c_idx = lax.broadcasted_iota(jnp.int32, (ATT_SPAN, nk), 1)
    for var in range(1 if nb == 1 else 2):
        j = a_idx - c_idx + var * ATT_SPAN
        valid = (j >= 0) & (j <= ATT_SPAN)
        for h in range(2):
            scale = jnp.where(hp == 0, -slopes[h] * dil, -slopes[2 + h] * dil)
            bias_ref[h, var] = jnp.where(valid, j.astype(F32) * scale, NEG_INF)

    lane = lax.broadcasted_iota(jnp.int32, (ATT_SPAN, LANE), 1)
    hm = [lane < ATT_E, lane >= ATT_E]

    def slices(t):
        r, qi = t // nb, t % nb
        qsl = pl.ds(pl.multiple_of(qi * ATT_SPAN, ATT_SPAN), ATT_SPAN)
        if nb > 1:
            return r, qsl, pl.ds(pl.multiple_of(jnp.maximum(qi - 1, 0) * ATT_SPAN, ATT_SPAN), nk), jnp.minimum(qi, 1)
        return r, qsl, qsl, 0

    def scores(t, carry):
        r, qsl, ksl, var = slices(t)
        kk = ks_ref[r, ksl, :]
        for h in range(2):
            s = _dot_nt(qm_ref[h, r, qsl, :], kk) + bias_ref[h, var]
            s_ref[t, h] = s
            m_ref[t, h] = jnp.max(s, axis=-1, keepdims=True)
        return carry

    def outputs(t, carry):
        r, qsl, ksl, _ = slices(t)
        vv = vs_ref[r, ksl, :]
        o_acc = jnp.zeros((ATT_SPAN, LANE), F32)
        m_acc = jnp.zeros((ATT_SPAN, LANE), F32)
        d_acc = jnp.zeros((ATT_SPAN, LANE), F32)
        for h in range(2):
            m = m_ref[t, h]
            od = _dot(jnp.exp(s_ref[t, h] - m).astype(BF16), vv)
            o_acc = jnp.where(hm[h], od[:, :LANE], o_acc)
            d_acc = jnp.where(hm[h], od[:, LANE:], d_acc)
            m_acc = jnp.where(hm[h], m, m_acc)
        os_ref[r, qsl, :] = o_acc
        ms_ref[r, qsl, :] = m_acc
        ds_ref[r, qsl, :] = d_acc
        return carry

    lax.fori_loop(0, dil * nb, scores, 0, unroll=ATT_UNROLL)
    lax.fori_loop(0, dil * nb, outputs, 0, unroll=ATT_UNROLL)
    for res_ref, out_ref in ((os_ref, o_ref), (ms_ref, mo_ref), (ds_ref, do_ref)):
        if dil == 1:
            out_ref[...] = res_ref[0]
        elif via_swap:
            for res, tok in slabs:
                out_ref[tok, :] = jnp.swapaxes(res_ref[:, res, :], 0, 1).reshape(step * dil, LANE)
        else:
            for r in range(dil):
                out_ref[pl.ds(r, n, stride=dil), :] = res_ref[r]


def _attn_prompt(a3, gi):
    bsz, t_len, _ = a3.shape
    win, dil = ATT_GROUPS[gi]
    keep = min(win, t_len)
    kern = functools.partial(_attn_prompt_kernel, t_len=t_len, keep=keep, dil=dil,
                             slopes=[float(s) for s in _SLOPES[gi * ATT_HG:(gi + 1) * ATT_HG]])
    uspec = lambda off: pl.BlockSpec((None, t_len, LANE), lambda b, hp: (b, 0, (off + gi * ATT_GW) // LANE + hp))
    ospec = lambda rows: pl.BlockSpec((None, rows, LANE), lambda b, hp: (b, 0, hp))
    n = t_len // dil
    nk = ATT_SPAN if n == ATT_SPAN else 2 * ATT_SPAN
    nblk = t_len // ATT_SPAN
    scratch = [pltpu.VMEM((2, dil, n, LANE), BF16), pltpu.VMEM((dil, n, LANE), BF16),
               pltpu.VMEM((dil, n, 2 * LANE), BF16), pltpu.VMEM((dil, n, LANE), F32),
               pltpu.VMEM((dil, n, LANE), F32), pltpu.VMEM((dil, n, LANE), F32),
               pltpu.VMEM((2, 2, ATT_SPAN, nk), F32),
               pltpu.VMEM((nblk, 2, ATT_SPAN, nk), F32), pltpu.VMEM((nblk, 2, ATT_SPAN, 1), F32),
               pltpu.VMEM((t_len, LANE), F32)]
    o, mx, dn, kc, vc = pl.pallas_call(
        kern,
        grid=(bsz, ATT_GW // LANE),
        in_specs=[uspec(A_Q), uspec(A_K), uspec(A_V)],
        out_specs=[ospec(t_len)] * 3 + [ospec(keep)] * 2,
        out_shape=[jax.ShapeDtypeStruct((bsz, t_len, ATT_GW), F32)] * 3
                  + [jax.ShapeDtypeStruct((bsz, keep, ATT_GW), F32)] * 2,
        scratch_shapes=scratch,
        compiler_params=_params(("parallel", "parallel")),
        name=f"attn_prompt_g{gi}",
    )(a3, a3, a3)
    m = bsz * t_len
    return ([o.reshape(m, ATT_GW), mx.reshape(m, ATT_GW), dn.reshape(m, ATT_GW)],
            kc.reshape(bsz, keep, ATT_HG, ATT_E), vc.reshape(bsz, keep, ATT_HG, ATT_E))


_KX_PAD = LANE


def _head_masks(rows):
    lane = lax.broadcasted_iota(jnp.int32, (rows, ATT_GW), 1)
    return [(lane >= h * ATT_E) & (lane < (h + 1) * ATT_E) for h in range(ATT_HG)]


def _attn_sample_kernel(*refs, t_len, first, layer_slopes):
    ng = len(ATT_GROUPS)
    qkv = refs[0:3 * ng]
    caches = refs[3 * ng:5 * ng]
    outs = refs[5 * ng:] if first else refs[7 * ng:]
    o_refs, m_refs, d_refs, cache_out = outs[0:ng], outs[ng:2 * ng], outs[2 * ng:3 * ng], outs[3 * ng:5 * ng]
    kx_ref, vx_ref = outs[5 * ng:]

    def shift_in(c_ref, new_t):
        ext = jnp.concatenate([c_ref[...], new_t], axis=1)
        return ext, pltpu.roll(ext, ext.shape[1] - t_len, axis=1)

    def attend():
        hm = _head_masks(t_len)
        rows = ATT_HG * t_len
        zrows = jnp.zeros((_KX_PAD - t_len, ATT_GW), F32)
        for gi, (win, dil) in enumerate(ATT_GROUPS):
            q_ref, kn_ref, vn_ref = qkv[3 * gi:3 * gi + 3]
            wb = caches[2 * gi].shape[1]
            nk = wb + _KX_PAD
            for c_ref, n_ref, x_ref, co_ref in ((caches[2 * gi], kn_ref, kx_ref, cache_out[2 * gi]),
                                                (caches[2 * gi + 1], vn_ref, vx_ref, cache_out[2 * gi + 1])):
                new_t = jnp.concatenate([n_ref[...], zrows], axis=0).T
                ext, moved = shift_in(c_ref, new_t)
                x_ref[:, 0:nk] = ext.astype(BF16)
                co_ref[...] = moved[:, 0:wb] if first else moved[:, wb - LANE:wb]
            q = q_ref[...] * (ATT_E ** -0.5)
            qst = jnp.concatenate([jnp.where(hm[h], q, 0.0) for h in range(ATT_HG)], axis=0).astype(BF16)
            s = _dot(qst, kx_ref[:, 0:nk])
            ri = lax.broadcasted_iota(jnp.int32, (rows, nk), 0)
            ci = lax.broadcasted_iota(jnp.int32, (rows, nk), 1)
            dist = wb + (ri % t_len) - ci
            valid = (dist >= 0) & (dist <= win) & ((dist & (dil - 1)) == 0)
            hrow = lax.broadcasted_iota(jnp.int32, (rows, 1), 0) // t_len
            slope = jnp.zeros((rows, 1), F32)
            for h in range(ATT_HG):
                slope = jnp.where(hrow == h, float(layer_slopes[gi * ATT_HG + h]), slope)
            s = jnp.where(valid, s - slope * dist.astype(F32), NEG_INF)
            m = jnp.max(s, axis=-1, keepdims=True)
            p = jnp.exp(s - m)
            den = jnp.sum(p, axis=-1, keepdims=True)
            ost = _dot_nt(p.astype(BF16), vx_ref[:, 0:nk])
            o = jnp.zeros((t_len, ATT_GW), F32)
            mx = jnp.zeros((t_len, ATT_GW), F32)
            dn = jnp.zeros((t_len, ATT_GW), F32)
            for h in range(ATT_HG):
                hrows = slice(h * t_len, (h + 1) * t_len)
                o = jnp.where(hm[h], ost[hrows], o)
                mx = jnp.where(hm[h], m[hrows], mx)
                dn = jnp.where(hm[h], den[hrows], dn)
            o_refs[gi][...] = o
            m_refs[gi][...] = mx
            d_refs[gi][...] = dn

    if not first:
        attend()
        return
    pl.when(pl.program_id(1) == 0)(attend)

    @pl.when(pl.program_id(1) > 0)
    def _():
        for c_ref, co_ref in zip(caches, cache_out):
            _, moved = shift_in(c_ref, jnp.zeros((ATT_GW, _KX_PAD), F32))
            co_ref[...] = moved[:, 0:c_ref.shape[1]]


def _attn_sample(a3, caches_t, layer, cache_prev):
    bsz, t_len, _ = a3.shape
    ng = len(ATT_GROUPS)
    first = cache_prev is None
    kern = functools.partial(_attn_sample_kernel, t_len=t_len, first=first,
                             layer_slopes=[float(s) for s in _SLOPES])
    in_specs, args = [], []
    for gi in range(ng):
        for off in (A_Q, A_K, A_V):
            in_specs.append(pl.BlockSpec((None, t_len, ATT_GW), lambda b, *_, o=off // ATT_GW + gi: (b, 0, o)))
            args.append(a3)
    small = pl.BlockSpec((None, t_len, ATT_GW), lambda b, *_: (b, 0, 0))
    out_specs = [small] * (3 * ng)
    out_shape = [jax.ShapeDtypeStruct((bsz, t_len, ATT_GW), F32)] * (3 * ng)
    wbs = [c.shape[3] for c in caches_t]
    for c, wb in zip(caches_t, wbs):
        whole = (None, None, ATT_GW, wb)
        if first:
            in_specs.append(pl.BlockSpec(whole, lambda b, l: (l, b, 0, 0)))
            out_specs.append(pl.BlockSpec(whole, lambda b, l: (l, b, 0, 0)))
        else:
            in_specs.append(pl.BlockSpec(whole, lambda b: (layer, b, 0, 0)))
            out_specs.append(pl.BlockSpec((None, None, ATT_GW, LANE), lambda b, t=wb // LANE - 1: (layer, b, 0, t)))
        args.append(c)
        out_shape.append(jax.ShapeDtypeStruct(c.shape, F32))
    aliases = {}
    if not first:
        for i, c in enumerate(cache_prev):
            in_specs.append(pl.BlockSpec(memory_space=pl.ANY))
            args.append(c)
            aliases[5 * ng + i] = 3 * ng + i
    res = pl.pallas_call(
        kern,
        grid=(bsz, DEPTH) if first else (bsz,),
        in_specs=in_specs,
        out_specs=out_specs,
        out_shape=out_shape,
        scratch_shapes=[pltpu.VMEM((ATT_GW, max(wbs) + _KX_PAD), BF16)] * 2,
        input_output_aliases=aliases,
        compiler_params=_params(("parallel", "arbitrary") if first else ("parallel",)),
        name="attn_sample",
    )(*args)
    m = bsz * t_len
    parts = [[res[k * ng + gi].reshape(m, ATT_GW) for k in range(3)] for gi in range(ng)]
    return parts, list(res[3 * ng:])


def _merge_kernel(*refs, tm, pool_t_len):
    it = iter(refs)
    x_ref, og_ref = next(it), next(it)
    if pool_t_len is None:
        p_ref = next(it)
    else:
        u_ref, halo_ref = next(it), next(it)
    att = [(next(it), next(it), next(it)) for _ in ATT_GROUPS]
    g_ref, wg_ref, woa_ref, wob_ref, woc_ref, wout_ref = (next(it) for _ in range(6))
    if pool_t_len is not None:
        pw_ref, ps_ref = next(it), next(it)
    h_ref = next(it)
    x = x_ref[...]
    xn = _rms(x, g_ref[...]).astype(BF16)

    def gate(i):
        return _sigmoid(_dot(xn, wg_ref[:, i * D_MODEL:(i + 1) * D_MODEL]))

    merged = gate(0) * _dot(og_ref[...].astype(BF16), woa_ref[...])
    if pool_t_len is None:
        yb = _dot(p_ref[...].astype(BF16), wob_ref[...])
    else:
        t0 = (pl.program_id(0) * tm) % pool_t_len
        halo = jnp.where(t0 == 0, 0.0, halo_ref[...])
        ext = jnp.concatenate([halo, u_ref[...]], axis=0)
        t_abs = lax.broadcasted_iota(jnp.int32, (tm, 1), 0) + t0
        yb = jnp.zeros_like(x)
        for gi, mixed in enumerate(_pool_mix(ext, t_abs, pw_ref, ps_ref)):
            yb = yb + _dot(mixed.astype(BF16), wob_ref[gi * POOL_GROUP:(gi + 1) * POOL_GROUP, :])
    merged = merged + gate(1) * yb
    ms = [m_ref[...] for _, m_ref, _ in att]
    m = jnp.maximum(jnp.maximum(ms[0], ms[1]), ms[2])
    ws = [jnp.exp(mg - m) for mg in ms]
    inv = 1.0 / sum(d_ref[...] * w for (_, _, d_ref), w in zip(att, ws))
    yc = jnp.zeros_like(x)
    for gi, (o_ref, _, _) in enumerate(att):
        oc = (o_ref[...] * (ws[gi] * inv)).astype(BF16)
        yc = yc + _dot(oc, woc_ref[gi * ATT_GW:(gi + 1) * ATT_GW, :])
    merged = merged + gate(2) * yc
    h_ref[...] = x + _dot(merged.astype(BF16), wout_ref[...])


def _merge(x2, og, p, att, g, wg, woa, wob, woc, wout, tm, pool=None):
    m = x2.shape[0]
    row = lambda w: pl.BlockSpec((tm, w), lambda i: (i, 0))
    in_specs = [row(D_MODEL), row(GLA_VW), row(D_MODEL)]
    args = [x2, og, p]
    if pool is not None:
        hb = tm // POOL_HALO
        in_specs.append(pl.BlockSpec((POOL_HALO, D_MODEL), lambda i: (jnp.maximum(i * hb - 1, 0), 0)))
        args.append(p)
    weights = [g, wg, woa, wob, woc, wout] + ([] if pool is None else list(pool[:2]))
    return pl.pallas_call(
        functools.partial(_merge_kernel, tm=tm, pool_t_len=None if pool is None else pool[2]),
        grid=(m // tm,),
        in_specs=in_specs + [row(ATT_GW)] * (3 * len(att)) + [_resident(a.shape) for a in weights],
        out_specs=row(D_MODEL),
        out_shape=jax.ShapeDtypeStruct((m, D_MODEL), F32),
        compiler_params=_params(("parallel",)),
        name="merge",
    )(*args, *[a for parts in att for a in parts], *weights)


FFN_TF = 256
FFN_HALO = 16


def _ffn_kernel(*refs, t_len, tm, use_halo, has_prev, final):
    it = iter(refs)
    h_ref = next(it)
    halo_ref = next(it) if use_halo else None
    g_ref, wup_ref, cw_ref, cb_ref, wd_ref, fg_ref = (next(it) for _ in range(6))
    c1_ref, c2_ref = (next(it), next(it)) if has_prev else (None, None)
    y_ref, cv_ref = next(it), next(it)
    hn_ref, hh_ref = next(it), next(it)
    a_scr = None if use_halo else next(it)
    ho = FFN_HALO if use_halo else 0
    if use_halo:
        hn_ref[0:ho, :] = _rms(halo_ref[...], g_ref[...]).astype(BF16)
    hn_ref[ho:ho + tm, :] = _rms(h_ref[...], g_ref[...]).astype(BF16)
    t_loc = (lax.broadcasted_iota(jnp.int32, (tm, 1), 0) + pl.program_id(0) * tm) % t_len
    for jt in range(D_FF // FFN_TF):
        cols = slice(jt * FFN_TF, (jt + 1) * FFN_TF)
        a_ext = _dot(hn_ref[...], wup_ref[:, cols])
        a = a_ext[ho:]
        bg = _dot(hn_ref[ho:ho + tm, :], wup_ref[:, D_FF + jt * FFN_TF:D_FF + (jt + 1) * FFN_TF])
        if use_halo:
            a1, a2 = a_ext[ho - 1:ho - 1 + tm], a_ext[ho - 2:ho - 2 + tm]
        else:
            a1, a2 = pltpu.roll(a, 1, axis=0), pltpu.roll(a, 2, axis=0)
        a1 = jnp.where(t_loc >= 1, a1, c1_ref[:, cols] if has_prev else 0.0)
        a2 = jnp.where(t_loc >= 2, a2, c2_ref[:, cols] if has_prev else 0.0)
        y = cb_ref[:, cols] + cw_ref[0:1, cols] * a2 + cw_ref[1:2, cols] * a1 + cw_ref[2:3, cols] * a
        hh_ref[:, cols] = ((y * _sigmoid(y)) * bg).astype(BF16)
        if use_halo:
            cv_ref[:, cols] = a[tm - 2:tm]
        else:
            nseq = tm // t_len
            for kk in range(FFN_TF // LANE):
                lc = slice(kk * LANE, (kk + 1) * LANE)
                oc = slice(jt * FFN_TF + kk * LANE, jt * FFN_TF + (kk + 1) * LANE)
                a_scr[kk] = a[:, lc]
                cv_ref[0, :, oc] = a_scr[kk, pl.ds(t_len - 2, nseq, stride=t_len), :]
                cv_ref[1, :, oc] = a_scr[kk, pl.ds(t_len - 1, nseq, stride=t_len), :]
    out = h_ref[...] + _dot(hh_ref[...], wd_ref[...])
    if final:
        out = _rms(out, fg_ref[...])
    y_ref[...] = out


def _ffn(h2, g, wup, cw, cb, wd, fg, prev, t_len, tm, final):
    m = h2.shape[0]
    bsz = m // t_len
    use_halo = tm % t_len != 0
    has_prev = prev is not None
    kern = functools.partial(_ffn_kernel, t_len=t_len, tm=tm, use_halo=use_halo, has_prev=has_prev, final=final)
    in_specs = [pl.BlockSpec((tm, D_MODEL), lambda i: (i, 0))]
    args = [h2]
    if use_halo:
        hb = tm // FFN_HALO
        in_specs.append(pl.BlockSpec((FFN_HALO, D_MODEL), lambda i: (jnp.maximum(i * hb - 1, 0), 0)))
        args.append(h2)
    in_specs += [_resident(a.shape) for a in (g, wup, cw, cb, wd, fg)]
    args += [g, wup, cw, cb, wd, fg]
    if has_prev:
        z = jnp.zeros((bsz, t_len - 2, D_FF), F32)
        c1 = jnp.concatenate([prev[:, 1:2], jnp.zeros((bsz, 1, D_FF), F32), z], axis=1).reshape(m, D_FF)
        c2 = jnp.concatenate([prev, z], axis=1).reshape(m, D_FF)
        in_specs += [pl.BlockSpec((tm, D_FF), lambda i: (i, 0))] * 2
        args += [c1, c2]
    scratch = [pltpu.VMEM(((FFN_HALO if use_halo else 0) + tm, D_MODEL), BF16), pltpu.VMEM((tm, D_FF), BF16)]
    if use_halo:
        cv_spec = pl.BlockSpec((None, 2, D_FF), lambda i: (i, 0, 0))
        cv_shape = jax.ShapeDtypeStruct((m // tm, 2, D_FF), F32)
    else:
        nseq = tm // t_len
        cv_spec = pl.BlockSpec((2, nseq, D_FF), lambda i: (0, i, 0))
        cv_shape = jax.ShapeDtypeStruct((2, bsz, D_FF), F32)
        scratch.append(pltpu.VMEM((FFN_TF // LANE, tm, LANE), F32))
    y, cv = pl.pallas_call(
        kern,
        grid=(m // tm,),
        in_specs=in_specs,
        out_specs=[pl.BlockSpec((tm, D_MODEL), lambda i: (i, 0)), cv_spec],
        out_shape=[jax.ShapeDtypeStruct((m, D_MODEL), F32), cv_shape],
        scratch_shapes=scratch,
        compiler_params=_params(("parallel",)),
        name="ffn",
    )(*args)
    if use_halo:
        per = t_len // tm
        cv = cv[per - 1::per]
    else:
        cv = jnp.swapaxes(cv, 0, 1)
    return y, cv


W_SPLIT_ROWS = 256


def _split_w_in_kernel(w_ref, w1_ref, wal_ref, wu_ref, wa_ref, wg_ref):
    o = _IN_OFF
    w1_ref[...] = w_ref[:, o[0]:o[4]].astype(BF16)
    pad = jnp.zeros((w_ref.shape[0], LANE - GLA_GATE_RANK), F32)
    wal_ref[...] = jnp.concatenate([w_ref[:, o[4]:o[5]], pad], axis=1).astype(BF16)
    wu_ref[...] = w_ref[:, o[5]:o[6]].astype(BF16)
    wa_ref[...] = w_ref[:, o[6]:o[9]].astype(BF16)
    wg_ref[...] = w_ref[:, o[9]:o[12]].astype(BF16)


def _split_w_in(w_in):
    depth, d, n = w_in.shape
    widths = (ZG_W, LANE, D_MODEL, A_W, 3 * D_MODEL)
    return pl.pallas_call(
        _split_w_in_kernel,
        grid=(depth, d // W_SPLIT_ROWS),
        in_specs=[pl.BlockSpec((None, W_SPLIT_ROWS, n), lambda l, i: (l, i, 0))],
        out_specs=[pl.BlockSpec((None, W_SPLIT_ROWS, w), lambda l, i: (l, i, 0)) for w in widths],
        out_shape=[jax.ShapeDtypeStruct((depth, d, w), BF16) for w in widths],
        compiler_params=_params(("parallel", "parallel")),
        name="split_w_in",
    )(w_in)


def _prep_weights(w_in, gla_wa2, pool_w, w_oa, w_ob, w_oc, w_out, ffn_w_up, ffn_w_down):
    cast = lambda a: a.astype(BF16)
    w1, wal, wu, wa, wg = _split_w_in(w_in)
    wa2p = cast(jnp.pad(gla_wa2, ((0, 0), (0, LANE - GLA_GATE_RANK), (0, 0))))
    return (w1, wal, wu, wa, wg, wa2p, cast(pool_w), cast(w_oa), cast(w_ob), cast(w_oc), cast(w_out),
            cast(ffn_w_up), cast(ffn_w_down))


def _cache_channel_major(c):
    d, b, w = c.shape[:3]
    return jnp.transpose(c, (0, 1, 3, 4, 2)).reshape(d, b, ATT_GW, w)


def _cache_token_major(ct):
    d, b, _, w = ct.shape
    return jnp.transpose(ct.reshape(d, b, ATT_HG, ATT_E, w), (0, 1, 4, 2, 3))


def _run_group(x, states, weights, small, tiles):
    bsz, t_len, _ = x.shape
    m = bsz * t_len
    prompt = states is None
    act = _act_dtype(t_len)
    w1, wal, wu, wa, wg, wa2p, pw, woa, wob, woc, wout, wup, wdn = weights
    norm1_g, norm2_g, gla_ba, gla_norm_g, pool_scale, conv_w, conv_b, final_g = small
    x2 = x.reshape(m, D_MODEL)
    new_gla, new_pool, new_kv, new_conv = [], [], [], []
    kv_t = None if prompt else [_cache_channel_major(c) for c in states["kv"]]
    kv_run = None
    for l in range(DEPTH):
        al, zg, u, a = _in_proj(x2, norm1_g[l][None], wal[l],
                                [(w1[l], ZG_TN, act), (wu[l], U_TN, F32), (wa[l], A_TN, act)], tiles["in_tm"], act)
        zg3 = zg.reshape(bsz, t_len, ZG_W)
        al3 = al.reshape(bsz, t_len, LANE)
        u3 = u.reshape(bsz, t_len, D_MODEL)
        a3 = a.reshape(bsz, t_len, A_W)
        if prompt:
            s0, s0_layer = jnp.zeros((1, bsz, GLA_HEADS, GLA_DK, GLA_DV), F32), 0
            prev = jnp.zeros((bsz, POOL_HALO, D_MODEL), F32)
            n_prev = 0
        else:
            s0, s0_layer = states["gla"], l
            prev = jnp.pad(states["pool"][l], ((0, 0), (POOL_HALO - POOL_STATE, 0), (0, 0)))
            n_prev = POOL_STATE
        og, g_new = _gla(zg3, al3, wa2p[l], gla_ba[l][None], gla_norm_g[l][None], s0, s0_layer, t_len,
                         tiles["gla_hps"])
        if prompt:
            pb, p_new, fused_pool = u, u3[:, t_len - POOL_STATE:], (pw[l], pool_scale[l][None], t_len)
        else:
            pb, p_new = _pool(u3, prev, pw[l], pool_scale[l][None], t_len, n_prev)
            fused_pool = None
        if prompt:
            att, kv = [], []
            for gi in range(len(ATT_GROUPS)):
                parts, kc, vc = _attn_prompt(a3, gi)
                att.append(parts)
                kv += [kc, vc]
        else:
            att, kv_run = _attn_sample(a3, kv_t, l, kv_run)
            kv = None
        h2 = _merge(x2, og.reshape(m, GLA_VW), pb.reshape(m, D_MODEL), att, norm1_g[l][None], wg[l],
                    woa[l], wob[l], woc[l], wout[l], tiles["merge_tm"], fused_pool)
        x2, c_new = _ffn(h2, norm2_g[l][None], wup[l], conv_w[l], conv_b[l][None], wdn[l], final_g[None],
                         None if prompt else states["conv"][l], t_len, tiles["ffn_tm"], l == DEPTH - 1)
        new_gla.append(g_new)
        new_pool.append(p_new)
        new_kv.append(kv)
        new_conv.append(c_new)
    if prompt:
        kv_out = [jnp.stack([kvl[i] for kvl in new_kv], axis=0) for i in range(2 * len(ATT_GROUPS))]
    else:
        kv_out = [_cache_token_major(c) for c in kv_run]
    return (x2.reshape(bsz, t_len, D_MODEL), jnp.stack(new_gla, 0), jnp.stack(new_pool, 0), kv_out,
            jnp.stack(new_conv, 0))


def kernel(x_prompt, x_sample, state_gla, state_pool, cache_k_w128, cache_v_w128, cache_k_w512, cache_v_w512,
           cache_k_w2048, cache_v_w2048, state_ffn_conv, norm1_g, norm2_g, w_in, gla_wa2, gla_ba, gla_norm_g,
           pool_w, pool_scale, w_oa, w_ob, w_oc, w_out, ffn_w_up, ffn_conv_w, ffn_conv_b, ffn_w_down,
           final_norm_g):
    weights = _prep_weights(w_in, gla_wa2, pool_w, w_oa, w_ob, w_oc, w_out, ffn_w_up, ffn_w_down)
    small = (norm1_g, norm2_g, gla_ba, gla_norm_g, pool_scale, ffn_conv_w, ffn_conv_b, final_norm_g)
    y_p, gla_p, pool_p, kv_p, conv_p = _run_group(
        x_prompt, None, weights, small, dict(in_tm=512, merge_tm=512, ffn_tm=1024, gla_hps=1))
    states = dict(gla=state_gla, pool=state_pool, conv=state_ffn_conv,
                  kv=[cache_k_w128, cache_v_w128, cache_k_w512, cache_v_w512, cache_k_w2048, cache_v_w2048])
    m_s = x_sample.shape[0] * x_sample.shape[1]
    y_s, gla_s, pool_s, kv_s, conv_s = _run_group(
        x_sample, states, weights, small, dict(in_tm=m_s, merge_tm=m_s, ffn_tm=m_s, gla_hps=GLA_HEADS))
    k128_p, v128_p, k512_p, v512_p, k2048_p, v2048_p = kv_p
    k128_s, v128_s, k512_s, v512_s, k2048_s, v2048_s = kv_s
    return (y_p, y_s, gla_p, gla_s, pool_p, pool_s, k128_p, k128_s, v128_p, v128_s,
            k512_p, k512_s, v512_p, v512_s, k2048_p, k2048_s, v2048_p, v2048_s, conv_p, conv_s)
```
